```python
import math
import jax
import jax.numpy as jnp
from jax import lax
import numpy as np

D_MODEL = 1024
BATCH = 16
SEQ = 2048
DEPTH = 2

CTX_LEN = 256
GRID_W = 64
NORM_EPS = 1e-6

D_MIX = D_MODEL
RG_WIDTH = D_MIX // 4
SSD_WIDTH = D_MIX // 4
DA_WIDTH = D_MIX - RG_WIDTH - SSD_WIDTH

RG_HEADS = 4
RG_BLOCK = RG_WIDTH // RG_HEADS
RG_CONV = 4
RG_C = 8.0

SSD_HEADDIM = 64
SSD_HEADS = SSD_WIDTH // SSD_HEADDIM
SSD_GROUPS = 2
SSD_STATE = 64
SSD_CONV = 4
SSD_CHUNK = 128
SSD_XBC = SSD_WIDTH + 2 * SSD_GROUPS * SSD_STATE

DA_HEADS = 4
DA_HEAD_DIM = DA_WIDTH // (2 * DA_HEADS)
DA_V_DIM = 2 * DA_HEAD_DIM
ROPE_BASE = 10000.0
Q_BLOCK = 128

D_FF = ((8 * D_MODEL + 3 * 256 - 1) // (3 * 256)) * 256

IN_SIZES = (RG_WIDTH, RG_WIDTH, SSD_WIDTH, SSD_XBC, 2 * SSD_HEADS, DA_WIDTH, DA_WIDTH, DA_HEADS * DA_V_DIM)
D_IN = sum(IN_SIZES)
IN_OFFSETS = tuple(int(v) for v in np.cumsum(IN_SIZES)[:-1])

kernel_name = "hybrid_rglru_ssd_diffattn_dit_block"

F32 = jnp.float32


def rmsnorm(x, g):
    xf = x.astype(F32)
    y = xf * lax.rsqrt(jnp.mean(xf * xf, axis=-1, keepdims=True) + NORM_EPS)
    return (y * g.astype(F32)).astype(x.dtype)


def modulate(x, shift, scale):
    return x * (1 + scale) + shift


def dwconv_centred(x, w, b):
    k = w.shape[0]
    y = lax.conv_general_dilated(x, w[:, None, :], window_strides=(1,),
                                 padding=[(k // 2, k - 1 - k // 2)],
                                 dimension_numbers=("NWC", "WIO", "NWC"),
                                 feature_group_count=x.shape[-1])
    return y + b


def swiglu(x, w_gate, w_up, w_down):
    return (jax.nn.silu(x @ w_gate) * (x @ w_up)) @ w_down


def blockdiag(x, w, b):
    xs = x.reshape(x.shape[:-1] + (RG_HEADS, RG_BLOCK))
    return jnp.einsum("blhi,hij->blhj", xs, w).reshape(x.shape) + b


def rglru_coeffs(xc, w_a, b_a, w_x, b_x, lam):
    r = jax.nn.sigmoid(blockdiag(xc, w_a, b_a).astype(F32))
    i = jax.nn.sigmoid(blockdiag(xc, w_x, b_x).astype(F32))
    log_a = -RG_C * r * jax.nn.softplus(-lam.astype(F32))
    a = jnp.exp(log_a)
    bx = jnp.sqrt(-jnp.expm1(2.0 * log_a)) * (i * xc.astype(F32))
    return a, bx


def linear_scan(a, bx, h0, reverse):
    def combine(left, right):
        a_l, b_l = left
        a_r, b_r = right
        return a_l * a_r, a_r * b_l + b_r
    a_cum, h = lax.associative_scan(combine, (a, bx), axis=1, reverse=reverse)
    return h + a_cum * h0[:, None, :]


def rglru_mixer(x_ctx, g_ctx, x_lat, g_lat, conv_w, conv_b, w_a, b_a, w_x, b_x, lam, need_ctx):
    xc_ctx = dwconv_centred(x_ctx, conv_w, conv_b)
    xc_lat = dwconv_centred(x_lat, conv_w, conv_b)
    h_ctx_sum = 0.0
    h_lat_sum = 0.0
    for direction, rev in enumerate((False, True)):
        a_c, b_c = rglru_coeffs(xc_ctx, w_a[direction], b_a[direction], w_x[direction], b_x[direction], lam[direction])
        h_c = linear_scan(a_c, b_c, jnp.zeros_like(b_c[:, 0]), rev)
        h_end = h_c[:, 0] if rev else h_c[:, -1]
        a_l, b_l = rglru_coeffs(xc_lat, w_a[direction], b_a[direction], w_x[direction], b_x[direction], lam[direction])
        h_lat_sum = h_lat_sum + linear_scan(a_l, b_l, h_end, rev)
        if need_ctx:
            h_ctx_sum = h_ctx_sum + h_c
    y_lat = (h_lat_sum * jax.nn.gelu(g_lat.astype(F32))).astype(x_lat.dtype)
    y_ctx = (h_ctx_sum * jax.nn.gelu(g_ctx.astype(F32))).astype(x_ctx.dtype) if need_ctx else None
    return y_ctx, y_lat


def segsum(a):
    t = a.shape[-1]
    a_rep = jnp.broadcast_to(a[..., :, None], a.shape + (t,))
    strict = jnp.tril(jnp.ones((t, t), dtype=bool), -1)
    cs = jnp.cumsum(jnp.where(strict, a_rep, 0.0), axis=-2)
    return jnp.where(jnp.tril(jnp.ones((t, t), dtype=bool)), cs, -jnp.inf)


def ssd_scan(x, dt, a_neg, bm, cm, h0):
    b, L, H, P = x.shape
    n_state = bm.shape[-1]
    nc = L // SSD_CHUNK
    xd = (x * dt[..., None]).reshape(b, nc, SSD_CHUNK, H, P)
    bc = bm.reshape(b, nc, SSD_CHUNK, H, n_state)
    cc = cm.reshape(b, nc, SSD_CHUNK, H, n_state)
    a_dt = (dt * a_neg).reshape(b, nc, SSD_CHUNK, H).transpose(0, 3, 1, 2)
    a_cs = jnp.cumsum(a_dt, axis=-1)
    decay = jnp.exp(segsum(a_dt))
    scores = jnp.einsum("bclhn,bcshn->bhcls", cc, bc) * decay
    y_diag = jnp.einsum("bhcls,bcshp->bclhp", scores, xd)
    decay_states = jnp.exp(a_cs[..., -1:] - a_cs)
    states = jnp.einsum("bclhn,bhcl,bclhp->bchpn", bc, decay_states, xd)
    states = jnp.concatenate([h0[:, None], states], axis=1)
    chunk_a = jnp.pad(a_cs[..., -1], ((0, 0), (0, 0), (1, 0)))
    decay_chunk = jnp.exp(segsum(chunk_a))
    new_states = jnp.einsum("bhzc,bchpn->bzhpn", decay_chunk, states)
    entry_states, final_state = new_states[:, :-1], new_states[:, -1]
    y_off = jnp.einsum("bclhn,bchpn,bhcl->bclhp", cc, entry_states, jnp.exp(a_cs))
    return (y_diag + y_off).reshape(b, L, H, P), final_state


def gated_group_rmsnorm(y, z, g):
    b, n = z.shape[:2]
    v = y.reshape(b, n, SSD_WIDTH) * jax.nn.silu(z.astype(F32))
    v = v.reshape(b, n, SSD_GROUPS, SSD_WIDTH // SSD_GROUPS)
    v = v * lax.rsqrt(jnp.mean(v * v, axis=-1, keepdims=True) + NORM_EPS)
    return (v.reshape(b, n, SSD_WIDTH) * g.astype(F32)).astype(z.dtype)


def ssd_mixer(z_ctx, xbc_ctx, dt_ctx, z_lat, xbc_lat, dt_lat, conv_w, conv_b, dt_bias, a_log, d_skip, norm_g, need_ctx):
    def prep(xbc, dt_raw):
        xbc = jax.nn.silu(dwconv_centred(xbc, conv_w, conv_b)).astype(F32)
        b, n = xbc.shape[:2]
        xs, bm, cm = jnp.split(xbc, (SSD_WIDTH, SSD_WIDTH + SSD_GROUPS * SSD_STATE), axis=-1)
        xs = xs.reshape(b, n, SSD_HEADS, SSD_HEADDIM)
        rep = SSD_HEADS // SSD_GROUPS
        bm = jnp.repeat(bm.reshape(b, n, SSD_GROUPS, SSD_STATE), rep, axis=2)
        cm = jnp.repeat(cm.reshape(b, n, SSD_GROUPS, SSD_STATE), rep, axis=2)
        dts = jax.nn.softplus(dt_raw.astype(F32).reshape(b, n, 2, SSD_HEADS) + dt_bias.astype(F32))
        return xs, bm, cm, dts

    xs_c, b_c, c_c, dt_c = prep(xbc_ctx, dt_ctx)
    xs_l, b_l, c_l, dt_l = prep(xbc_lat, dt_lat)
    dsk = d_skip.astype(F32)[:, None]
    y_c = xs_c * dsk
    y_l = xs_l * dsk
    bsz = xs_c.shape[0]
    for direction in range(2):
        if direction == 0:
            flip = lambda t: t
        else:
            flip = lambda t: jnp.flip(t, axis=1)
        a_neg = -jnp.exp(a_log[direction].astype(F32))
        h0 = jnp.zeros((bsz, SSD_HEADS, SSD_HEADDIM, SSD_STATE), F32)
        yc_dir, h_ctx_end = ssd_scan(flip(xs_c), flip(dt_c[:, :, direction]), a_neg, flip(b_c), flip(c_c), h0)
        yl_dir, _ = ssd_scan(flip(xs_l), flip(dt_l[:, :, direction]), a_neg, flip(b_l), flip(c_l), h_ctx_end)
        y_l = y_l + flip(yl_dir)
        if need_ctx:
            y_c = y_c + flip(yc_dir)
    out_l = gated_group_rmsnorm(y_l, z_lat, norm_g)
    out_c = gated_group_rmsnorm(y_c, z_ctx, norm_g) if need_ctx else None
    return out_c, out_l


def axial_rope_tables(row, col):
    half = DA_HEAD_DIM // 2
    inv_freq = jnp.power(ROPE_BASE, -jnp.arange(0, half, 2, dtype=F32) / half)
    ang_r = row.astype(F32)[:, None] * inv_freq
    ang_c = col.astype(F32)[:, None] * inv_freq
    return jnp.cos(ang_r), jnp.sin(ang_r), jnp.cos(ang_c), jnp.sin(ang_c)


def rotate(x, cos, sin):
    x1, x2 = jnp.split(x, 2, axis=-1)
    return jnp.concatenate([x1 * cos - x2 * sin, x1 * sin + x2 * cos], axis=-1)


def apply_axial_rope(x, rope):
    cos_r, sin_r, cos_c, sin_c = (t[:, None, None, :] for t in rope)
    xf = x.astype(F32)
    half = DA_HEAD_DIM // 2
    out = jnp.concatenate([rotate(xf[..., :half], cos_r, sin_r), rotate(xf[..., half:], cos_c, sin_c)], axis=-1)
    return out.astype(x.dtype)


def diff_attn_core(q, k, v, lam):
    s = jnp.einsum("bqhcd,bkhcd->bhcqk", q, k).astype(F32)
    p = jax.nn.softmax(s, axis=-1)
    w = (p[:, :, 0] - lam * p[:, :, 1]).astype(v.dtype)
    return jnp.einsum("bhqk,bkhd->bqhd", w, v)


def diff_attn_mixer(q_ctx, k_ctx, v_ctx, q_lat, k_lat, v_lat, rope, lam_vec, subln_g, layer_idx, need_ctx):
    b, n = q_lat.shape[:2]
    m = q_ctx.shape[1]
    scale = DA_HEAD_DIM ** -0.5
    kc = k_ctx.reshape(b, m, DA_HEADS, 2, DA_HEAD_DIM)
    vc = v_ctx.reshape(b, m, DA_HEADS, DA_V_DIM)
    ql = apply_axial_rope(q_lat.reshape(b, n, DA_HEADS, 2, DA_HEAD_DIM), rope) * scale
    kl = apply_axial_rope(k_lat.reshape(b, n, DA_HEADS, 2, DA_HEAD_DIM), rope)
    vl = v_lat.reshape(b, n, DA_HEADS, DA_V_DIM)
    lam_init = 0.8 - 0.6 * math.exp(-0.3 * layer_idx)
    lv = lam_vec.astype(F32)
    lam = jnp.exp(jnp.sum(lv[0] * lv[1])) - jnp.exp(jnp.sum(lv[2] * lv[3])) + lam_init
    k_all = jnp.concatenate([kl, kc], axis=1)
    v_all = jnp.concatenate([vl, vc], axis=1)
    nb = n // Q_BLOCK
    q_blocks = ql.reshape(b, nb, Q_BLOCK, DA_HEADS, 2, DA_HEAD_DIM).swapaxes(0, 1)
    o_blocks = lax.map(lambda qb: diff_attn_core(qb, k_all, v_all, lam), q_blocks)
    o_lat = o_blocks.swapaxes(0, 1).reshape(b, n, DA_HEADS, DA_V_DIM)

    def finish(o):
        return (rmsnorm(o, subln_g) * (1.0 - lam_init)).reshape(o.shape[0], o.shape[1], DA_WIDTH)

    y_lat = finish(o_lat)
    if need_ctx:
        qc = q_ctx.reshape(b, m, DA_HEADS, 2, DA_HEAD_DIM) * scale
        y_ctx = finish(diff_attn_core(qc, kc, vc, lam))
    else:
        y_ctx = None
    return y_ctx, y_lat


def hybrid_layer(h_ctx, h_lat, mod_ctx, mod_lat, rope, layer_idx, need_ctx,
                 norm1_g, w_in, rg_conv_w, rg_conv_b, rg_w_a, rg_b_a, rg_w_x, rg_b_x, rg_lambda,
                 ssd_conv_w, ssd_conv_b, ssd_dt_bias, ssd_a_log, ssd_d, ssd_norm_g,
                 da_lambda, da_subln_g, w_out, norm2_g, w_gate, w_up, w_down):
    csh1, csc1, cg1, csh2, csc2, cg2 = mod_ctx
    sh1, sc1, g1, sh2, sc2, g2 = mod_lat
    u_ctx = modulate(rmsnorm(h_ctx, norm1_g), csh1, csc1) @ w_in
    u_lat = modulate(rmsnorm(h_lat, norm1_g), sh1, sc1) @ w_in
    rgx_c, rgg_c, sz_c, sxbc_c, sdt_c, q_c, k_c, v_c = jnp.split(u_ctx, IN_OFFSETS, axis=-1)
    rgx_l, rgg_l, sz_l, sxbc_l, sdt_l, q_l, k_l, v_l = jnp.split(u_lat, IN_OFFSETS, axis=-1)

    rg_c, rg_l = rglru_mixer(rgx_c, rgg_c, rgx_l, rgg_l, rg_conv_w, rg_conv_b,
                             rg_w_a, rg_b_a, rg_w_x, rg_b_x, rg_lambda, need_ctx)
    ssd_c, ssd_l = ssd_mixer(sz_c, sxbc_c, sdt_c, sz_l, sxbc_l, sdt_l, ssd_conv_w, ssd_conv_b,
                             ssd_dt_bias, ssd_a_log, ssd_d, ssd_norm_g, need_ctx)
    da_c, da_l = diff_attn_mixer(q_c, k_c, v_c, q_l, k_l, v_l, rope, da_lambda, da_subln_g,
                                 layer_idx, need_ctx)

    mix_lat = jnp.concatenate([rg_l, ssd_l, da_l], axis=-1) @ w_out
    h_lat = h_lat + g1 * mix_lat
    h_lat = h_lat + g2 * swiglu(modulate(rmsnorm(h_lat, norm2_g), sh2, sc2), w_gate, w_up, w_down)
    if need_ctx:
        mix_ctx = jnp.concatenate([rg_c, ssd_c, da_c], axis=-1) @ w_out
        h_ctx = h_ctx + cg1 * mix_ctx
        h_ctx = h_ctx + cg2 * swiglu(modulate(rmsnorm(h_ctx, norm2_g), csh2, csc2), w_gate, w_up, w_down)
    return h_ctx, h_lat


def setup_inputs(seed: int = 0) -> dict:
    key = jax.random.key(seed)
    ks = iter(jax.random.split(key, 48))

    def nrm(shape, scale):
        return jax.random.normal(next(ks), shape, F32) * scale

    def gain(shape):
        return 1.0 + nrm(shape, 0.02)

    a0 = jax.random.uniform(next(ks), (DEPTH, 2, RG_WIDTH), F32, 0.9, 0.999)
    s0 = a0 ** (1.0 / RG_C)
    rg_lambda = jnp.log(s0) - jnp.log1p(-s0)
    dt0 = jnp.exp(jax.random.uniform(next(ks), (DEPTH, 2, SSD_HEADS), F32, math.log(1e-3), math.log(1e-1)))
    ssd_dt_bias = dt0 + jnp.log(-jnp.expm1(-dt0))
    ssd_a_log = jnp.log(jax.random.uniform(next(ks), (DEPTH, 2, SSD_HEADS), F32, 1.0, 16.0))

    return {
        "x": nrm((BATCH, SEQ, D_MODEL), 1.0),
        "c": nrm((BATCH, D_MODEL), 1.0),
        "ctx": nrm((BATCH, CTX_LEN, D_MODEL), 1.0),
        "c_ctx": nrm((D_MODEL,), 1.0),
        "w_mod": nrm((DEPTH, D_MODEL, 6 * D_MODEL), 0.3 * D_MODEL ** -0.5),
        "b_mod": nrm((DEPTH, 6 * D_MODEL), 0.02),
        "norm1_g": gain((DEPTH, D_MODEL)),
        "w_in": nrm((DEPTH, D_MODEL, D_IN), D_MODEL ** -0.5),
        "rg_conv_w": nrm((DEPTH, RG_CONV, RG_WIDTH), RG_CONV ** -0.5),
        "rg_conv_b": nrm((DEPTH, RG_WIDTH), 0.02),
        "rg_w_a": nrm((DEPTH, 2, RG_HEADS, RG_BLOCK, RG_BLOCK), RG_BLOCK ** -0.5),
        "rg_b_a": nrm((DEPTH, 2, RG_WIDTH), 0.02),
        "rg_w_x": nrm((DEPTH, 2, RG_HEADS, RG_BLOCK, RG_BLOCK), RG_BLOCK ** -0.5),
        "rg_b_x": nrm((DEPTH, 2, RG_WIDTH), 0.02),
        "rg_lambda": rg_lambda,
        "ssd_conv_w": nrm((DEPTH, SSD_CONV, SSD_XBC), SSD_CONV ** -0.5),
        "ssd_conv_b": nrm((DEPTH, SSD_XBC), 0.02),
        "ssd_dt_bias": ssd_dt_bias,
        "ssd_a_log": ssd_a_log,
        "ssd_d": gain((DEPTH, SSD_HEADS)),
        "ssd_norm_g": gain((DEPTH, SSD_WIDTH)),
        "da_lambda": nrm((DEPTH, 4, DA_HEAD_DIM), 0.1),
        "da_subln_g": gain((DEPTH, DA_V_DIM)),
        "w_out": nrm((DEPTH, D_MIX, D_MODEL), D_MIX ** -0.5),
        "norm2_g": gain((DEPTH, D_MODEL)),
        "w_gate": nrm((DEPTH, D_MODEL, D_FF), D_MODEL ** -0.5),
        "w_up": nrm((DEPTH, D_MODEL, D_FF), D_MODEL ** -0.5),
        "w_down": nrm((DEPTH, D_FF, D_MODEL), D_FF ** -0.5),
        "final_norm_g": gain((D_MODEL,)),
    }


def reference(x, c, ctx, c_ctx, w_mod, b_mod, norm1_g, w_in, rg_conv_w, rg_conv_b, rg_w_a, rg_b_a,
              rg_w_x, rg_b_x, rg_lambda, ssd_conv_w, ssd_conv_b, ssd_dt_bias, ssd_a_log, ssd_d,
              ssd_norm_g, da_lambda, da_subln_g, w_out, norm2_g, w_gate, w_up, w_down, final_norm_g):
    n = x.shape[1]
    rows = n // GRID_W
    row = jnp.repeat(jnp.arange(rows, dtype=jnp.int32), GRID_W)
    col = jnp.tile(jnp.arange(GRID_W, dtype=jnp.int32), rows)
    rope = axial_rope_tables(row, col)
    h_lat, h_ctx = x, ctx
    for l in range(DEPTH):
        need_ctx = l < DEPTH - 1
        mod_lat = jnp.split((jax.nn.silu(c) @ w_mod[l] + b_mod[l])[:, None, :], 6, axis=-1)
        mod_ctx = jnp.split(jax.nn.silu(c_ctx) @ w_mod[l] + b_mod[l], 6, axis=-1)
        h_ctx, h_lat = hybrid_layer(
            h_ctx, h_lat, mod_ctx, mod_lat, rope, l, need_ctx,
            norm1_g[l], w_in[l], rg_conv_w[l], rg_conv_b[l], rg_w_a[l], rg_b_a[l], rg_w_x[l], rg_b_x[l],
            rg_lambda[l], ssd_conv_w[l], ssd_conv_b[l], ssd_dt_bias[l], ssd_a_log[l], ssd_d[l],
            ssd_norm_g[l], da_lambda[l], da_subln_g[l], w_out[l], norm2_g[l], w_gate[l], w_up[l], w_down[l])
    return rmsnorm(h_lat, final_norm_g)
```

```python
import functools
import math

import jax
import jax.numpy as jnp
from jax import lax
from jax.experimental import pallas as pl
from jax.experimental.pallas import tpu as pltpu

F32 = jnp.float32
BF16 = jnp.bfloat16

D_MODEL = 1024
DEPTH = 2
CTX_LEN = 256
GRID_W = 64
NORM_EPS = 1e-6

RG_WIDTH = 256
RG_HEADS = 4
RG_BLOCK = 64
RG_C = 8.0

SSD_WIDTH = 256
SSD_HEADDIM = 64
SSD_HEADS = 4
SSD_GROUPS = 2
SSD_STATE = 64
SSD_CHUNK = 128
SSD_XBC = 512

DA_WIDTH = 512
DA_HEADS = 4
DA_HEAD_DIM = 64
DA_V_DIM = 128
ROPE_BASE = 10000.0

D_FF = 2816
IN_SIZES = (256, 256, 256, 512, 8, 512, 512, 512)
D_IN = sum(IN_SIZES)

LANES = 128
SUBLANES = 8
TOK_TILE = 256
CONV_PAD = SUBLANES
VMEM_LIMIT = 56 * 1024 * 1024

U_RG = 0
U_SZ = 512
U_XBC = 768
U_Q = 1280
U_K = 1792
U_V = 2304
U_DT = 2816
U_COLS = 2944


def _silu(x):
    return x * jax.nn.sigmoid(x)


def _softplus(x):
    return jnp.maximum(x, 0.0) + jnp.log1p(jnp.exp(-jnp.abs(x)))


def _rms(x, g):
    ms = jnp.mean(x * x, axis=-1, keepdims=True)
    return x * lax.rsqrt(ms + NORM_EPS) * g


def _const_spec(shape):
    nd = len(shape)
    return pl.BlockSpec(shape, lambda *_: (0,) * nd, pipeline_mode=pl.Buffered(1))


def _params(*sem):
    return pltpu.CompilerParams(dimension_semantics=sem, vmem_limit_bytes=VMEM_LIMIT)


def _mod_kernel(c_ref, w_ref, b_ref, o_ref):
    c = c_ref[...]
    a = _silu(c).astype(BF16)
    o_ref[...] = jnp.dot(a, w_ref[...].astype(BF16), preferred_element_type=F32) + b_ref[...]


def _modulation(c_all, w_mod, b_mod):
    r = c_all.shape[0]
    out = pl.pallas_call(
        _mod_kernel,
        grid=(DEPTH, 6),
        in_specs=[
            pl.BlockSpec((r, D_MODEL), lambda l, j: (0, 0)),
            pl.BlockSpec((None, D_MODEL, D_MODEL), lambda l, j: (l, 0, j)),
            pl.BlockSpec((None, 1, D_MODEL), lambda l, j: (l, 0, j)),
        ],
        out_specs=pl.BlockSpec((None, r, D_MODEL), lambda l, j: (l, 0, j)),
        out_shape=jax.ShapeDtypeStruct((DEPTH, r, 6 * D_MODEL), F32),
        compiler_params=_params("arbitrary", "arbitrary"),
    )(c_all, w_mod, b_mod.reshape(DEPTH, 1, 6 * D_MODEL))
    return out.reshape(DEPTH, r, 6, D_MODEL)


def _rope(x, cos, sin, first_half):
    width = x.shape[-1]
    partner = jnp.where(first_half, pltpu.roll(x, width - 16, 1), pltpu.roll(x, 16, 1))
    return x * cos + partner * sin


def _inproj_kernel(h_ref, mod_ref, g_ref, w_ref, cq_ref, sq_ref, ck_ref, sk_ref,
                   rg_ref, sz_ref, xbc_ref, dt_ref, q_ref, k_ref, v_ref):
    mod = mod_ref[...]
    y = _rms(h_ref[...], g_ref[...])
    y = y * (1.0 + mod[1:2]) + mod[0:1]
    u = jnp.dot(y.astype(BF16), w_ref[...], preferred_element_type=F32)
    rg_ref[...] = u[:, U_RG:U_SZ]
    sz_ref[...] = u[:, U_SZ:U_XBC]
    xbc_ref[...] = u[:, U_XBC:U_Q]
    dt_ref[...] = u[:, U_DT:U_COLS]
    q = u[:, U_Q:U_K]
    k = u[:, U_K:U_V]
    lane = lax.broadcasted_iota(jnp.int32, q.shape, 1)
    first_half = (lane % 32) < 16
    q_ref[...] = _rope(q, cq_ref[...], sq_ref[...], first_half).astype(BF16)
    k_ref[...] = _rope(k, ck_ref[...], sk_ref[...], first_half).astype(BF16)
    v_ref[...] = u[:, U_V:U_DT].astype(BF16)


def _in_proj(h, mod, norm_g, w_in_p, tables):
    b, l, _ = h.shape
    nt = l // TOK_TILE
    tok = lambda w: pl.BlockSpec((None, TOK_TILE, w), lambda t, i: (i, t, 0))
    tab = pl.BlockSpec((TOK_TILE, DA_WIDTH), lambda t, i: (t, 0))
    widths = (512, 256, 512, 128, 512, 512, 512)
    dtypes = (F32, F32, F32, F32, BF16, BF16, BF16)
    return pl.pallas_call(
        _inproj_kernel,
        grid=(nt, b),
        in_specs=[
            tok(D_MODEL),
            pl.BlockSpec((None, 6, D_MODEL), lambda t, i: (jnp.where(t == 0, b, i), 0, 0)),
            _const_spec((1, D_MODEL)),
            _const_spec((D_MODEL, U_COLS)),
            tab, tab, tab, tab,
        ],
        out_specs=[tok(w) for w in widths],
        out_shape=[jax.ShapeDtypeStruct((b, l, w), d) for w, d in zip(widths, dtypes)],
        compiler_params=_params("arbitrary", "arbitrary"),
    )(h, mod, norm_g, w_in_p, *tables)


def _conv_rows(xp_ref, r0, rows, w, bias):
    assert CTX_LEN % rows == 0
    span = rows + 2 * CONV_PAD
    xa = xp_ref[pl.ds(r0, span), :]
    lr = lax.broadcasted_iota(jnp.int32, (rows, xa.shape[1]), 0)
    acc = None
    for tap in range(4):
        off = tap - 2
        sh = xa if off == 0 else pltpu.roll(xa, (-off) % span, 0)
        val = sh[CONV_PAD:CONV_PAD + rows]
        if off < 0:
            val = jnp.where(lr >= jnp.where(r0 == CTX_LEN, -off, 0), val, 0.0)
        elif off > 0:
            val = jnp.where(lr < jnp.where(r0 + rows == CTX_LEN, rows - off, rows), val, 0.0)
        term = val * w[tap:tap + 1]
        acc = term if acc is None else acc + term
    return acc + bias


def _fill_padded(xp_ref, x_ref, col0, width, l):
    zeros = jnp.zeros((CONV_PAD, width), F32)
    xp_ref[0:CONV_PAD, :] = zeros
    xp_ref[CONV_PAD + l:2 * CONV_PAD + l, :] = zeros
    xp_ref[CONV_PAD:CONV_PAD + l, :] = x_ref[:, col0:col0 + width]


RG_ROWS = 256


def _scan8(a, bx, reverse):
    row = lax.broadcasted_iota(jnp.int32, a.shape, 0)
    for s in (1, 2, 4):
        shift = (SUBLANES - s) if reverse else s
        a_sh = pltpu.roll(a, shift, 0)
        b_sh = pltpu.roll(bx, shift, 0)
        ok = (row < SUBLANES - s) if reverse else (row >= s)
        bx = jnp.where(ok, a * b_sh + bx, bx)
        a = jnp.where(ok, a * a_sh, a)
    return a, bx


def _rglru_kernel(rg_ref, cw_ref, cb_ref, wg_ref, bg_ref, lam_ref, o_ref,
                  xp_ref, af_ref, bf_ref, ab_ref, bb_ref):
    l = rg_ref.shape[0]
    _fill_padded(xp_ref, rg_ref, 0, RG_WIDTH, l)
    cw = cw_ref[...]
    cb = cb_ref[...]
    bg = bg_ref[...]
    sp = _softplus(-lam_ref[...])

    def coeffs(i, carry):
        r0 = pl.multiple_of(i * RG_ROWS, RG_ROWS)
        xc = _conv_rows(xp_ref, r0, RG_ROWS, cw, cb)
        gates = jnp.dot(xc.astype(BF16), wg_ref[...], preferred_element_type=F32) + bg
        for d, (a_ref, b_ref) in enumerate(((af_ref, bf_ref), (ab_ref, bb_ref))):
            rr = jax.nn.sigmoid(gates[:, (2 * d) * RG_WIDTH:(2 * d + 1) * RG_WIDTH])
            ii = jax.nn.sigmoid(gates[:, (2 * d + 1) * RG_WIDTH:(2 * d + 2) * RG_WIDTH])
            log_a = -RG_C * rr * sp[d:d + 1]
            a = jnp.exp(log_a)
            one_m_a2 = -jnp.tanh(log_a) * (a * a + 1.0)
            a_ref[pl.ds(r0, RG_ROWS), :] = a
            b_ref[pl.ds(r0, RG_ROWS), :] = jnp.sqrt(one_m_a2) * (ii * xc)
        return carry

    lax.fori_loop(0, l // RG_ROWS, coeffs, 0)

    nblk = l // SUBLANES
    nctx = CTX_LEN // SUBLANES

    def scan(j, carry):
        hf, hb = carry
        rf = pl.multiple_of(j * SUBLANES, SUBLANES)
        jb = jnp.where(j < nctx, nctx - 1 - j, nblk + nctx - 1 - j)
        rb = pl.multiple_of(jb * SUBLANES, SUBLANES)
        a, bx = _scan8(af_ref[pl.ds(rf, SUBLANES), :], bf_ref[pl.ds(rf, SUBLANES), :], False)
        h = a * hf + bx
        af_ref[pl.ds(rf, SUBLANES), :] = h
        hf = h[SUBLANES - 1:SUBLANES]
        a, bx = _scan8(ab_ref[pl.ds(rb, SUBLANES), :], bb_ref[pl.ds(rb, SUBLANES), :], True)
        h = a * hb + bx
        ab_ref[pl.ds(rb, SUBLANES), :] = h
        hb = h[0:1]
        return hf, hb

    zero = jnp.zeros((1, RG_WIDTH), F32)
    lax.fori_loop(0, nblk, scan, (zero, zero), unroll=4)

    def finish(i, carry):
        r0 = pl.multiple_of(i * RG_ROWS, RG_ROWS)
        hsum = af_ref[pl.ds(r0, RG_ROWS), :] + ab_ref[pl.ds(r0, RG_ROWS), :]
        g = rg_ref[pl.ds(r0, RG_ROWS), RG_WIDTH:2 * RG_WIDTH]
        o_ref[pl.ds(r0, RG_ROWS), :] = (hsum * jax.nn.gelu(g, approximate=True)).astype(BF16)
        return carry

    lax.fori_loop(0, l // RG_ROWS, finish, 0)


def _rglru(rg, conv_w, conv_b, w_gates, b_gates, lam):
    b, l, _ = rg.shape
    seq = lambda w: pl.BlockSpec((None, l, w), lambda i: (i, 0, 0))
    scratch = [pltpu.VMEM((l + 2 * CONV_PAD, RG_WIDTH), F32)] + [pltpu.VMEM((l, RG_WIDTH), F32)] * 4
    return pl.pallas_call(
        _rglru_kernel,
        grid=(b,),
        in_specs=[
            seq(2 * RG_WIDTH),
            _const_spec((4, RG_WIDTH)),
            _const_spec((1, RG_WIDTH)),
            _const_spec((RG_WIDTH, 4 * RG_WIDTH)),
            _const_spec((1, 4 * RG_WIDTH)),
            _const_spec((2, RG_WIDTH)),
        ],
        out_specs=seq(RG_WIDTH),
        out_shape=jax.ShapeDtypeStruct((b, l, RG_WIDTH), BF16),
        scratch_shapes=scratch,
        compiler_params=_params("arbitrary"),
    )(rg, conv_w, conv_b, w_gates, b_gates, lam)


SSD_ROWS = 256
GROUP_W = SSD_WIDTH // SSD_GROUPS


def _ssd_chunk(xbc_ref, dts_ref, y_ref, r0, state, aneg, direction, tri, tri_t, mask):
    q = SSD_CHUNK
    x = xbc_ref[pl.ds(r0, q), 0:SSD_WIDTH]
    bm = xbc_ref[pl.ds(r0, q), SSD_WIDTH:SSD_WIDTH + LANES]
    cm = xbc_ref[pl.ds(r0, q), SSD_WIDTH + LANES:SSD_XBC]
    dt = dts_ref[pl.ds(r0, q), :]
    a = dt * aneg
    hi = lax.Precision.HIGHEST
    c_col = jnp.dot(tri, a, precision=hi, preferred_element_type=F32)
    c_row = jnp.dot(a.T, tri_t, precision=hi, preferred_element_type=F32)
    tot_row = jnp.sum(a, axis=0, keepdims=True)
    tot_col = jnp.sum(a.T, axis=1, keepdims=True)
    bm_t = bm.T
    cm16 = cm.astype(BF16)
    bm16 = bm.astype(BF16)
    state16 = state.astype(BF16)
    ys = []
    new_state = []
    for h in range(SSD_HEADS):
        g = h // (SSD_HEADS // SSD_GROUPS)
        li = direction * SSD_HEADS + h
        cc = c_col[:, li:li + 1]
        cr = c_row[li:li + 1, :]
        decay = jnp.exp(jnp.where(mask, cc - cr, -1e30))
        cg = cm16[:, g * SSD_STATE:(g + 1) * SSD_STATE]
        bg = bm16[:, g * SSD_STATE:(g + 1) * SSD_STATE]
        gram = lax.dot_general(cg, bg, (((1,), (1,)), ((), ())), preferred_element_type=F32)
        xd = (x[:, h * SSD_HEADDIM:(h + 1) * SSD_HEADDIM] * dt[:, li:li + 1]).astype(BF16)
        y_diag = jnp.dot((gram * decay).astype(BF16), xd, preferred_element_type=F32)
        s_h = state[:, h * SSD_HEADDIM:(h + 1) * SSD_HEADDIM]
        y_off = jnp.exp(cc) * jnp.dot(cg, state16[:, h * SSD_HEADDIM:(h + 1) * SSD_HEADDIM],
                                      preferred_element_type=F32)
        ys.append(y_diag + y_off)
        w_bt = bm_t[g * SSD_STATE:(g + 1) * SSD_STATE, :] * jnp.exp(tot_col[li:li + 1, :] - cr)
        ds = jnp.dot(w_bt.astype(BF16), xd, preferred_element_type=F32)
        new_state.append(jnp.exp(tot_row[:, li:li + 1]) * s_h + ds)
    y_ref[pl.ds(r0, q), :] += jnp.concatenate(ys, axis=1)
    return jnp.concatenate(new_state, axis=1)


def _ssd_kernel(z_ref, xbc_in_ref, dt_ref, cw_ref, cb_ref, dtb_ref, aneg_ref, dsk_ref, ng_ref, o_ref,
                xp_ref, xbc_ref, dts_ref, y_ref):
    l = z_ref.shape[0]
    _fill_padded(xp_ref, xbc_in_ref, 0, SSD_XBC, l)
    cw = cw_ref[...]
    cb = cb_ref[...]
    dtb = dtb_ref[...]
    dsk = dsk_ref[...]

    def prep(i, carry):
        r0 = pl.multiple_of(i * SSD_ROWS, SSD_ROWS)
        xbc = _silu(_conv_rows(xp_ref, r0, SSD_ROWS, cw, cb))
        xbc_ref[pl.ds(r0, SSD_ROWS), :] = xbc
        dts_ref[pl.ds(r0, SSD_ROWS), :] = _softplus(dt_ref[pl.ds(r0, SSD_ROWS), :] + dtb)
        y_ref[pl.ds(r0, SSD_ROWS), :] = xbc[:, 0:SSD_WIDTH] * dsk
        return carry

    lax.fori_loop(0, l // SSD_ROWS, prep, 0)

    q = SSD_CHUNK
    ri = lax.broadcasted_iota(jnp.int32, (q, q), 0)
    ci = lax.broadcasted_iota(jnp.int32, (q, q), 1)
    lower = ri >= ci
    upper = ci >= ri
    tri_lo = lower.astype(F32)
    tri_up = upper.astype(F32)
    aneg = aneg_ref[...]
    nchunk = l // q
    nctx = CTX_LEN // q

    def step(j, carry):
        sf, sb = carry
        rf = pl.multiple_of(j * q, q)
        jb = jnp.where(j < nctx, nctx - 1 - j, nchunk + nctx - 1 - j)
        rb = pl.multiple_of(jb * q, q)
        sf = _ssd_chunk(xbc_ref, dts_ref, y_ref, rf, sf, aneg, 0, tri_lo, tri_up, lower)
        sb = _ssd_chunk(xbc_ref, dts_ref, y_ref, rb, sb, aneg, 1, tri_up, tri_lo, upper)
        return sf, sb

    zero = jnp.zeros((SSD_STATE, SSD_WIDTH), F32)
    lax.fori_loop(0, nchunk, step, (zero, zero))

    ng = ng_ref[...]

    def finish(i, carry):
        r0 = pl.multiple_of(i * SSD_ROWS, SSD_ROWS)
        v = y_ref[pl.ds(r0, SSD_ROWS), :] * _silu(z_ref[pl.ds(r0, SSD_ROWS), :])
        parts = []
        for g in range(SSD_GROUPS):
            vg = v[:, g * GROUP_W:(g + 1) * GROUP_W]
            parts.append(vg * lax.rsqrt(jnp.mean(vg * vg, axis=-1, keepdims=True) + NORM_EPS))
        o_ref[pl.ds(r0, SSD_ROWS), :] = (jnp.concatenate(parts, axis=1) * ng).astype(BF16)
        return carry

    lax.fori_loop(0, l // SSD_ROWS, finish, 0)


def _ssd(sz, sxbc, sdt, conv_w, conv_b, dt_bias, aneg, dskip, norm_g):
    b, l, _ = sz.shape
    seq = lambda w: pl.BlockSpec((None, l, w), lambda i: (i, 0, 0))
    scratch = [
        pltpu.VMEM((l + 2 * CONV_PAD, SSD_XBC), F32),
        pltpu.VMEM((l, SSD_XBC), F32),
        pltpu.VMEM((l, LANES), F32),
        pltpu.VMEM((l, SSD_WIDTH), F32),
    ]
    return pl.pallas_call(
        _ssd_kernel,
        grid=(b,),
        in_specs=[
            seq(SSD_WIDTH), seq(SSD_XBC), seq(LANES),
            _const_spec((4, SSD_XBC)),
            _const_spec((1, SSD_XBC)),
            _const_spec((1, LANES)),
            _const_spec((1, LANES)),
            _const_spec((1, SSD_WIDTH)),
            _const_spec((1, SSD_WIDTH)),
        ],
        out_specs=seq(SSD_WIDTH),
        out_shape=jax.ShapeDtypeStruct((b, l, SSD_WIDTH), BF16),
        scratch_shapes=scratch,
        compiler_params=_params("arbitrary"),
    )(sz, sxbc, sdt, conv_w, conv_b, dt_bias, aneg, dskip, norm_g)


def _softmax_pair(q, k):
    out = []
    for c in range(2):
        qc = q[:, c * DA_HEAD_DIM:(c + 1) * DA_HEAD_DIM]
        kc = k[:, c * DA_HEAD_DIM:(c + 1) * DA_HEAD_DIM]
        s = lax.dot_general(qc, kc, (((1,), (1,)), ((), ())), preferred_element_type=F32)
        p = jnp.exp(s - jnp.max(s, axis=-1, keepdims=True))
        out.append(p / jnp.sum(p, axis=-1, keepdims=True))
    return out


def _attn_kernel(q_ref, k_ref, v_ref, lam_ref, g_ref, o_ref, *, lam_init):
    lv = lam_ref[...]
    lam = (jnp.exp(jnp.sum(lv[0:1] * lv[1:2], axis=-1, keepdims=True))
           - jnp.exp(jnp.sum(lv[2:3] * lv[3:4], axis=-1, keepdims=True)) + lam_init)

    def run(nkeys):
        p0, p1 = _softmax_pair(q_ref[...], k_ref[0:nkeys, :])
        w = (p0 - lam * p1).astype(BF16)
        o = jnp.dot(w, v_ref[0:nkeys, :], preferred_element_type=F32)
        o_ref[...] = (_rms(o, g_ref[...]) * (1.0 - lam_init)).astype(BF16)

    @pl.when(pl.program_id(2) == 0)
    def _():
        run(CTX_LEN)

    @pl.when(pl.program_id(2) != 0)
    def _():
        run(k_ref.shape[0])


def _attention(q, k, v, lam_vec, subln_g, layer_idx):
    b, l, _ = q.shape
    lam_init = 0.8 - 0.6 * math.exp(-0.3 * layer_idx)
    kv = pl.BlockSpec((None, l, DA_V_DIM), lambda i, h, t: (i, 0, h))
    qo = pl.BlockSpec((None, TOK_TILE, DA_V_DIM), lambda i, h, t: (i, t, h))
    return pl.pallas_call(
        functools.partial(_attn_kernel, lam_init=lam_init),
        grid=(b, DA_HEADS, l // TOK_TILE),
        in_specs=[qo, kv, kv, _const_spec((4, DA_HEAD_DIM)), _const_spec((1, DA_V_DIM))],
        out_specs=qo,
        out_shape=jax.ShapeDtypeStruct((b, l, DA_WIDTH), BF16),
        compiler_params=_params("arbitrary", "arbitrary", "arbitrary"),
    )(q, k, v, lam_vec, subln_g)


def _out_ffn_kernel(h_ref, rg_ref, ssd_ref, da_ref, mod_ref, g2_ref, wo_ref, wg_ref, wu_ref, wd_ref,
                    gf_ref, o_ref, *, final):
    mod = mod_ref[...]
    mix = jnp.dot(rg_ref[...], wo_ref[0:RG_WIDTH, :], preferred_element_type=F32)
    mix += jnp.dot(ssd_ref[...], wo_ref[RG_WIDTH:RG_WIDTH + SSD_WIDTH, :], preferred_element_type=F32)
    mix += jnp.dot(da_ref[...], wo_ref[RG_WIDTH + SSD_WIDTH:, :], preferred_element_type=F32)
    h = h_ref[...] + mod[2:3] * mix
    y = _rms(h, g2_ref[...])
    y = (y * (1.0 + mod[4:5]) + mod[3:4]).astype(BF16)
    gate = jnp.dot(y, wg_ref[...], preferred_element_type=F32)
    up = jnp.dot(y, wu_ref[...], preferred_element_type=F32)
    act = (_silu(gate) * up).astype(BF16)
    h = h + mod[5:6] * jnp.dot(act, wd_ref[...], preferred_element_type=F32)
    o_ref[...] = _rms(h, gf_ref[...]) if final else h


def _out_ffn(h, rg, ssd, da, mod, norm2_g, w_out, w_gate, w_up, w_down, final_g, final):
    b, l, _ = h.shape
    t0 = CTX_LEN // TOK_TILE if final else 0
    nt = l // TOK_TILE - t0
    tok = lambda w: pl.BlockSpec((None, TOK_TILE, w), lambda t, i: (i, t + t0, 0))
    out_rows = nt * TOK_TILE
    kwargs = {} if final else {"input_output_aliases": {0: 0}}
    return pl.pallas_call(
        functools.partial(_out_ffn_kernel, final=final),
        grid=(nt, b),
        in_specs=[
            tok(D_MODEL), tok(RG_WIDTH), tok(SSD_WIDTH), tok(DA_WIDTH),
            pl.BlockSpec((None, 6, D_MODEL), lambda t, i: (jnp.where(t + t0 == 0, b, i), 0, 0)),
            _const_spec((1, D_MODEL)),
            _const_spec((D_MODEL, D_MODEL)),
            _const_spec((D_MODEL, D_FF)),
            _const_spec((D_MODEL, D_FF)),
            _const_spec((D_FF, D_MODEL)),
            _const_spec((1, D_MODEL)),
        ],
        out_specs=pl.BlockSpec((None, TOK_TILE, D_MODEL), lambda t, i: (i, t, 0)),
        out_shape=jax.ShapeDtypeStruct((b, out_rows, D_MODEL), F32),
        compiler_params=_params("arbitrary", "arbitrary"),
        **kwargs,
    )(h, rg, ssd, da, mod, norm2_g, w_out, w_gate, w_up, w_down, final_g)


def _rope_tables(n_lat):
    half = DA_HEAD_DIM // 2
    inv_freq = jnp.power(ROPE_BASE, -jnp.arange(0, half, 2, dtype=F32) / half)
    t = jnp.arange(n_lat, dtype=jnp.int32)
    ang_r = (t // GRID_W).astype(F32)[:, None] * inv_freq
    ang_c = (t % GRID_W).astype(F32)[:, None] * inv_freq
    cos = jnp.concatenate([jnp.cos(ang_r), jnp.cos(ang_r), jnp.cos(ang_c), jnp.cos(ang_c)], axis=1)
    sin = jnp.concatenate([-jnp.sin(ang_r), jnp.sin(ang_r), -jnp.sin(ang_c), jnp.sin(ang_c)], axis=1)
    reps = DA_WIDTH // DA_HEAD_DIM
    cos = jnp.concatenate([jnp.ones((CTX_LEN, DA_HEAD_DIM), F32), cos], axis=0)
    sin = jnp.concatenate([jnp.zeros((CTX_LEN, DA_HEAD_DIM), F32), sin], axis=0)
    cos = jnp.tile(cos, (1, reps))
    sin = jnp.tile(sin, (1, reps))
    scale = DA_HEAD_DIM ** -0.5
    return cos * scale, sin * scale, cos, sin


def _permute_w_in(w):
    offs = [0]
    for s in IN_SIZES:
        offs.append(offs[-1] + s)
    parts = [w[:, offs[i]:offs[i + 1]] for i in range(len(IN_SIZES))]
    pad = jnp.zeros((w.shape[0], LANES - IN_SIZES[4]), w.dtype)
    return jnp.concatenate(parts[0:4] + parts[5:8] + [parts[4], pad], axis=1).astype(BF16)


def _block_diag(w):
    eye = jnp.eye(RG_HEADS, dtype=w.dtype)
    return jnp.einsum("hij,hg->higj", w, eye).reshape(RG_WIDTH, RG_WIDTH)


def _lane_pad(v):
    return jnp.concatenate([v, jnp.zeros((LANES - v.shape[0],), v.dtype)])[None, :]


def kernel(x, c, ctx, c_ctx, w_mod, b_mod, norm1_g, w_in, rg_conv_w, rg_conv_b, rg_w_a, rg_b_a, rg_w_x, rg_b_x, rg_lambda, ssd_conv_w, ssd_conv_b, ssd_dt_bias, ssd_a_log, ssd_d, ssd_norm_g, da_lambda, da_subln_g, w_out, norm2_g, w_gate, w_up, w_down, final_norm_g):
    b, n_lat, _ = x.shape
    assert ctx.shape[1] == CTX_LEN and n_lat % TOK_TILE == 0
    rows = -(-(b + 1) // SUBLANES) * SUBLANES
    c_all = jnp.concatenate([c, c_ctx[None, :], jnp.zeros((rows - b - 1, D_MODEL), F32)], axis=0)
    mods = _modulation(c_all, w_mod, b_mod)
    tables = _rope_tables(n_lat)
    h = jnp.concatenate([ctx, x], axis=1)
    out = None
    for l in range(DEPTH):
        final = l == DEPTH - 1
        rg, sz, sxbc, sdt, q, k, v = _in_proj(h, mods[l], norm1_g[l][None, :], _permute_w_in(w_in[l]), tables)
        w_gates = jnp.concatenate(
            [_block_diag(w[l, d]) for d in range(2) for w in (rg_w_a, rg_w_x)], axis=1).astype(BF16)
        b_gates = jnp.concatenate(
            [bb[l, d] for d in range(2) for bb in (rg_b_a, rg_b_x)])[None, :]
        y_rg = _rglru(rg, rg_conv_w[l], rg_conv_b[l][None, :], w_gates, b_gates, rg_lambda[l])
        y_ssd = _ssd(sz, sxbc, sdt, ssd_conv_w[l], ssd_conv_b[l][None, :],
                     _lane_pad(ssd_dt_bias[l].reshape(-1)),
                     _lane_pad(-jnp.exp(ssd_a_log[l].reshape(-1))),
                     jnp.repeat(ssd_d[l], SSD_HEADDIM)[None, :], ssd_norm_g[l][None, :])
        y_da = _attention(q, k, v, da_lambda[l], da_subln_g[l][None, :], l)
        res = _out_ffn(h, y_rg, y_ssd, y_da, mods[l], norm2_g[l][None, :], w_out[l].astype(BF16),
                       w_gate[l].astype(BF16), w_up[l].astype(BF16), w_down[l].astype(BF16),
                       final_norm_g[None, :], final)
        if final:
            out = res
        else:
            h = res
    return out
```

```python
import functools
import math

import jax
import jax.numpy as jnp
from jax import lax
from jax.experimental import pallas as pl
from jax.experimental.pallas import tpu as pltpu

F32 = jnp.float32
BF16 = jnp.bfloat16

D_MODEL = 1024
DEPTH = 2
CTX_LEN = 256
GRID_W = 64
NORM_EPS = 1e-6

RG_WIDTH = 256
RG_HEADS = 4
RG_BLOCK = 64
RG_C = 8.0

SSD_WIDTH = 256
SSD_HEADDIM = 64
SSD_HEADS = 4
SSD_GROUPS = 2
SSD_STATE = 64
SSD_CHUNK = 128
SSD_XBC = 512

DA_WIDTH = 512
DA_HEADS = 4
DA_HEAD_DIM = 64
DA_V_DIM = 128
ROPE_BASE = 10000.0

D_FF = 2816
IN_SIZES = (256, 256, 256, 512, 8, 512, 512, 512)
D_IN = sum(IN_SIZES)

LANES = 128
SUBLANES = 8
TOK_TILE = 256
CONV_PAD = SUBLANES
VMEM_LIMIT = 56 * 1024 * 1024

U_RG = 0
U_SZ = 512
U_XBC = 768
U_Q = 1280
U_K = 1792
U_V = 2304
U_DT = 2816
U_COLS = 2944


def _silu(x):
    return x * jax.nn.sigmoid(x)


def _softplus(x):
    return jnp.maximum(x, 0.0) + jnp.log1p(jnp.exp(-jnp.abs(x)))


def _rms(x, g):
    ms = jnp.mean(x * x, axis=-1, keepdims=True)
    return x * lax.rsqrt(ms + NORM_EPS) * g


def _const_spec(shape):
    nd = len(shape)
    return pl.BlockSpec(shape, lambda *_: (0,) * nd, pipeline_mode=pl.Buffered(1))


def _params(*sem):
    return pltpu.CompilerParams(dimension_semantics=sem, vmem_limit_bytes=VMEM_LIMIT)


def _mod_kernel(c_ref, w_ref, b_ref, o_ref):
    c = c_ref[...]
    a = _silu(c).astype(BF16)
    o_ref[...] = jnp.dot(a, w_ref[...].astype(BF16), preferred_element_type=F32) + b_ref[...]


def _modulation(c_all, w_mod, b_mod):
    r = c_all.shape[0]
    out = pl.pallas_call(
        _mod_kernel,
        grid=(DEPTH, 6),
        in_specs=[
            pl.BlockSpec((r, D_MODEL), lambda l, j: (0, 0)),
            pl.BlockSpec((None, D_MODEL, D_MODEL), lambda l, j: (l, 0, j)),
            pl.BlockSpec((None, 1, D_MODEL), lambda l, j: (l, 0, j)),
        ],
        out_specs=pl.BlockSpec((None, r, D_MODEL), lambda l, j: (l, 0, j)),
        out_shape=jax.ShapeDtypeStruct((DEPTH, r, 6 * D_MODEL), F32),
        compiler_params=_params("arbitrary", "arbitrary"),
    )(c_all, w_mod, b_mod.reshape(DEPTH, 1, 6 * D_MODEL))
    return out.reshape(DEPTH, r, 6, D_MODEL)


def _rope(x, cos, sin, first_half):
    width = x.shape[-1]
    partner = jnp.where(first_half, pltpu.roll(x, width - 16, 1), pltpu.roll(x, 16, 1))
    return x * cos + partner * sin


def _load_stream(h_refs):
    if len(h_refs) == 1:
        return h_refs[0][...]
    return jnp.where(pl.program_id(0) == 0, h_refs[0][...], h_refs[1][...])


def _stream_specs(hs, b, t0=0):
    if len(hs) == 1:
        return [pl.BlockSpec((None, TOK_TILE, D_MODEL), lambda t, i: (i, t + t0, 0))]
    assert t0 == 0
    return [pl.BlockSpec((None, TOK_TILE, D_MODEL), lambda t, i: (jnp.where(t == 0, i, 0), 0, 0)),
            pl.BlockSpec((None, TOK_TILE, D_MODEL), lambda t, i: (i, jnp.maximum(t - 1, 0), 0))]


def _inproj_kernel(*refs):
    (mod_ref, g_ref, w_ref, cq_ref, sq_ref, ck_ref, sk_ref,
     rg_ref, sz_ref, xbc_ref, dt_ref, q_ref, k_ref, v_ref) = refs[-14:]
    mod = mod_ref[...]
    y = _rms(_load_stream(refs[:-14]), g_ref[...])
    y = y * (1.0 + mod[1:2]) + mod[0:1]
    u = jnp.dot(y.astype(BF16), w_ref[...], preferred_element_type=F32)
    rg_ref[...] = u[:, U_RG:U_SZ]
    sz_ref[...] = u[:, U_SZ:U_XBC]
    xbc_ref[...] = u[:, U_XBC:U_Q]
    dt_ref[...] = u[:, U_DT:U_COLS]
    q = u[:, U_Q:U_K]
    k = u[:, U_K:U_V]
    lane = lax.broadcasted_iota(jnp.int32, q.shape, 1)
    first_half = (lane % 32) < 16
    q_ref[...] = _rope(q, cq_ref[...], sq_ref[...], first_half).astype(BF16)
    k_ref[...] = _rope(k, ck_ref[...], sk_ref[...], first_half).astype(BF16)
    v_ref[...] = u[:, U_V:U_DT].astype(BF16)


def _in_proj(hs, mod, norm_g, w_in_p, tables):
    b = hs[0].shape[0]
    l = sum(h.shape[1] for h in hs)
    nt = l // TOK_TILE
    tok = lambda w: pl.BlockSpec((None, TOK_TILE, w), lambda t, i: (i, t, 0))
    tab = pl.BlockSpec((TOK_TILE, DA_WIDTH), lambda t, i: (t, 0))
    widths = (512, 256, 512, 128, 512, 512, 512)
    dtypes = (F32, F32, F32, F32, BF16, BF16, BF16)
    return pl.pallas_call(
        _inproj_kernel,
        grid=(nt, b),
        in_specs=_stream_specs(hs, b) + [
            pl.BlockSpec((None, 6, D_MODEL), lambda t, i: (jnp.where(t == 0, b, i), 0, 0)),
            _const_spec((1, D_MODEL)),
            _const_spec((D_MODEL, U_COLS)),
            tab, tab, tab, tab,
        ],
        out_specs=[tok(w) for w in widths],
        out_shape=[jax.ShapeDtypeStruct((b, l, w), d) for w, d in zip(widths, dtypes)],
        compiler_params=_params("arbitrary", "arbitrary"),
    )(*hs, mod, norm_g, w_in_p, *tables)


def _conv_rows(x_ref, col0, width, r0, rows, w, bias):
    l = x_ref.shape[0]
    assert CTX_LEN % rows == 0
    span = rows + 2 * CONV_PAD
    lo = pl.multiple_of(jnp.maximum(r0 - CONV_PAD, 0), CONV_PAD)
    hi = pl.multiple_of(jnp.minimum(r0 + rows, l - CONV_PAD), CONV_PAD)
    cols = slice(col0, col0 + width)
    seg_start = jnp.logical_or(r0 == 0, r0 == CTX_LEN)
    seg_end = jnp.logical_or(r0 + rows == CTX_LEN, r0 + rows == l)
    xa = jnp.concatenate([jnp.where(seg_start, 0.0, x_ref[pl.ds(lo, CONV_PAD), cols]),
                          x_ref[pl.ds(r0, rows), cols],
                          jnp.where(seg_end, 0.0, x_ref[pl.ds(hi, CONV_PAD), cols])], axis=0)
    acc = None
    for tap in range(4):
        off = tap - 2
        sh = xa if off == 0 else pltpu.roll(xa, (-off) % span, 0)
        term = sh[CONV_PAD:CONV_PAD + rows] * w[tap:tap + 1]
        acc = term if acc is None else acc + term
    return acc + bias


RG_ROWS = 256


def _scan8(a, bx, reverse):
    row = lax.broadcasted_iota(jnp.int32, a.shape, 0)
    for s in (1, 2, 4):
        shift = (SUBLANES - s) if reverse else s
        a_sh = pltpu.roll(a, shift, 0)
        b_sh = pltpu.roll(bx, shift, 0)
        ok = (row < SUBLANES - s) if reverse else (row >= s)
        bx = jnp.where(ok, a * b_sh + bx, bx)
        a = jnp.where(ok, a * a_sh, a)
    return a, bx


def _rglru_kernel(rg_ref, cw_ref, cb_ref, wg_ref, bg_ref, lam_ref, o_ref,
                  af_ref, bf_ref, ab_ref, bb_ref):
    l = rg_ref.shape[0]
    cw = cw_ref[...]
    cb = cb_ref[...]
    bg = bg_ref[...]
    sp = _softplus(-lam_ref[...])

    def coeffs(i, carry):
        r0 = pl.multiple_of(i * RG_ROWS, RG_ROWS)
        xc = _conv_rows(rg_ref, 0, RG_WIDTH, r0, RG_ROWS, cw, cb)
        gates = jnp.dot(xc.astype(BF16), wg_ref[...], preferred_element_type=F32) + bg
        for d, (a_ref, b_ref) in enumerate(((af_ref, bf_ref), (ab_ref, bb_ref))):
            rr = jax.nn.sigmoid(gates[:, (2 * d) * RG_WIDTH:(2 * d + 1) * RG_WIDTH])
            ii = jax.nn.sigmoid(gates[:, (2 * d + 1) * RG_WIDTH:(2 * d + 2) * RG_WIDTH])
            log_a = -RG_C * rr * sp[d:d + 1]
            a = jnp.exp(log_a)
            one_m_a2 = -jnp.tanh(log_a) * (a * a + 1.0)
            a_ref[pl.ds(r0, RG_ROWS), :] = a
            b_ref[pl.ds(r0, RG_ROWS), :] = jnp.sqrt(one_m_a2) * (ii * xc)
        return carry

    lax.fori_loop(0, l // RG_ROWS, coeffs, 0)

    nblk = l // SUBLANES
    nctx = CTX_LEN // SUBLANES

    def scan(j, carry):
        hf, hb = carry
        rf = pl.multiple_of(j * SUBLANES, SUBLANES)
        jb = jnp.where(j < nctx, nctx - 1 - j, nblk + nctx - 1 - j)
        rb = pl.multiple_of(jb * SUBLANES, SUBLANES)
        a, bx = _scan8(af_ref[pl.ds(rf, SUBLANES), :], bf_ref[pl.ds(rf, SUBLANES), :], False)
        h = a * hf + bx
        af_ref[pl.ds(rf, SUBLANES), :] = h
        hf = h[SUBLANES - 1:SUBLANES]
        a, bx = _scan8(ab_ref[pl.ds(rb, SUBLANES), :], bb_ref[pl.ds(rb, SUBLANES), :], True)
        h = a * hb + bx
        ab_ref[pl.ds(rb, SUBLANES), :] = h
        hb = h[0:1]
        return hf, hb

    zero = jnp.zeros((1, RG_WIDTH), F32)
    lax.fori_loop(0, nblk, scan, (zero, zero), unroll=4)

    def finish(i, carry):
        r0 = pl.multiple_of(i * RG_ROWS, RG_ROWS)
        hsum = af_ref[pl.ds(r0, RG_ROWS), :] + ab_ref[pl.ds(r0, RG_ROWS), :]
        g = rg_ref[pl.ds(r0, RG_ROWS), RG_WIDTH:2 * RG_WIDTH]
        o_ref[pl.ds(r0, RG_ROWS), :] = (hsum * jax.nn.gelu(g, approximate=True)).astype(BF16)
        return carry

    lax.fori_loop(0, l // RG_ROWS, finish, 0)


def _rglru(rg, conv_w, conv_b, w_gates, b_gates, lam):
    b, l, _ = rg.shape
    seq = lambda w: pl.BlockSpec((None, l, w), lambda i: (i, 0, 0))
    scratch = [pltpu.VMEM((l, RG_WIDTH), F32)] * 4
    return pl.pallas_call(
        _rglru_kernel,
        grid=(b,),
        in_specs=[
            seq(2 * RG_WIDTH),
            _const_spec((4, RG_WIDTH)),
            _const_spec((1, RG_WIDTH)),
            _const_spec((RG_WIDTH, 4 * RG_WIDTH)),
            _const_spec((1, 4 * RG_WIDTH)),
            _const_spec((2, RG_WIDTH)),
        ],
        out_specs=seq(RG_WIDTH),
        out_shape=jax.ShapeDtypeStruct((b, l, RG_WIDTH), BF16),
        scratch_shapes=scratch,
        compiler_params=_params("arbitrary"),
    )(rg, conv_w, conv_b, w_gates, b_gates, lam)


SSD_ROWS = 256
GROUP_W = SSD_WIDTH // SSD_GROUPS
HEADS_PER_GROUP = SSD_HEADS // SSD_GROUPS
STATE_ROWS = SSD_GROUPS * SSD_STATE
NT_DIMS = (((1,), (1,)), ((), ()))
SSD_GROUP_CHUNKS = 3


def _ssd_head_lane(direction, group, j):
    return direction * SSD_HEADS + group * HEADS_PER_GROUP + j


def _ssd_kernel(z_ref, xin_ref, dt_ref, cw_ref, cb_ref, dtb_ref, aneg_ref, dsk_ref, ng_ref, o_ref,
                xbc_ref, dts_ref, a3_ref, y_ref, ecc_ref, ds_ref, sent_ref, etot_ref):
    l = z_ref.shape[0]
    q = SSD_CHUNK
    assert HEADS_PER_GROUP == 2 and GROUP_W == LANES and STATE_ROWS == LANES
    cw = cw_ref[...]
    cb = cb_ref[...]
    dtb = dtb_ref[...]
    dsk = dsk_ref[...]
    aneg = aneg_ref[...]

    def prep(i, carry):
        r0 = pl.multiple_of(i * SSD_ROWS, SSD_ROWS)
        rows = pl.ds(r0, SSD_ROWS)
        xbc = _silu(_conv_rows(xin_ref, 0, SSD_XBC, r0, SSD_ROWS, cw, cb))
        xbc_ref[rows, :] = xbc
        dt = _softplus(dt_ref[rows, :] + dtb)
        dts_ref[rows, :] = dt
        y_ref[rows, :] = xbc[:, 0:SSD_WIDTH] * dsk
        a = dt * aneg
        hi = a.astype(BF16).astype(F32)
        r1 = a - hi
        mid = r1.astype(BF16).astype(F32)
        lo = r1 - mid
        a3_ref[rows, :] = (hi + pltpu.roll(mid, 2 * SSD_HEADS, 1) + pltpu.roll(lo, 4 * SSD_HEADS, 1)).astype(BF16)
        return carry

    lax.fori_loop(0, l // SSD_ROWS, prep, 0)

    ri = lax.broadcasted_iota(jnp.int32, (q, q), 0)
    ci = lax.broadcasted_iota(jnp.int32, (q, q), 1)
    lower = ri >= ci
    upper = ci >= ri
    tri_lo = lower.astype(F32).astype(BF16)
    tri_up = upper.astype(F32).astype(BF16)
    eye16 = (ri == ci).astype(F32).astype(BF16)
    eye8 = (lax.broadcasted_iota(jnp.int32, (SUBLANES, LANES), 0)
            == lax.broadcasted_iota(jnp.int32, (SUBLANES, LANES), 1)).astype(F32)
    lane = lax.broadcasted_iota(jnp.int32, (q, LANES), 1)
    left = lane < SSD_HEADDIM
    lane_row = lax.broadcasted_iota(jnp.int32, (1, LANES), 1)
    left_row = lane_row < SSD_HEADDIM
    sub8 = lax.broadcasted_iota(jnp.int32, (SUBLANES, q), 0)
    left_state = lax.broadcasted_iota(jnp.int32, (SSD_STATE, GROUP_W), 1) < SSD_HEADDIM
    nchunk = l // q
    nctx = CTX_LEN // q

    def local(cg, carry):
        chunks = [cg * SSD_GROUP_CHUNKS + k for k in range(SSD_GROUP_CHUNKS)]
        rows = [pl.ds(pl.multiple_of(c * q, q), q) for c in chunks]
        hi = lax.Precision.HIGHEST

        def fold(cs):
            return cs + pltpu.roll(cs, LANES - 2 * SSD_HEADS, 1) + pltpu.roll(cs, LANES - 4 * SSD_HEADS, 1)

        x16, bm_t, grams, cs_lo, cs_up = [], [], [], [], []
        for r in rows:
            x16.append(xbc_ref[r, 0:SSD_WIDTH].astype(BF16))
            bm16 = xbc_ref[r, SSD_WIDTH:SSD_WIDTH + LANES].astype(BF16)
            cm = xbc_ref[r, SSD_WIDTH + LANES:SSD_XBC]
            bm_t.append(lax.dot_general(eye16, bm16, NT_DIMS, preferred_element_type=F32))
            gr = []
            for g in range(SSD_GROUPS):
                cm_g = jnp.where((lane < SSD_STATE) == (g == 0), cm, 0.0).astype(BF16)
                gr.append(lax.dot_general(cm_g, bm16, NT_DIMS, preferred_element_type=F32))
            grams.append(gr)
            a3 = a3_ref[r, :]
            cs_lo.append(jnp.dot(tri_lo, a3, preferred_element_type=F32))
            cs_up.append(jnp.dot(tri_up, a3, preferred_element_type=F32))

        c_col, c_row, dt_row = [], [], []
        for k, r in enumerate(rows):
            cc = jnp.where(lane < SSD_HEADS, fold(cs_lo[k]), fold(cs_up[k]))
            c_col.append(cc)
            c_row.append(lax.dot_general(eye8, cc, NT_DIMS, precision=hi, preferred_element_type=F32))
            dt_row.append(lax.dot_general(eye8, dts_ref[r, :], NT_DIMS, precision=hi,
                                          preferred_element_type=F32))

        zblock = jnp.zeros((SSD_STATE, GROUP_W), F32)
        for k, (c, r) in enumerate(zip(chunks, rows)):
            tot_col = jnp.where(sub8 < SSD_HEADS, c_row[k][:, q - 1:q], c_row[k][:, 0:1])
            dtw_row = dt_row[k] * jnp.exp(tot_col - c_row[k])
            tot_row = jnp.where(lane_row < SSD_HEADS, c_col[k][q - 1:q, :], c_col[k][0:1, :])
            e_tot = jnp.exp(tot_row)
            for d in range(2):
                mask = lower if d == 0 else upper
                ds = []
                etot = []
                for g in range(SSD_GROUPS):
                    xg16 = x16[k][:, g * GROUP_W:(g + 1) * GROUP_W]
                    la = _ssd_head_lane(d, g, 0)
                    lb = _ssd_head_lane(d, g, 1)
                    yd = []
                    dsh = []
                    ecc = []
                    for li in (la, lb):
                        ccb = jnp.broadcast_to(c_col[k][:, li:li + 1], (q, q))
                        decay = jnp.exp(jnp.where(mask, ccb - c_row[k][li:li + 1, :], -1e30))
                        m = (grams[k][g] * decay * dt_row[k][li:li + 1, :]).astype(BF16)
                        yd.append(jnp.dot(m, xg16, preferred_element_type=F32))
                        ecc.append(jnp.exp(ccb))
                        b_t = (bm_t[k][g * SSD_STATE:(g + 1) * SSD_STATE, :] * dtw_row[li:li + 1, :]).astype(BF16)
                        dsh.append(jnp.dot(b_t, xg16, preferred_element_type=F32))
                    cols = slice(g * GROUP_W, (g + 1) * GROUP_W)
                    y_ref[r, cols] += jnp.where(left, yd[0], yd[1])
                    ecc_ref[d, r, cols] = jnp.where(left, ecc[0], ecc[1])
                    dsg = jnp.where(left_state, dsh[0], dsh[1])
                    ds.append(jnp.concatenate([dsg, zblock] if g == 0 else [zblock, dsg], axis=1))
                    etot.append(jnp.where(left_row, e_tot[:, la:la + 1], e_tot[:, lb:lb + 1]))
                ds_ref[d, pl.ds(pl.multiple_of(c * STATE_ROWS, STATE_ROWS), STATE_ROWS), :] = (
                    jnp.concatenate(ds, axis=0))
                etot_ref[d, pl.ds(pl.multiple_of(c * SUBLANES, SUBLANES), SUBLANES), :] = (
                    jnp.broadcast_to(jnp.concatenate(etot, axis=1), (SUBLANES, SSD_WIDTH)))
        return carry

    assert nchunk % SSD_GROUP_CHUNKS == 0
    lax.fori_loop(0, nchunk // SSD_GROUP_CHUNKS, local, 0)

    def carry_state(j, state):
        sf, sb = state
        jb = jnp.where(j < nctx, nctx - 1 - j, nchunk + nctx - 1 - j)
        out = []
        for d, (c, s) in enumerate(((j, sf), (jb, sb))):
            srows = pl.ds(pl.multiple_of(c * STATE_ROWS, STATE_ROWS), STATE_ROWS)
            sent_ref[d, srows, :] = s.astype(BF16)
            e = etot_ref[d, pl.ds(pl.multiple_of(c * SUBLANES, SUBLANES), 1), :]
            out.append(e * s + ds_ref[d, srows, :])
        return tuple(out)

    zero = jnp.zeros((STATE_ROWS, SSD_WIDTH), F32)
    lax.fori_loop(0, nchunk, carry_state, (zero, zero))

    ng = ng_ref[...]

    def finish(c, carry):
        r0 = pl.multiple_of(c * q, q)
        rows = pl.ds(r0, q)
        srows = pl.ds(pl.multiple_of(c * STATE_ROWS, STATE_ROWS), STATE_ROWS)
        cm16 = xbc_ref[rows, SSD_WIDTH + LANES:SSD_XBC].astype(BF16)
        y = y_ref[rows, :]
        for d in range(2):
            y = y + ecc_ref[d, rows, :] * jnp.dot(cm16, sent_ref[d, srows, :], preferred_element_type=F32)
        v = y * _silu(z_ref[rows, :])
        parts = []
        for g in range(SSD_GROUPS):
            vg = v[:, g * GROUP_W:(g + 1) * GROUP_W]
            parts.append(vg * lax.rsqrt(jnp.mean(vg * vg, axis=-1, keepdims=True) + NORM_EPS))
        o_ref[rows, :] = (jnp.concatenate(parts, axis=1) * ng).astype(BF16)
        return carry

    lax.fori_loop(0, nchunk, finish, 0, unroll=3)


def _ssd(sz, sxbc, sdt, conv_w, conv_b, dt_bias, aneg, dskip, norm_g):
    b, l, _ = sz.shape
    nchunk = l // SSD_CHUNK
    seq = lambda w: pl.BlockSpec((None, l, w), lambda i: (i, 0, 0))
    scratch = [
        pltpu.VMEM((l, SSD_XBC), F32),
        pltpu.VMEM((l, LANES), F32),
        pltpu.VMEM((l, LANES), BF16),
        pltpu.VMEM((l, SSD_WIDTH), F32),
        pltpu.VMEM((2, l, SSD_WIDTH), F32),
        pltpu.VMEM((2, nchunk * STATE_ROWS, SSD_WIDTH), F32),
        pltpu.VMEM((2, nchunk * STATE_ROWS, SSD_WIDTH), BF16),
        pltpu.VMEM((2, nchunk * SUBLANES, SSD_WIDTH), F32),
    ]
    return pl.pallas_call(
        _ssd_kernel,
        grid=(b,),
        in_specs=[
            seq(SSD_WIDTH), seq(SSD_XBC), seq(LANES),
            _const_spec((4, SSD_XBC)),
            _const_spec((1, SSD_XBC)),
            _const_spec((1, LANES)),
            _const_spec((1, LANES)),
            _const_spec((1, SSD_WIDTH)),
            _const_spec((1, SSD_WIDTH)),
        ],
        out_specs=seq(SSD_WIDTH),
        out_shape=jax.ShapeDtypeStruct((b, l, SSD_WIDTH), BF16),
        scratch_shapes=scratch,
        compiler_params=_params("arbitrary"),
    )(sz, sxbc, sdt, conv_w, conv_b, dt_bias, aneg, dskip, norm_g)


def _attn_probs(q, k_ref, p_ref, nkeys):
    l = k_ref.shape[0]
    for c in range(2):
        qc = q[:, c * DA_HEAD_DIM:(c + 1) * DA_HEAD_DIM]
        kc = k_ref[0:nkeys, c * DA_HEAD_DIM:(c + 1) * DA_HEAD_DIM]
        s = lax.dot_general(qc, kc, (((1,), (1,)), ((), ())), preferred_element_type=F32)
        p_ref[:, c * l:c * l + nkeys] = jnp.exp2(s - jnp.max(s, axis=-1, keepdims=True)).astype(BF16)


def _attn_values(p_ref, vaug_ref, nkeys, lam, g, lam_init):
    l = vaug_ref.shape[0]
    outs = []
    for c in range(2):
        ov = jnp.dot(p_ref[:, c * l:c * l + nkeys], vaug_ref[0:nkeys, :], preferred_element_type=F32)
        outs.append(ov[:, 0:DA_V_DIM] / ov[:, DA_V_DIM:DA_V_DIM + 1])
    o = outs[0] - lam * outs[1]
    return (_rms(o, g) * (1.0 - lam_init)).astype(BF16)


def _attn_kernel(q_ref, k_ref, v_ref, lam_ref, g_ref, o_ref, vaug_ref, pa_ref, pb_ref, *, lam_init):
    l = q_ref.shape[0]
    nt = l // TOK_TILE
    assert CTX_LEN == TOK_TILE and nt % 2 == 1 and nt >= 3
    lv = lam_ref[...]
    lam = (jnp.exp(jnp.sum(lv[0:1] * lv[1:2], axis=-1, keepdims=True))
           - jnp.exp(jnp.sum(lv[2:3] * lv[3:4], axis=-1, keepdims=True)) + lam_init)
    g = g_ref[...]
    vaug_ref[:, 0:DA_V_DIM] = v_ref[...]
    vaug_ref[:, DA_V_DIM:2 * DA_V_DIM] = jnp.ones((l, DA_V_DIM), BF16)

    def rows(t):
        return pl.ds(pl.multiple_of(t * TOK_TILE, TOK_TILE), TOK_TILE)

    def probs(t, p_ref, nkeys=l):
        _attn_probs(q_ref[rows(t), :], k_ref, p_ref, nkeys)

    def values(t, p_ref, nkeys=l):
        o_ref[rows(t), :] = _attn_values(p_ref, vaug_ref, nkeys, lam, g, lam_init)

    probs(0, pa_ref, CTX_LEN)
    probs(1, pb_ref)
    values(0, pa_ref, CTX_LEN)

    def pair(i, carry):
        t = 2 * i
        probs(t, pa_ref)
        values(t - 1, pb_ref)
        probs(t + 1, pb_ref)
        values(t, pa_ref)
        return carry

    lax.fori_loop(1, nt // 2, pair, 0)
    probs(nt - 1, pa_ref)
    values(nt - 2, pb_ref)
    values(nt - 1, pa_ref)


def _attention(q, k, v, lam_vec, subln_g, layer_idx):
    b, l, _ = q.shape
    lam_init = 0.8 - 0.6 * math.exp(-0.3 * layer_idx)
    seq = pl.BlockSpec((None, l, DA_V_DIM), lambda i, h: (i, 0, h))
    return pl.pallas_call(
        functools.partial(_attn_kernel, lam_init=lam_init),
        grid=(b, DA_HEADS),
        in_specs=[seq, seq, seq, _const_spec((4, DA_HEAD_DIM)), _const_spec((1, DA_V_DIM))],
        out_specs=seq,
        out_shape=jax.ShapeDtypeStruct((b, l, DA_WIDTH), BF16),
        scratch_shapes=[pltpu.VMEM((l, 2 * DA_V_DIM), BF16)] + [pltpu.VMEM((TOK_TILE, 2 * l), BF16)] * 2,
        compiler_params=_params("arbitrary", "arbitrary"),
    )(q, k, v, lam_vec, subln_g)


def _out_ffn_kernel(*refs, final):
    rg_ref, ssd_ref, da_ref, mod_ref, g2_ref, wo_ref, wg_ref, wu_ref, wd_ref, gf_ref, o_ref = refs[-11:]
    mod = mod_ref[...]
    mix = jnp.dot(rg_ref[...], wo_ref[0:RG_WIDTH, :], preferred_element_type=F32)
    mix += jnp.dot(ssd_ref[...], wo_ref[RG_WIDTH:RG_WIDTH + SSD_WIDTH, :], preferred_element_type=F32)
    mix += jnp.dot(da_ref[...], wo_ref[RG_WIDTH + SSD_WIDTH:, :], preferred_element_type=F32)
    h = _load_stream(refs[:-11]) + mod[2:3] * mix
    y = _rms(h, g2_ref[...])
    y = (y * (1.0 + mod[4:5]) + mod[3:4]).astype(BF16)
    gate = jnp.dot(y, wg_ref[...], preferred_element_type=F32)
    up = jnp.dot(y, wu_ref[...], preferred_element_type=F32)
    act = (_silu(gate) * up).astype(BF16)
    h = h + mod[5:6] * jnp.dot(act, wd_ref[...], preferred_element_type=F32)
    o_ref[...] = _rms(h, gf_ref[...]) if final else h


def _out_ffn(hs, rg, ssd, da, mod, norm2_g, w_out, w_gate, w_up, w_down, final_g, final):
    b, l, _ = rg.shape
    t0 = CTX_LEN // TOK_TILE if final else 0
    nt = l // TOK_TILE - t0
    tok = lambda w: pl.BlockSpec((None, TOK_TILE, w), lambda t, i: (i, t + t0, 0))
    out_rows = nt * TOK_TILE
    kwargs = {"input_output_aliases": {0: 0}} if (len(hs) == 1 and not final) else {}
    return pl.pallas_call(
        functools.partial(_out_ffn_kernel, final=final),
        grid=(nt, b),
        in_specs=_stream_specs(hs, b, t0) + [
            tok(RG_WIDTH), tok(SSD_WIDTH), tok(DA_WIDTH),
            pl.BlockSpec((None, 6, D_MODEL), lambda t, i: (jnp.where(t + t0 == 0, b, i), 0, 0)),
            _const_spec((1, D_MODEL)),
            _const_spec((D_MODEL, D_MODEL)),
            _const_spec((D_MODEL, D_FF)),
            _const_spec((D_MODEL, D_FF)),
            _const_spec((D_FF, D_MODEL)),
            _const_spec((1, D_MODEL)),
        ],
        out_specs=pl.BlockSpec((None, TOK_TILE, D_MODEL), lambda t, i: (i, t, 0)),
        out_shape=jax.ShapeDtypeStruct((b, out_rows, D_MODEL), F32),
        compiler_params=_params("arbitrary", "arbitrary"),
        **kwargs,
    )(*hs, rg, ssd, da, mod, norm2_g, w_out, w_gate, w_up, w_down, final_g)


def _rope_tables(n_lat):
    half = DA_HEAD_DIM // 2
    inv_freq = jnp.power(ROPE_BASE, -jnp.arange(0, half, 2, dtype=F32) / half)
    t = jnp.arange(n_lat, dtype=jnp.int32)
    ang_r = (t // GRID_W).astype(F32)[:, None] * inv_freq
    ang_c = (t % GRID_W).astype(F32)[:, None] * inv_freq
    cos = jnp.concatenate([jnp.cos(ang_r), jnp.cos(ang_r), jnp.cos(ang_c), jnp.cos(ang_c)], axis=1)
    sin = jnp.concatenate([-jnp.sin(ang_r), jnp.sin(ang_r), -jnp.sin(ang_c), jnp.sin(ang_c)], axis=1)
    reps = DA_WIDTH // DA_HEAD_DIM
    cos = jnp.concatenate([jnp.ones((CTX_LEN, DA_HEAD_DIM), F32), cos], axis=0)
    sin = jnp.concatenate([jnp.zeros((CTX_LEN, DA_HEAD_DIM), F32), sin], axis=0)
    cos = jnp.tile(cos, (1, reps))
    sin = jnp.tile(sin, (1, reps))
    scale = DA_HEAD_DIM ** -0.5 * math.log2(math.e)
    return cos * scale, sin * scale, cos, sin


def _permute_w_in(w):
    offs = [0]
    for s in IN_SIZES:
        offs.append(offs[-1] + s)
    parts = [w[:, offs[i]:offs[i + 1]] for i in range(len(IN_SIZES))]
    pad = jnp.zeros((w.shape[0], LANES - IN_SIZES[4]), w.dtype)
    return jnp.concatenate(parts[0:4] + parts[5:8] + [parts[4], pad], axis=1).astype(BF16)


def _block_diag(w):
    eye = jnp.eye(RG_HEADS, dtype=w.dtype)
    return jnp.einsum("hij,hg->higj", w, eye).reshape(RG_WIDTH, RG_WIDTH)


def _lane_pad(v):
    return jnp.concatenate([v, jnp.zeros((LANES - v.shape[0],), v.dtype)])[None, :]


def kernel(x, c, ctx, c_ctx, w_mod, b_mod, norm1_g, w_in, rg_conv_w, rg_conv_b, rg_w_a, rg_b_a, rg_w_x, rg_b_x, rg_lambda, ssd_conv_w, ssd_conv_b, ssd_dt_bias, ssd_a_log, ssd_d, ssd_norm_g, da_lambda, da_subln_g, w_out, norm2_g, w_gate, w_up, w_down, final_norm_g):
    b, n_lat, _ = x.shape
    assert ctx.shape[1] == CTX_LEN and n_lat % TOK_TILE == 0
    rows = -(-(b + 1) // SUBLANES) * SUBLANES
    c_all = jnp.concatenate([c, c_ctx[None, :], jnp.zeros((rows - b - 1, D_MODEL), F32)], axis=0)
    mods = _modulation(c_all, w_mod, b_mod)
    tables = _rope_tables(n_lat)
    hs = (ctx, x)
    out = None
    for l in range(DEPTH):
        final = l == DEPTH - 1
        rg, sz, sxbc, sdt, q, k, v = _in_proj(hs, mods[l], norm1_g[l][None, :], _permute_w_in(w_in[l]), tables)
        w_gates = jnp.concatenate(
            [_block_diag(w[l, d]) for d in range(2) for w in (rg_w_a, rg_w_x)], axis=1).astype(BF16)
        b_gates = jnp.concatenate(
            [bb[l, d] for d in range(2) for bb in (rg_b_a, rg_b_x)])[None, :]
        y_rg = _rglru(rg, rg_conv_w[l], rg_conv_b[l][None, :], w_gates, b_gates, rg_lambda[l])
        y_ssd = _ssd(sz, sxbc, sdt, ssd_conv_w[l], ssd_conv_b[l][None, :],
                     _lane_pad(ssd_dt_bias[l].reshape(-1)),
                     _lane_pad(-jnp.exp(ssd_a_log[l].reshape(-1))),
                     jnp.repeat(ssd_d[l], SSD_HEADDIM)[None, :], ssd_norm_g[l][None, :])
        y_da = _attention(q, k, v, da_lambda[l], da_subln_g[l][None, :], l)
        res = _out_ffn(hs, y_rg, y_ssd, y_da, mods[l], norm2_g[l][None, :], w_out[l].astype(BF16),
                       w_gate[l].astype(BF16), w_up[l].astype(BF16), w_down[l].astype(BF16),
                       final_norm_g[None, :], final)
        if final:
            out = res
        else:
            hs = (res,)
    return out
```

```python
import functools
import math

import jax
import jax.numpy as jnp
from jax import lax
from jax.experimental import pallas as pl
from jax.experimental.pallas import tpu as pltpu

F32 = jnp.float32
BF16 = jnp.bfloat16

D_MODEL = 1024
DEPTH = 2
CTX_LEN = 256
GRID_W = 64
NORM_EPS = 1e-6

RG_WIDTH = 256
RG_HEADS = 4
RG_BLOCK = 64
RG_C = 8.0

SSD_WIDTH = 256
SSD_HEADDIM = 64
SSD_HEADS = 4
SSD_GROUPS = 2
SSD_STATE = 64
SSD_CHUNK = 128
SSD_XBC = 512

DA_WIDTH = 512
DA_HEADS = 4
DA_HEAD_DIM = 64
DA_V_DIM = 128
ROPE_BASE = 10000.0

D_FF = 2816
IN_SIZES = (256, 256, 256, 512, 8, 512, 512, 512)
D_IN = sum(IN_SIZES)

LANES = 128
SUBLANES = 8
TOK_TILE = 256
INPROJ_BATCH = 4
FFN_BATCH = 2
CONV_PAD = SUBLANES
VMEM_LIMIT = 56 * 1024 * 1024

U_RG = 0
U_SZ = 512
U_XBC = 768
U_Q = 1280
U_K = 1792
U_V = 2304
U_DT = 2816
U_COLS = 2944


def _silu(x):
    hx = 0.5 * x
    return hx * jnp.tanh(hx) + hx


def _softplus(x):
    return jnp.maximum(x, 0.0) + jnp.log(1.0 + jnp.exp(-jnp.abs(x)))


def _rms(x, g):
    ms = jnp.mean(x * x, axis=-1, keepdims=True)
    return x * lax.rsqrt(ms + NORM_EPS) * g


def _const_spec(shape, layer=None):
    nd = len(shape)
    if layer is None:
        return pl.BlockSpec(shape, lambda *_: (0,) * nd, pipeline_mode=pl.Buffered(1))
    return pl.BlockSpec((None,) + tuple(shape), lambda *_: (layer,) + (0,) * nd, pipeline_mode=pl.Buffered(1))


def _params(*sem):
    return pltpu.CompilerParams(dimension_semantics=sem, vmem_limit_bytes=VMEM_LIMIT)


def _mod_kernel(c_ref, w_ref, b_ref, o_ref):
    c = c_ref[...]
    a = _silu(c).astype(BF16)
    o_ref[...] = jnp.dot(a, w_ref[...].astype(BF16), preferred_element_type=F32) + b_ref[...]


def _modulation(c_all, w_mod, b_mod):
    r = c_all.shape[0]
    out = pl.pallas_call(
        _mod_kernel,
        grid=(DEPTH, 6),
        in_specs=[
            pl.BlockSpec((r, D_MODEL), lambda l, j: (0, 0)),
            pl.BlockSpec((None, D_MODEL, D_MODEL), lambda l, j: (l, 0, j)),
            pl.BlockSpec((None, 1, D_MODEL), lambda l, j: (l, 0, j)),
        ],
        out_specs=pl.BlockSpec((None, r, D_MODEL), lambda l, j: (l, 0, j)),
        out_shape=jax.ShapeDtypeStruct((DEPTH, r, 6 * D_MODEL), F32),
        compiler_params=_params("arbitrary", "arbitrary"),
    )(c_all, w_mod, b_mod.reshape(DEPTH, 1, 6 * D_MODEL))
    return out.reshape(DEPTH, r, 6, D_MODEL)


def _rope(x, cos, sin, first_half):
    width = x.shape[-1]
    partner = jnp.where(first_half, pltpu.roll(x, width - 16, 1), pltpu.roll(x, 16, 1))
    return x * cos + partner * sin


def _load_stream(h_refs):
    if len(h_refs) == 1:
        return h_refs[0][...]
    return jnp.where(pl.program_id(0) == 0, h_refs[0][...], h_refs[1][...])


def _stream_specs(hs, b, t0=0, nb=None):
    lead = nb
    if len(hs) == 1:
        return [pl.BlockSpec((lead, TOK_TILE, D_MODEL), lambda t, i: (i, t + t0, 0))]
    assert t0 == 0
    return [pl.BlockSpec((lead, TOK_TILE, D_MODEL), lambda t, i: (jnp.where(t == 0, i, 0), 0, 0)),
            pl.BlockSpec((lead, TOK_TILE, D_MODEL), lambda t, i: (i, jnp.maximum(t - 1, 0), 0))]


def _inproj_kernel(*refs):
    (mod_ref, g_ref, w_ref, cq_ref, sq_ref, ck_ref, sk_ref,
     rg_ref, sz_ref, xbc_ref, dt_ref, q_ref, k_ref, v_ref) = refs[-14:]
    h = _load_stream(refs[:-14])
    g = g_ref[...]
    ys = []
    for j in range(INPROJ_BATCH):
        mod = mod_ref[j]
        ys.append((_rms(h[j], g) * (1.0 + mod[1:2]) + mod[0:1]).astype(BF16))
    us = [jnp.dot(y, w_ref[...], preferred_element_type=F32) for y in ys]
    lane = lax.broadcasted_iota(jnp.int32, (TOK_TILE, DA_WIDTH), 1)
    first_half = (lane % 32) < 16
    for j, u in enumerate(us):
        rg_ref[j] = u[:, U_RG:U_SZ]
        sz_ref[j] = u[:, U_SZ:U_XBC]
        xbc_ref[j] = u[:, U_XBC:U_Q]
        dt_ref[j] = u[:, U_DT:U_COLS]
        q_ref[j] = _rope(u[:, U_Q:U_K], cq_ref[...], sq_ref[...], first_half).astype(BF16)
        k_ref[j] = _rope(u[:, U_K:U_V], ck_ref[...], sk_ref[...], first_half).astype(BF16)
        v_ref[j] = u[:, U_V:U_DT].astype(BF16)


def _in_proj(hs, mod, norm_g, w_in_p, tables):
    b = hs[0].shape[0]
    l = sum(h.shape[1] for h in hs)
    nt = l // TOK_TILE
    nb = INPROJ_BATCH
    assert b % nb == 0
    tok = lambda w: pl.BlockSpec((nb, TOK_TILE, w), lambda t, i: (i, t, 0))
    tab = pl.BlockSpec((TOK_TILE, DA_WIDTH), lambda t, i: (t, 0))
    widths = (512, 256, 512, 128, 512, 512, 512)
    dtypes = (F32, F32, F32, F32, BF16, BF16, BF16)
    return pl.pallas_call(
        _inproj_kernel,
        grid=(nt, b // nb),
        in_specs=_stream_specs(hs, b, nb=nb) + [
            pl.BlockSpec((nb, 6, D_MODEL), lambda t, i: (jnp.where(t == 0, b // nb, i), 0, 0)),
            _const_spec((1, D_MODEL)),
            _const_spec((D_MODEL, U_COLS)),
            tab, tab, tab, tab,
        ],
        out_specs=[tok(w) for w in widths],
        out_shape=[jax.ShapeDtypeStruct((b, l, w), d) for w, d in zip(widths, dtypes)],
        compiler_params=_params("arbitrary", "arbitrary"),
    )(*hs, mod, norm_g, w_in_p, *tables)


def _conv_rows(x_ref, col0, width, r0, rows, w, bias):
    l = x_ref.shape[0]
    assert CTX_LEN % rows == 0
    span = rows + 2 * CONV_PAD
    lo = pl.multiple_of(jnp.maximum(r0 - CONV_PAD, 0), CONV_PAD)
    hi = pl.multiple_of(jnp.minimum(r0 + rows, l - CONV_PAD), CONV_PAD)
    cols = slice(col0, col0 + width)
    seg_start = jnp.logical_or(r0 == 0, r0 == CTX_LEN)
    seg_end = jnp.logical_or(r0 + rows == CTX_LEN, r0 + rows == l)
    xa = jnp.concatenate([jnp.where(seg_start, 0.0, x_ref[pl.ds(lo, CONV_PAD), cols]),
                          x_ref[pl.ds(r0, rows), cols],
                          jnp.where(seg_end, 0.0, x_ref[pl.ds(hi, CONV_PAD), cols])], axis=0)
    acc = None
    for tap in range(4):
        off = tap - 2
        sh = xa if off == 0 else pltpu.roll(xa, (-off) % span, 0)
        term = sh[CONV_PAD:CONV_PAD + rows] * w[tap:tap + 1]
        acc = term if acc is None else acc + term
    return acc + bias


RG_ROWS = 256


def _scan8(a, bx, reverse):
    row = lax.broadcasted_iota(jnp.int32, a.shape, 0)
    for s in (1, 2, 4):
        shift = (SUBLANES - s) if reverse else s
        a_sh = pltpu.roll(a, shift, 0)
        b_sh = pltpu.roll(bx, shift, 0)
        ok = (row < SUBLANES - s) if reverse else (row >= s)
        bx = jnp.where(ok, a * b_sh + bx, bx)
        a = jnp.where(ok, a * a_sh, a)
    return a, bx


def _rglru_kernel(rg_ref, cw_ref, cb_ref, wg_ref, bg_ref, lam_ref, o_ref,
                  af_ref, bf_ref, ab_ref, bb_ref):
    l = rg_ref.shape[0]
    cw = cw_ref[...]
    cb = cb_ref[...]
    bg = bg_ref[...]
    coef = (-0.5 * RG_C) * _softplus(-lam_ref[...])

    def coeffs(i, carry):
        r0 = pl.multiple_of(i * RG_ROWS, RG_ROWS)
        xc = _conv_rows(rg_ref, 0, RG_WIDTH, r0, RG_ROWS, cw, cb)
        t = jnp.tanh(jnp.dot(xc.astype(BF16), wg_ref[...], preferred_element_type=F32) + bg)
        hx = 0.5 * xc
        for d, (a_ref, b_ref) in enumerate(((af_ref, bf_ref), (ab_ref, bb_ref))):
            t_a = t[:, (2 * d) * RG_WIDTH:(2 * d + 1) * RG_WIDTH]
            t_x = t[:, (2 * d + 1) * RG_WIDTH:(2 * d + 2) * RG_WIDTH]
            log_a = coef[d:d + 1] * t_a + coef[d:d + 1]
            a = jnp.exp(log_a)
            one_m_a2 = jnp.tanh(log_a) * (-1.0 - a * a)
            a_ref[pl.ds(r0, RG_ROWS), :] = a
            b_ref[pl.ds(r0, RG_ROWS), :] = jnp.sqrt(one_m_a2) * (hx * t_x + hx)
        return carry

    lax.fori_loop(0, l // RG_ROWS, coeffs, 0)

    nblk = l // SUBLANES
    nctx = CTX_LEN // SUBLANES

    def scan(j, carry):
        hf, hb = carry
        rf = pl.multiple_of(j * SUBLANES, SUBLANES)
        jb = jnp.where(j < nctx, nctx - 1 - j, nblk + nctx - 1 - j)
        rb = pl.multiple_of(jb * SUBLANES, SUBLANES)
        a, bx = _scan8(af_ref[pl.ds(rf, SUBLANES), :], bf_ref[pl.ds(rf, SUBLANES), :], False)
        h = a * hf + bx
        af_ref[pl.ds(rf, SUBLANES), :] = h
        hf = h[SUBLANES - 1:SUBLANES]
        a, bx = _scan8(ab_ref[pl.ds(rb, SUBLANES), :], bb_ref[pl.ds(rb, SUBLANES), :], True)
        h = a * hb + bx
        ab_ref[pl.ds(rb, SUBLANES), :] = h
        hb = h[0:1]
        return hf, hb

    zero = jnp.zeros((1, RG_WIDTH), F32)
    lax.fori_loop(0, nblk, scan, (zero, zero), unroll=4)

    def finish(i, carry):
        r0 = pl.multiple_of(i * RG_ROWS, RG_ROWS)
        hsum = af_ref[pl.ds(r0, RG_ROWS), :] + ab_ref[pl.ds(r0, RG_ROWS), :]
        g = rg_ref[pl.ds(r0, RG_ROWS), RG_WIDTH:2 * RG_WIDTH]
        o_ref[pl.ds(r0, RG_ROWS), :] = (hsum * jax.nn.gelu(g, approximate=True)).astype(BF16)
        return carry

    lax.fori_loop(0, l // RG_ROWS, finish, 0)


def _rglru(rg, conv_w, conv_b, w_gates, b_gates, lam):
    b, l, _ = rg.shape
    seq = lambda w: pl.BlockSpec((None, l, w), lambda i: (i, 0, 0))
    scratch = [pltpu.VMEM((l, RG_WIDTH), F32)] * 4
    return pl.pallas_call(
        _rglru_kernel,
        grid=(b,),
        in_specs=[
            seq(2 * RG_WIDTH),
            _const_spec((4, RG_WIDTH)),
            _const_spec((1, RG_WIDTH)),
            _const_spec((RG_WIDTH, 4 * RG_WIDTH)),
            _const_spec((1, 4 * RG_WIDTH)),
            _const_spec((2, RG_WIDTH)),
        ],
        out_specs=seq(RG_WIDTH),
        out_shape=jax.ShapeDtypeStruct((b, l, RG_WIDTH), BF16),
        scratch_shapes=scratch,
        compiler_params=_params("arbitrary"),
    )(rg, conv_w, conv_b, w_gates, b_gates, lam)


SSD_ROWS = 64
GROUP_W = SSD_WIDTH // SSD_GROUPS
HEADS_PER_GROUP = SSD_HEADS // SSD_GROUPS
STATE_ROWS = SSD_GROUPS * SSD_STATE
NT_DIMS = (((1,), (1,)), ((), ()))
SSD_GROUP_CHUNKS = 3


def _ssd_head_lane(direction, group, j):
    return direction * SSD_HEADS + group * HEADS_PER_GROUP + j


def _ssd_kernel(z_ref, xin_ref, dt_ref, cw_ref, cb_ref, dtb_ref, aneg_ref, dsk_ref, ng_ref, o_ref,
                xbc_ref, dts_ref, a3_ref, y_ref, ecc_ref, ds_ref, sent_ref, etot_ref):
    l = z_ref.shape[0]
    q = SSD_CHUNK
    assert HEADS_PER_GROUP == 2 and GROUP_W == LANES and STATE_ROWS == LANES
    cw = cw_ref[...]
    cb = cb_ref[...]
    dtb = dtb_ref[...]
    dsk = dsk_ref[...]
    aneg = aneg_ref[...] * math.log2(math.e)

    def prep(i, carry):
        r0 = pl.multiple_of(i * SSD_ROWS, SSD_ROWS)
        rows = pl.ds(r0, SSD_ROWS)
        xbc = _silu(_conv_rows(xin_ref, 0, SSD_XBC, r0, SSD_ROWS, cw, cb))
        xbc_ref[rows, :] = xbc
        dt = _softplus(dt_ref[rows, :] + dtb)
        dts_ref[rows, :] = dt
        y_ref[rows, :] = xbc[:, 0:SSD_WIDTH] * dsk
        a = dt * aneg
        hi = a.astype(BF16).astype(F32)
        r1 = a - hi
        mid = r1.astype(BF16).astype(F32)
        lo = r1 - mid
        a3_ref[rows, :] = (hi + pltpu.roll(mid, 2 * SSD_HEADS, 1) + pltpu.roll(lo, 4 * SSD_HEADS, 1)).astype(BF16)
        return carry

    lax.fori_loop(0, l // SSD_ROWS, prep, 0, unroll=3)

    ri = lax.broadcasted_iota(jnp.int32, (q, q), 0)
    ci = lax.broadcasted_iota(jnp.int32, (q, q), 1)
    lower = ri >= ci
    upper = ci >= ri
    tri_lo = lower.astype(F32).astype(BF16)
    tri_up = upper.astype(F32).astype(BF16)
    eye16 = (ri == ci).astype(F32).astype(BF16)
    eye8 = (lax.broadcasted_iota(jnp.int32, (SUBLANES, LANES), 0)
            == lax.broadcasted_iota(jnp.int32, (SUBLANES, LANES), 1)).astype(F32)
    lane = lax.broadcasted_iota(jnp.int32, (q, LANES), 1)
    left = lane < SSD_HEADDIM
    lane_row = lax.broadcasted_iota(jnp.int32, (1, LANES), 1)
    left_row = lane_row < SSD_HEADDIM
    sub8 = lax.broadcasted_iota(jnp.int32, (SUBLANES, q), 0)
    left_state = lax.broadcasted_iota(jnp.int32, (SSD_STATE, GROUP_W), 1) < SSD_HEADDIM
    nchunk = l // q
    nctx = CTX_LEN // q

    def local(cg, carry):
        chunks = [cg * SSD_GROUP_CHUNKS + k for k in range(SSD_GROUP_CHUNKS)]
        rows = [pl.ds(pl.multiple_of(c * q, q), q) for c in chunks]
        hi = lax.Precision.HIGHEST

        def fold(cs):
            return cs + pltpu.roll(cs, LANES - 2 * SSD_HEADS, 1) + pltpu.roll(cs, LANES - 4 * SSD_HEADS, 1)

        x16, bm_t, grams, cs_lo, cs_up = [], [], [], [], []
        for r in rows:
            x16.append(xbc_ref[r, 0:SSD_WIDTH].astype(BF16))
            bm16 = xbc_ref[r, SSD_WIDTH:SSD_WIDTH + LANES].astype(BF16)
            cm = xbc_ref[r, SSD_WIDTH + LANES:SSD_XBC]
            bm_t.append(lax.dot_general(eye16, bm16, NT_DIMS, preferred_element_type=F32))
            gr = []
            for g in range(SSD_GROUPS):
                cm_g = jnp.where((lane < SSD_STATE) == (g == 0), cm, 0.0).astype(BF16)
                gr.append(lax.dot_general(cm_g, bm16, NT_DIMS, preferred_element_type=F32))
            grams.append(gr)
            a3 = a3_ref[r, :]
            cs_lo.append(jnp.dot(tri_lo, a3, preferred_element_type=F32))
            cs_up.append(jnp.dot(tri_up, a3, preferred_element_type=F32))

        c_col, c_row, dt_row = [], [], []
        for k, r in enumerate(rows):
            cc = jnp.where(lane < SSD_HEADS, fold(cs_lo[k]), fold(cs_up[k]))
            c_col.append(cc)
            c_row.append(lax.dot_general(eye8, cc, NT_DIMS, precision=hi, preferred_element_type=F32))
            dt_row.append(lax.dot_general(eye8, dts_ref[r, :], NT_DIMS, precision=hi,
                                          preferred_element_type=F32))

        zblock = jnp.zeros((SSD_STATE, GROUP_W), F32)
        for k, (c, r) in enumerate(zip(chunks, rows)):
            tot_col = jnp.where(sub8 < SSD_HEADS, c_row[k][:, q - 1:q], c_row[k][:, 0:1])
            dtw_row = dt_row[k] * jnp.exp2(tot_col - c_row[k])
            tot_row = jnp.where(lane_row < SSD_HEADS, c_col[k][q - 1:q, :], c_col[k][0:1, :])
            e_tot = jnp.exp2(tot_row)
            for d in range(2):
                mask = lower if d == 0 else upper
                ds = []
                etot = []
                for g in range(SSD_GROUPS):
                    xg16 = x16[k][:, g * GROUP_W:(g + 1) * GROUP_W]
                    la = _ssd_head_lane(d, g, 0)
                    lb = _ssd_head_lane(d, g, 1)
                    yd = []
                    dsh = []
                    ecc = []
                    for li in (la, lb):
                        ccb = jnp.broadcast_to(c_col[k][:, li:li + 1], (q, q))
                        decay = jnp.exp2(jnp.where(mask, ccb - c_row[k][li:li + 1, :], -1e30))
                        m = (grams[k][g] * decay * dt_row[k][li:li + 1, :]).astype(BF16)
                        yd.append(jnp.dot(m, xg16, preferred_element_type=F32))
                        ecc.append(jnp.exp2(ccb))
                        b_t = (bm_t[k][g * SSD_STATE:(g + 1) * SSD_STATE, :] * dtw_row[li:li + 1, :]).astype(BF16)
                        dsh.append(jnp.dot(b_t, xg16, preferred_element_type=F32))
                    cols = slice(g * GROUP_W, (g + 1) * GROUP_W)
                    y_ref[r, cols] += jnp.where(left, yd[0], yd[1])
                    ecc_ref[d, r, cols] = jnp.where(left, ecc[0], ecc[1])
                    dsg = jnp.where(left_state, dsh[0], dsh[1])
                    ds.append(jnp.concatenate([dsg, zblock] if g == 0 else [zblock, dsg], axis=1))
                    etot.append(jnp.where(left_row, e_tot[:, la:la + 1], e_tot[:, lb:lb + 1]))
                ds_ref[d, pl.ds(pl.multiple_of(c * STATE_ROWS, STATE_ROWS), STATE_ROWS), :] = (
                    jnp.concatenate(ds, axis=0))
                etot_ref[d, pl.ds(pl.multiple_of(c * SUBLANES, SUBLANES), SUBLANES), :] = (
                    jnp.broadcast_to(jnp.concatenate(etot, axis=1), (SUBLANES, SSD_WIDTH)))
        return carry

    assert nchunk % SSD_GROUP_CHUNKS == 0
    lax.fori_loop(0, nchunk // SSD_GROUP_CHUNKS, local, 0)

    def carry_state(j, state):
        sf, sb = state
        jb = jnp.where(j < nctx, nctx - 1 - j, nchunk + nctx - 1 - j)
        out = []
        for d, (c, s) in enumerate(((j, sf), (jb, sb))):
            srows = pl.ds(pl.multiple_of(c * STATE_ROWS, STATE_ROWS), STATE_ROWS)
            sent_ref[d, srows, :] = s.astype(BF16)
            e = etot_ref[d, pl.ds(pl.multiple_of(c * SUBLANES, SUBLANES), 1), :]
            out.append(e * s + ds_ref[d, srows, :])
        return tuple(out)

    zero = jnp.zeros((STATE_ROWS, SSD_WIDTH), F32)
    lax.fori_loop(0, nchunk, carry_state, (zero, zero))

    ng = ng_ref[...]

    def finish(c, carry):
        r0 = pl.multiple_of(c * q, q)
        rows = pl.ds(r0, q)
        srows = pl.ds(pl.multiple_of(c * STATE_ROWS, STATE_ROWS), STATE_ROWS)
        cm16 = xbc_ref[rows, SSD_WIDTH + LANES:SSD_XBC].astype(BF16)
        y = y_ref[rows, :]
        for d in range(2):
            y = y + ecc_ref[d, rows, :] * jnp.dot(cm16, sent_ref[d, srows, :], preferred_element_type=F32)
        v = y * _silu(z_ref[rows, :])
        parts = []
        for g in range(SSD_GROUPS):
            vg = v[:, g * GROUP_W:(g + 1) * GROUP_W]
            parts.append(vg * lax.rsqrt(jnp.mean(vg * vg, axis=-1, keepdims=True) + NORM_EPS))
        o_ref[rows, :] = (jnp.concatenate(parts, axis=1) * ng).astype(BF16)
        return carry

    lax.fori_loop(0, nchunk, finish, 0, unroll=3)


def _ssd(sz, sxbc, sdt, conv_w, conv_b, dt_bias, aneg, dskip, norm_g):
    b, l, _ = sz.shape
    nchunk = l // SSD_CHUNK
    seq = lambda w: pl.BlockSpec((None, l, w), lambda i: (i, 0, 0))
    scratch = [
        pltpu.VMEM((l, SSD_XBC), F32),
        pltpu.VMEM((l, LANES), F32),
        pltpu.VMEM((l, LANES), BF16),
        pltpu.VMEM((l, SSD_WIDTH), F32),
        pltpu.VMEM((2, l, SSD_WIDTH), F32),
        pltpu.VMEM((2, nchunk * STATE_ROWS, SSD_WIDTH), F32),
        pltpu.VMEM((2, nchunk * STATE_ROWS, SSD_WIDTH), BF16),
        pltpu.VMEM((2, nchunk * SUBLANES, SSD_WIDTH), F32),
    ]
    return pl.pallas_call(
        _ssd_kernel,
        grid=(b,),
        in_specs=[
            seq(SSD_WIDTH), seq(SSD_XBC), seq(LANES),
            _const_spec((4, SSD_XBC)),
            _const_spec((1, SSD_XBC)),
            _const_spec((1, LANES)),
            _const_spec((1, LANES)),
            _const_spec((1, SSD_WIDTH)),
            _const_spec((1, SSD_WIDTH)),
        ],
        out_specs=seq(SSD_WIDTH),
        out_shape=jax.ShapeDtypeStruct((b, l, SSD_WIDTH), BF16),
        scratch_shapes=scratch,
        compiler_params=_params("arbitrary"),
    )(sz, sxbc, sdt, conv_w, conv_b, dt_bias, aneg, dskip, norm_g)


def _attn_probs(q, k_ref, p_ref, nkeys):
    l = k_ref.shape[0]
    for c in range(2):
        qc = q[:, c * DA_HEAD_DIM:(c + 1) * DA_HEAD_DIM]
        kc = k_ref[0:nkeys, c * DA_HEAD_DIM:(c + 1) * DA_HEAD_DIM]
        s = lax.dot_general(qc, kc, (((1,), (1,)), ((), ())), preferred_element_type=F32)
        p_ref[:, c * l:c * l + nkeys] = jnp.exp2(s - jnp.max(s, axis=-1, keepdims=True)).astype(BF16)


def _attn_values(p_ref, vaug_ref, nkeys, lam, g, lam_init):
    l = vaug_ref.shape[0]
    outs = []
    for c in range(2):
        ov = jnp.dot(p_ref[:, c * l:c * l + nkeys], vaug_ref[0:nkeys, :], preferred_element_type=F32)
        outs.append(ov[:, 0:DA_V_DIM] / ov[:, DA_V_DIM:DA_V_DIM + 1])
    o = outs[0] - lam * outs[1]
    return (_rms(o, g) * (1.0 - lam_init)).astype(BF16)


def _attn_kernel(q_ref, k_ref, v_ref, lam_ref, g_ref, o_ref, vaug_ref, pa_ref, pb_ref, *, lam_init):
    l = q_ref.shape[0]
    nt = l // TOK_TILE
    assert CTX_LEN == TOK_TILE and nt % 2 == 1 and nt >= 3
    lv = lam_ref[...]
    lam = (jnp.exp(jnp.sum(lv[0:1] * lv[1:2], axis=-1, keepdims=True))
           - jnp.exp(jnp.sum(lv[2:3] * lv[3:4], axis=-1, keepdims=True)) + lam_init)
    g = g_ref[...]
    vaug_ref[:, 0:DA_V_DIM] = v_ref[...]
    vaug_ref[:, DA_V_DIM:2 * DA_V_DIM] = jnp.ones((l, DA_V_DIM), BF16)

    def rows(t):
        return pl.ds(pl.multiple_of(t * TOK_TILE, TOK_TILE), TOK_TILE)

    def probs(t, p_ref, nkeys=l):
        _attn_probs(q_ref[rows(t), :], k_ref, p_ref, nkeys)

    def values(t, p_ref, nkeys=l):
        o_ref[rows(t), :] = _attn_values(p_ref, vaug_ref, nkeys, lam, g, lam_init)

    probs(0, pa_ref, CTX_LEN)
    probs(1, pb_ref)
    values(0, pa_ref, CTX_LEN)

    def pair(i, carry):
        t = 2 * i
        probs(t, pa_ref)
        values(t - 1, pb_ref)
        probs(t + 1, pb_ref)
        values(t, pa_ref)
        return carry

    lax.fori_loop(1, nt // 2, pair, 0)
    probs(nt - 1, pa_ref)
    values(nt - 2, pb_ref)
    values(nt - 1, pa_ref)


def _attention(q, k, v, lam_vec, subln_g, layer_idx):
    b, l, _ = q.shape
    lam_init = 0.8 - 0.6 * math.exp(-0.3 * layer_idx)
    seq = pl.BlockSpec((None, l, DA_V_DIM), lambda i, h: (i, 0, h))
    return pl.pallas_call(
        functools.partial(_attn_kernel, lam_init=lam_init),
        grid=(b, DA_HEADS),
        in_specs=[seq, seq, seq, _const_spec((4, DA_HEAD_DIM)), _const_spec((1, DA_V_DIM))],
        out_specs=seq,
        out_shape=jax.ShapeDtypeStruct((b, l, DA_WIDTH), BF16),
        scratch_shapes=[pltpu.VMEM((l, 2 * DA_V_DIM), BF16)] + [pltpu.VMEM((TOK_TILE, 2 * l), BF16)] * 2,
        compiler_params=_params("arbitrary", "arbitrary"),
    )(q, k, v, lam_vec, subln_g)


def _out_ffn_kernel(*refs, final):
    rg_ref, ssd_ref, da_ref, mod_ref, g2_ref, wo_ref, wg_ref, wu_ref, wd_ref, gf_ref, o_ref = refs[-11:]
    h_in = _load_stream(refs[:-11])
    g2 = g2_ref[...]
    nb = FFN_BATCH
    mods = [mod_ref[j] for j in range(nb)]
    mixes = []
    for j in range(nb):
        mix = jnp.dot(rg_ref[j], wo_ref[0:RG_WIDTH, :], preferred_element_type=F32)
        mix += jnp.dot(ssd_ref[j], wo_ref[RG_WIDTH:RG_WIDTH + SSD_WIDTH, :], preferred_element_type=F32)
        mix += jnp.dot(da_ref[j], wo_ref[RG_WIDTH + SSD_WIDTH:, :], preferred_element_type=F32)
        mixes.append(mix)
    hs = [h_in[j] + mods[j][2:3] * mixes[j] for j in range(nb)]
    ys = [(_rms(hs[j], g2) * (1.0 + mods[j][4:5]) + mods[j][3:4]).astype(BF16) for j in range(nb)]
    gates, ups = [], []
    for y in ys:
        gates.append(jnp.dot(y, wg_ref[...], preferred_element_type=F32))
        ups.append(jnp.dot(y, wu_ref[...], preferred_element_type=F32))
    acts = [(_silu(gates[j]) * ups[j]).astype(BF16) for j in range(nb)]
    downs = [jnp.dot(a, wd_ref[...], preferred_element_type=F32) for a in acts]
    for j in range(nb):
        h = hs[j] + mods[j][5:6] * downs[j]
        o_ref[j] = _rms(h, gf_ref[...]) if final else h


def _out_ffn(hs, rg, ssd, da, mod, norm2_g, w_out, w_gate, w_up, w_down, final_g, layer, final):
    b, l, _ = rg.shape
    nb = FFN_BATCH
    assert b % nb == 0
    t0 = CTX_LEN // TOK_TILE if final else 0
    nt = l // TOK_TILE - t0
    tok = lambda w: pl.BlockSpec((nb, TOK_TILE, w), lambda t, i: (i, t + t0, 0))
    out_rows = nt * TOK_TILE
    kwargs = {"input_output_aliases": {0: 0}} if (len(hs) == 1 and not final) else {}
    return pl.pallas_call(
        functools.partial(_out_ffn_kernel, final=final),
        grid=(nt, b // nb),
        in_specs=_stream_specs(hs, b, t0, nb) + [
            tok(RG_WIDTH), tok(SSD_WIDTH), tok(DA_WIDTH),
            pl.BlockSpec((nb, 6, D_MODEL), lambda t, i: (jnp.where(t + t0 == 0, b // nb, i), 0, 0)),
            _const_spec((1, D_MODEL)),
            _const_spec((D_MODEL, D_MODEL), layer),
            _const_spec((D_MODEL, D_FF), layer),
            _const_spec((D_MODEL, D_FF), layer),
            _const_spec((D_FF, D_MODEL), layer),
            _const_spec((1, D_MODEL)),
        ],
        out_specs=pl.BlockSpec((nb, TOK_TILE, D_MODEL), lambda t, i: (i, t, 0)),
        out_shape=jax.ShapeDtypeStruct((b, out_rows, D_MODEL), F32),
        compiler_params=_params("arbitrary", "arbitrary"),
        **kwargs,
    )(*hs, rg, ssd, da, mod, norm2_g, w_out, w_gate, w_up, w_down, final_g)


def _rope_tables(n_lat):
    half = DA_HEAD_DIM // 2
    inv_freq = jnp.power(ROPE_BASE, -jnp.arange(0, half, 2, dtype=F32) / half)
    t = jnp.arange(n_lat, dtype=jnp.int32)
    ang_r = (t // GRID_W).astype(F32)[:, None] * inv_freq
    ang_c = (t % GRID_W).astype(F32)[:, None] * inv_freq
    cos = jnp.concatenate([jnp.cos(ang_r), jnp.cos(ang_r), jnp.cos(ang_c), jnp.cos(ang_c)], axis=1)
    sin = jnp.concatenate([-jnp.sin(ang_r), jnp.sin(ang_r), -jnp.sin(ang_c), jnp.sin(ang_c)], axis=1)
    reps = DA_WIDTH // DA_HEAD_DIM
    cos = jnp.concatenate([jnp.ones((CTX_LEN, DA_HEAD_DIM), F32), cos], axis=0)
    sin = jnp.concatenate([jnp.zeros((CTX_LEN, DA_HEAD_DIM), F32), sin], axis=0)
    cos = jnp.tile(cos, (1, reps))
    sin = jnp.tile(sin, (1, reps))
    scale = DA_HEAD_DIM ** -0.5 * math.log2(math.e)
    return cos * scale, sin * scale, cos, sin


def _permute_w_in(w):
    offs = [0]
    for s in IN_SIZES:
        offs.append(offs[-1] + s)
    parts = [w[:, offs[i]:offs[i + 1]] for i in range(len(IN_SIZES))]
    pad = jnp.zeros((w.shape[0], LANES - IN_SIZES[4]), w.dtype)
    return jnp.concatenate(parts[0:4] + parts[5:8] + [parts[4], pad], axis=1).astype(BF16)


def _block_diag(w):
    eye = jnp.eye(RG_HEADS, dtype=w.dtype)
    return jnp.einsum("hij,hg->higj", w, eye).reshape(RG_WIDTH, RG_WIDTH)


def _lane_pad(v):
    return jnp.concatenate([v, jnp.zeros((LANES - v.shape[0],), v.dtype)])[None, :]


def kernel(x, c, ctx, c_ctx, w_mod, b_mod, norm1_g, w_in, rg_conv_w, rg_conv_b, rg_w_a, rg_b_a, rg_w_x, rg_b_x, rg_lambda, ssd_conv_w, ssd_conv_b, ssd_dt_bias, ssd_a_log, ssd_d, ssd_norm_g, da_lambda, da_subln_g, w_out, norm2_g, w_gate, w_up, w_down, final_norm_g):
    b, n_lat, _ = x.shape
    assert ctx.shape[1] == CTX_LEN and n_lat % TOK_TILE == 0
    n_ctx = max(INPROJ_BATCH, FFN_BATCH)
    rows = -(-(b + n_ctx) // SUBLANES) * SUBLANES
    c_all = jnp.concatenate([c] + [c_ctx[None, :]] * n_ctx
                            + [jnp.zeros((rows - b - n_ctx, D_MODEL), F32)], axis=0)
    mods = _modulation(c_all, w_mod, b_mod)
    tables = _rope_tables(n_lat)
    hs = (ctx, x)
    w_out16, w_gate16, w_up16, w_down16 = (w.astype(BF16) for w in (w_out, w_gate, w_up, w_down))
    out = None
    for l in range(DEPTH):
        final = l == DEPTH - 1
        rg, sz, sxbc, sdt, q, k, v = _in_proj(hs, mods[l], norm1_g[l][None, :], _permute_w_in(w_in[l]), tables)
        w_gates = (0.5 * jnp.concatenate(
            [_block_diag(w[l, d]) for d in range(2) for w in (rg_w_a, rg_w_x)], axis=1)).astype(BF16)
        b_gates = 0.5 * jnp.concatenate(
            [bb[l, d] for d in range(2) for bb in (rg_b_a, rg_b_x)])[None, :]
        y_rg = _rglru(rg, rg_conv_w[l], rg_conv_b[l][None, :], w_gates, b_gates, rg_lambda[l])
        y_ssd = _ssd(sz, sxbc, sdt, ssd_conv_w[l], ssd_conv_b[l][None, :],
                     _lane_pad(ssd_dt_bias[l].reshape(-1)),
                     _lane_pad(-jnp.exp(ssd_a_log[l].reshape(-1))),
                     jnp.repeat(ssd_d[l], SSD_HEADDIM)[None, :], ssd_norm_g[l][None, :])
        y_da = _attention(q, k, v, da_lambda[l], da_subln_g[l][None, :], l)
        res = _out_ffn(hs, y_rg, y_ssd, y_da, mods[l], norm2_g[l][None, :], w_out16, w_gate16, w_up16,
                       w_down16, final_norm_g[None, :], l, final)
        if final:
            out = res
        else:
            hs = (res,)
    return out
```

```python
import functools
import math

import jax
import jax.numpy as jnp
from jax import lax
from jax.experimental import pallas as pl
from jax.experimental.pallas import tpu as pltpu

F32 = jnp.float32
BF16 = jnp.bfloat16

D_MODEL = 1024
DEPTH = 2
CTX_LEN = 256
GRID_W = 64
NORM_EPS = 1e-6

RG_WIDTH = 256
RG_HEADS = 4
RG_BLOCK = 64
RG_C = 8.0

SSD_WIDTH = 256
SSD_HEADDIM = 64
SSD_HEADS = 4
SSD_GROUPS = 2
SSD_STATE = 64
SSD_CHUNK = 128
SSD_XBC = 512

DA_WIDTH = 512
DA_HEADS = 4
DA_HEAD_DIM = 64
DA_V_DIM = 128
ROPE_BASE = 10000.0

D_FF = 2816
IN_SIZES = (256, 256, 256, 512, 8, 512, 512, 512)
D_IN = sum(IN_SIZES)

LANES = 128
SUBLANES = 8
TOK_TILE = 256
INPROJ_BATCH = 4
FFN_BATCH = 2
CONV_PAD = SUBLANES
VMEM_LIMIT = 56 * 1024 * 1024

U_RG = 0
U_SZ = 512
U_XBC = 768
U_Q = 1280
U_K = 1792
U_V = 2304
U_DT = 2816
U_COLS = 2944


def _silu(x):
    hx = 0.5 * x
    return hx * jnp.tanh(hx) + hx


def _softplus(x):
    return jnp.maximum(x, 0.0) + jnp.log(1.0 + jnp.exp(-jnp.abs(x)))


def _rms(x, g):
    ms = jnp.mean(x * x, axis=-1, keepdims=True)
    return x * lax.rsqrt(ms + NORM_EPS) * g


def _const_spec(shape, layer=None):
    nd = len(shape)
    if layer is None:
        return pl.BlockSpec(shape, lambda *_: (0,) * nd, pipeline_mode=pl.Buffered(1))
    return pl.BlockSpec((None,) + tuple(shape), lambda *_: (layer,) + (0,) * nd, pipeline_mode=pl.Buffered(1))


def _params(*sem):
    return pltpu.CompilerParams(dimension_semantics=sem, vmem_limit_bytes=VMEM_LIMIT)


def _mod_kernel(c_ref, w_ref, b_ref, o_ref):
    c = c_ref[...]
    a = _silu(c).astype(BF16)
    o_ref[...] = jnp.dot(a, w_ref[...].astype(BF16), preferred_element_type=F32) + b_ref[...]


def _modulation(c_all, w_mod, b_mod):
    r = c_all.shape[0]
    out = pl.pallas_call(
        _mod_kernel,
        grid=(DEPTH, 6),
        in_specs=[
            pl.BlockSpec((r, D_MODEL), lambda l, j: (0, 0)),
            pl.BlockSpec((None, D_MODEL, D_MODEL), lambda l, j: (l, 0, j)),
            pl.BlockSpec((None, 1, D_MODEL), lambda l, j: (l, 0, j)),
        ],
        out_specs=pl.BlockSpec((None, r, D_MODEL), lambda l, j: (l, 0, j)),
        out_shape=jax.ShapeDtypeStruct((DEPTH, r, 6 * D_MODEL), F32),
        compiler_params=_params("arbitrary", "arbitrary"),
    )(c_all, w_mod, b_mod.reshape(DEPTH, 1, 6 * D_MODEL))
    return out.reshape(DEPTH, r, 6, D_MODEL)


def _rope(x, cos, sin, first_half):
    width = x.shape[-1]
    partner = jnp.where(first_half, pltpu.roll(x, width - 16, 1), pltpu.roll(x, 16, 1))
    return x * cos + partner * sin


def _load_stream(h_refs):
    if len(h_refs) == 1:
        return h_refs[0][...]
    return jnp.where(pl.program_id(0) == 0, h_refs[0][...], h_refs[1][...])


def _stream_specs(hs, b, t0=0, nb=None):
    lead = nb
    if len(hs) == 1:
        return [pl.BlockSpec((lead, TOK_TILE, D_MODEL), lambda t, i: (i, t + t0, 0))]
    assert t0 == 0
    return [pl.BlockSpec((lead, TOK_TILE, D_MODEL), lambda t, i: (jnp.where(t == 0, i, 0), 0, 0)),
            pl.BlockSpec((lead, TOK_TILE, D_MODEL), lambda t, i: (i, jnp.maximum(t - 1, 0), 0))]


def _inproj_kernel(*refs):
    (mod_ref, g_ref, w_ref, cq_ref, sq_ref, ck_ref, sk_ref,
     rg_ref, sz_ref, xbc_ref, dt_ref, q_ref, k_ref, v_ref) = refs[-14:]
    h = _load_stream(refs[:-14])
    g = g_ref[...]
    ys = []
    for j in range(INPROJ_BATCH):
        mod = mod_ref[j]
        ys.append((_rms(h[j], g) * (1.0 + mod[1:2]) + mod[0:1]).astype(BF16))
    us = [jnp.dot(y, w_ref[...], preferred_element_type=F32) for y in ys]
    lane = lax.broadcasted_iota(jnp.int32, (TOK_TILE, DA_WIDTH), 1)
    first_half = (lane % 32) < 16
    for j, u in enumerate(us):
        rg_ref[j] = u[:, U_RG:U_SZ]
        sz_ref[j] = u[:, U_SZ:U_XBC]
        xbc_ref[j] = u[:, U_XBC:U_Q]
        dt_ref[j] = u[:, U_DT:U_COLS]
        q_ref[j] = _rope(u[:, U_Q:U_K], cq_ref[...], sq_ref[...], first_half).astype(BF16)
        k_ref[j] = _rope(u[:, U_K:U_V], ck_ref[...], sk_ref[...], first_half).astype(BF16)
        v_ref[j] = u[:, U_V:U_DT].astype(BF16)


def _in_proj(hs, mod, norm_g, w_in_p, tables):
    b = hs[0].shape[0]
    l = sum(h.shape[1] for h in hs)
    nt = l // TOK_TILE
    nb = INPROJ_BATCH
    assert b % nb == 0
    tok = lambda w: pl.BlockSpec((nb, TOK_TILE, w), lambda t, i: (i, t, 0))
    tab = pl.BlockSpec((TOK_TILE, DA_WIDTH), lambda t, i: (t, 0))
    widths = (512, 256, 512, 128, 512, 512, 512)
    dtypes = (F32, F32, F32, F32, BF16, BF16, BF16)
    return pl.pallas_call(
        _inproj_kernel,
        grid=(nt, b // nb),
        in_specs=_stream_specs(hs, b, nb=nb) + [
            pl.BlockSpec((nb, 6, D_MODEL), lambda t, i: (jnp.where(t == 0, b // nb, i), 0, 0)),
            _const_spec((1, D_MODEL)),
            _const_spec((D_MODEL, U_COLS)),
            tab, tab, tab, tab,
        ],
        out_specs=[tok(w) for w in widths],
        out_shape=[jax.ShapeDtypeStruct((b, l, w), d) for w, d in zip(widths, dtypes)],
        compiler_params=_params("arbitrary", "arbitrary"),
    )(*hs, mod, norm_g, w_in_p, *tables)


def _conv_rows(x_ref, col0, width, r0, rows, w, bias):
    l = x_ref.shape[0]
    assert CTX_LEN % rows == 0
    span = rows + 2 * CONV_PAD
    lo = pl.multiple_of(jnp.maximum(r0 - CONV_PAD, 0), CONV_PAD)
    hi = pl.multiple_of(jnp.minimum(r0 + rows, l - CONV_PAD), CONV_PAD)
    cols = slice(col0, col0 + width)
    seg_start = jnp.logical_or(r0 == 0, r0 == CTX_LEN)
    seg_end = jnp.logical_or(r0 + rows == CTX_LEN, r0 + rows == l)
    xa = jnp.concatenate([jnp.where(seg_start, 0.0, x_ref[pl.ds(lo, CONV_PAD), cols]),
                          x_ref[pl.ds(r0, rows), cols],
                          jnp.where(seg_end, 0.0, x_ref[pl.ds(hi, CONV_PAD), cols])], axis=0)
    acc = None
    for tap in range(4):
        off = tap - 2
        sh = xa if off == 0 else pltpu.roll(xa, (-off) % span, 0)
        term = sh[CONV_PAD:CONV_PAD + rows] * w[tap:tap + 1]
        acc = term if acc is None else acc + term
    return acc + bias


RG_ROWS = 256


def _scan8(a, bx, reverse):
    row = lax.broadcasted_iota(jnp.int32, a.shape, 0)
    for s in (1, 2, 4):
        shift = (SUBLANES - s) if reverse else s
        a_sh = pltpu.roll(a, shift, 0)
        b_sh = pltpu.roll(bx, shift, 0)
        ok = (row < SUBLANES - s) if reverse else (row >= s)
        bx = jnp.where(ok, a * b_sh + bx, bx)
        a = jnp.where(ok, a * a_sh, a)
    return a, bx


def _rglru_kernel(rg_ref, cw_ref, cb_ref, wg_ref, bg_ref, lam_ref, o_ref,
                  af_ref, bf_ref, ab_ref, bb_ref):
    l = rg_ref.shape[0]
    cw = cw_ref[...]
    cb = cb_ref[...]
    bg = bg_ref[...]
    coef = (-0.5 * RG_C) * _softplus(-lam_ref[...])

    def coeffs(i, carry):
        r0 = pl.multiple_of(i * RG_ROWS, RG_ROWS)
        xc = _conv_rows(rg_ref, 0, RG_WIDTH, r0, RG_ROWS, cw, cb)
        t = jnp.tanh(jnp.dot(xc.astype(BF16), wg_ref[...], preferred_element_type=F32) + bg)
        hx = 0.5 * xc
        for d, (a_ref, b_ref) in enumerate(((af_ref, bf_ref), (ab_ref, bb_ref))):
            t_a = t[:, (2 * d) * RG_WIDTH:(2 * d + 1) * RG_WIDTH]
            t_x = t[:, (2 * d + 1) * RG_WIDTH:(2 * d + 2) * RG_WIDTH]
            log_a = coef[d:d + 1] * t_a + coef[d:d + 1]
            a = jnp.exp(log_a)
            one_m_a2 = jnp.tanh(log_a) * (-1.0 - a * a)
            a_ref[pl.ds(r0, RG_ROWS), :] = a
            b_ref[pl.ds(r0, RG_ROWS), :] = jnp.sqrt(one_m_a2) * (hx * t_x + hx)
        return carry

    lax.fori_loop(0, l // RG_ROWS, coeffs, 0)

    nblk = l // SUBLANES
    nctx = CTX_LEN // SUBLANES

    def scan(j, carry):
        hf, hb = carry
        rf = pl.multiple_of(j * SUBLANES, SUBLANES)
        jb = jnp.where(j < nctx, nctx - 1 - j, nblk + nctx - 1 - j)
        rb = pl.multiple_of(jb * SUBLANES, SUBLANES)
        a, bx = _scan8(af_ref[pl.ds(rf, SUBLANES), :], bf_ref[pl.ds(rf, SUBLANES), :], False)
        h = a * hf + bx
        af_ref[pl.ds(rf, SUBLANES), :] = h
        hf = h[SUBLANES - 1:SUBLANES]
        a, bx = _scan8(ab_ref[pl.ds(rb, SUBLANES), :], bb_ref[pl.ds(rb, SUBLANES), :], True)
        h = a * hb + bx
        ab_ref[pl.ds(rb, SUBLANES), :] = h
        hb = h[0:1]
        return hf, hb

    zero = jnp.zeros((1, RG_WIDTH), F32)
    lax.fori_loop(0, nblk, scan, (zero, zero), unroll=4)

    def finish(i, carry):
        r0 = pl.multiple_of(i * RG_ROWS, RG_ROWS)
        hsum = af_ref[pl.ds(r0, RG_ROWS), :] + ab_ref[pl.ds(r0, RG_ROWS), :]
        g = rg_ref[pl.ds(r0, RG_ROWS), RG_WIDTH:2 * RG_WIDTH]
        o_ref[pl.ds(r0, RG_ROWS), :] = (hsum * jax.nn.gelu(g, approximate=True)).astype(BF16)
        return carry

    lax.fori_loop(0, l // RG_ROWS, finish, 0)


def _rglru(rg, conv_w, conv_b, w_gates, b_gates, lam):
    b, l, _ = rg.shape
    seq = lambda w: pl.BlockSpec((None, l, w), lambda i: (i, 0, 0))
    scratch = [pltpu.VMEM((l, RG_WIDTH), F32)] * 4
    return pl.pallas_call(
        _rglru_kernel,
        grid=(b,),
        in_specs=[
            seq(2 * RG_WIDTH),
            _const_spec((4, RG_WIDTH)),
            _const_spec((1, RG_WIDTH)),
            _const_spec((RG_WIDTH, 4 * RG_WIDTH)),
            _const_spec((1, 4 * RG_WIDTH)),
            _const_spec((2, RG_WIDTH)),
        ],
        out_specs=seq(RG_WIDTH),
        out_shape=jax.ShapeDtypeStruct((b, l, RG_WIDTH), BF16),
        scratch_shapes=scratch,
        compiler_params=_params("arbitrary"),
    )(rg, conv_w, conv_b, w_gates, b_gates, lam)


SSD_ROWS = 64
GROUP_W = SSD_WIDTH // SSD_GROUPS
HEADS_PER_GROUP = SSD_HEADS // SSD_GROUPS
STATE_ROWS = SSD_GROUPS * SSD_STATE
NT_DIMS = (((1,), (1,)), ((), ()))
SSD_GROUP_CHUNKS = 9


def _ssd_head_lane(direction, group, j):
    return direction * SSD_HEADS + group * HEADS_PER_GROUP + j


def _ssd_kernel(z_ref, xin_ref, dt_ref, cw_ref, cb_ref, dtb_ref, aneg_ref, dsk_ref, ng_ref, o_ref,
                xbc_ref, dts_ref, a3_ref, y_ref, ecc_ref, ds_ref, sent_ref, etot_ref):
    l = z_ref.shape[0]
    q = SSD_CHUNK
    assert HEADS_PER_GROUP == 2 and GROUP_W == LANES and STATE_ROWS == LANES
    cw = cw_ref[...]
    cb = cb_ref[...]
    dtb = dtb_ref[...]
    dsk = dsk_ref[...]
    aneg = aneg_ref[...] * math.log2(math.e)

    def prep(i, carry):
        r0 = pl.multiple_of(i * SSD_ROWS, SSD_ROWS)
        rows = pl.ds(r0, SSD_ROWS)
        xbc = _silu(_conv_rows(xin_ref, 0, SSD_XBC, r0, SSD_ROWS, cw, cb))
        xbc_ref[rows, :] = xbc
        dt = _softplus(dt_ref[rows, :] + dtb)
        dts_ref[rows, :] = dt
        y_ref[rows, :] = xbc[:, 0:SSD_WIDTH] * dsk
        a = dt * aneg
        hi = a.astype(BF16).astype(F32)
        r1 = a - hi
        mid = r1.astype(BF16).astype(F32)
        lo = r1 - mid
        a3_ref[rows, :] = (hi + pltpu.roll(mid, 2 * SSD_HEADS, 1) + pltpu.roll(lo, 4 * SSD_HEADS, 1)).astype(BF16)
        return carry

    lax.fori_loop(0, l // SSD_ROWS, prep, 0, unroll=3)

    ri = lax.broadcasted_iota(jnp.int32, (q, q), 0)
    ci = lax.broadcasted_iota(jnp.int32, (q, q), 1)
    lower = ri >= ci
    upper = ci >= ri
    tri_lo = lower.astype(F32).astype(BF16)
    tri_up = upper.astype(F32).astype(BF16)
    eye16 = (ri == ci).astype(F32).astype(BF16)
    eye8 = (lax.broadcasted_iota(jnp.int32, (SUBLANES, LANES), 0)
            == lax.broadcasted_iota(jnp.int32, (SUBLANES, LANES), 1)).astype(F32)
    lane = lax.broadcasted_iota(jnp.int32, (q, LANES), 1)
    left = lane < SSD_HEADDIM
    lane_row = lax.broadcasted_iota(jnp.int32, (1, LANES), 1)
    left_row = lane_row < SSD_HEADDIM
    sub8 = lax.broadcasted_iota(jnp.int32, (SUBLANES, q), 0)
    left_state = lax.broadcasted_iota(jnp.int32, (SSD_STATE, GROUP_W), 1) < SSD_HEADDIM
    nchunk = l // q
    nctx = CTX_LEN // q

    def local(cg, carry):
        chunks = [cg * SSD_GROUP_CHUNKS + k for k in range(SSD_GROUP_CHUNKS)]
        rows = [pl.ds(pl.multiple_of(c * q, q), q) for c in chunks]
        hi = lax.Precision.HIGHEST

        def fold(cs):
            return cs + pltpu.roll(cs, LANES - 2 * SSD_HEADS, 1) + pltpu.roll(cs, LANES - 4 * SSD_HEADS, 1)

        x16, bm_t, grams, cs_lo, cs_up = [], [], [], [], []
        for r in rows:
            x16.append(xbc_ref[r, 0:SSD_WIDTH].astype(BF16))
            bm16 = xbc_ref[r, SSD_WIDTH:SSD_WIDTH + LANES].astype(BF16)
            cm = xbc_ref[r, SSD_WIDTH + LANES:SSD_XBC]
            bm_t.append(lax.dot_general(eye16, bm16, NT_DIMS, preferred_element_type=F32))
            gr = []
            for g in range(SSD_GROUPS):
                cm_g = jnp.where((lane < SSD_STATE) == (g == 0), cm, 0.0).astype(BF16)
                gr.append(lax.dot_general(cm_g, bm16, NT_DIMS, preferred_element_type=F32))
            grams.append(gr)
            a3 = a3_ref[r, :]
            cs_lo.append(jnp.dot(tri_lo, a3, preferred_element_type=F32))
            cs_up.append(jnp.dot(tri_up, a3, preferred_element_type=F32))

        c_col, c_row, dt_row = [], [], []
        for k, r in enumerate(rows):
            cc = jnp.where(lane < SSD_HEADS, fold(cs_lo[k]), fold(cs_up[k]))
            c_col.append(cc)
            c_row.append(lax.dot_general(eye8, cc, NT_DIMS, precision=hi, preferred_element_type=F32))
            dt_row.append(lax.dot_general(eye8, dts_ref[r, :], NT_DIMS, precision=hi,
                                          preferred_element_type=F32))

        zblock = jnp.zeros((SSD_STATE, GROUP_W), F32)
        for k, (c, r) in enumerate(zip(chunks, rows)):
            tot_col = jnp.where(sub8 < SSD_HEADS, c_row[k][:, q - 1:q], c_row[k][:, 0:1])
            dtw_row = dt_row[k] * jnp.exp2(tot_col - c_row[k])
            tot_row = jnp.where(lane_row < SSD_HEADS, c_col[k][q - 1:q, :], c_col[k][0:1, :])
            e_tot = jnp.exp2(tot_row)
            for d in range(2):
                mask = lower if d == 0 else upper
                ds = []
                etot = []
                for g in range(SSD_GROUPS):
                    xg16 = x16[k][:, g * GROUP_W:(g + 1) * GROUP_W]
                    la = _ssd_head_lane(d, g, 0)
                    lb = _ssd_head_lane(d, g, 1)
                    yd = []
                    dsh = []
                    ecc = []
                    for li in (la, lb):
                        ccb = jnp.broadcast_to(c_col[k][:, li:li + 1], (q, q))
                        decay = jnp.exp2(jnp.where(mask, ccb - c_row[k][li:li + 1, :], -1e30))
                        m = (grams[k][g] * decay * dt_row[k][li:li + 1, :]).astype(BF16)
                        yd.append(jnp.dot(m, xg16, preferred_element_type=F32))
                        ecc.append(jnp.exp2(ccb))
                        b_t = (bm_t[k][g * SSD_STATE:(g + 1) * SSD_STATE, :] * dtw_row[li:li + 1, :]).astype(BF16)
                        dsh.append(jnp.dot(b_t, xg16, preferred_element_type=F32))
                    cols = slice(g * GROUP_W, (g + 1) * GROUP_W)
                    y_ref[r, cols] += jnp.where(left, yd[0], yd[1])
                    ecc_ref[d, r, cols] = jnp.where(left, ecc[0], ecc[1])
                    dsg = jnp.where(left_state, dsh[0], dsh[1])
                    ds.append(jnp.concatenate([dsg, zblock] if g == 0 else [zblock, dsg], axis=1))
                    etot.append(jnp.where(left_row, e_tot[:, la:la + 1], e_tot[:, lb:lb + 1]))
                ds_ref[d, pl.ds(pl.multiple_of(c * STATE_ROWS, STATE_ROWS), STATE_ROWS), :] = (
                    jnp.concatenate(ds, axis=0))
                etot_ref[d, pl.ds(pl.multiple_of(c * SUBLANES, SUBLANES), SUBLANES), :] = (
                    jnp.broadcast_to(jnp.concatenate(etot, axis=1), (SUBLANES, SSD_WIDTH)))
        return carry

    assert nchunk % SSD_GROUP_CHUNKS == 0
    lax.fori_loop(0, nchunk // SSD_GROUP_CHUNKS, local, 0)

    def carry_state(j, state):
        sf, sb = state
        jb = jnp.where(j < nctx, nctx - 1 - j, nchunk + nctx - 1 - j)
        out = []
        for d, (c, s) in enumerate(((j, sf), (jb, sb))):
            srows = pl.ds(pl.multiple_of(c * STATE_ROWS, STATE_ROWS), STATE_ROWS)
            sent_ref[d, srows, :] = s.astype(BF16)
            e = etot_ref[d, pl.ds(pl.multiple_of(c * SUBLANES, SUBLANES), 1), :]
            out.append(e * s + ds_ref[d, srows, :])
        return tuple(out)

    zero = jnp.zeros((STATE_ROWS, SSD_WIDTH), F32)
    lax.fori_loop(0, nchunk, carry_state, (zero, zero))

    ng = ng_ref[...]

    def finish(c, carry):
        r0 = pl.multiple_of(c * q, q)
        rows = pl.ds(r0, q)
        srows = pl.ds(pl.multiple_of(c * STATE_ROWS, STATE_ROWS), STATE_ROWS)
        cm16 = xbc_ref[rows, SSD_WIDTH + LANES:SSD_XBC].astype(BF16)
        y = y_ref[rows, :]
        for d in range(2):
            y = y + ecc_ref[d, rows, :] * jnp.dot(cm16, sent_ref[d, srows, :], preferred_element_type=F32)
        v = y * _silu(z_ref[rows, :])
        parts = []
        for g in range(SSD_GROUPS):
            vg = v[:, g * GROUP_W:(g + 1) * GROUP_W]
            parts.append(vg * lax.rsqrt(jnp.mean(vg * vg, axis=-1, keepdims=True) + NORM_EPS))
        o_ref[rows, :] = (jnp.concatenate(parts, axis=1) * ng).astype(BF16)
        return carry

    lax.fori_loop(0, nchunk, finish, 0, unroll=3)


def _ssd(sz, sxbc, sdt, conv_w, conv_b, dt_bias, aneg, dskip, norm_g):
    b, l, _ = sz.shape
    nchunk = l // SSD_CHUNK
    seq = lambda w: pl.BlockSpec((None, l, w), lambda i: (i, 0, 0))
    scratch = [
        pltpu.VMEM((l, SSD_XBC), F32),
        pltpu.VMEM((l, LANES), F32),
        pltpu.VMEM((l, LANES), BF16),
        pltpu.VMEM((l, SSD_WIDTH), F32),
        pltpu.VMEM((2, l, SSD_WIDTH), F32),
        pltpu.VMEM((2, nchunk * STATE_ROWS, SSD_WIDTH), F32),
        pltpu.VMEM((2, nchunk * STATE_ROWS, SSD_WIDTH), BF16),
        pltpu.VMEM((2, nchunk * SUBLANES, SSD_WIDTH), F32),
    ]
    return pl.pallas_call(
        _ssd_kernel,
        grid=(b,),
        in_specs=[
            seq(SSD_WIDTH), seq(SSD_XBC), seq(LANES),
            _const_spec((4, SSD_XBC)),
            _const_spec((1, SSD_XBC)),
            _const_spec((1, LANES)),
            _const_spec((1, LANES)),
            _const_spec((1, SSD_WIDTH)),
            _const_spec((1, SSD_WIDTH)),
        ],
        out_specs=seq(SSD_WIDTH),
        out_shape=jax.ShapeDtypeStruct((b, l, SSD_WIDTH), BF16),
        scratch_shapes=scratch,
        compiler_params=_params("arbitrary"),
    )(sz, sxbc, sdt, conv_w, conv_b, dt_bias, aneg, dskip, norm_g)


def _attn_probs(q, k_ref, p_ref, nkeys):
    l = k_ref.shape[0]
    for c in range(2):
        qc = q[:, c * DA_HEAD_DIM:(c + 1) * DA_HEAD_DIM]
        kc = k_ref[0:nkeys, c * DA_HEAD_DIM:(c + 1) * DA_HEAD_DIM]
        s = lax.dot_general(qc, kc, (((1,), (1,)), ((), ())), preferred_element_type=F32)
        p_ref[:, c * l:c * l + nkeys] = jnp.exp2(s - jnp.max(s, axis=-1, keepdims=True)).astype(BF16)


def _attn_values(p_ref, vaug_ref, nkeys, lam, g, lam_init):
    l = vaug_ref.shape[0]
    outs = []
    for c in range(2):
        ov = jnp.dot(p_ref[:, c * l:c * l + nkeys], vaug_ref[0:nkeys, :], preferred_element_type=F32)
        outs.append(ov[:, 0:DA_V_DIM] / ov[:, DA_V_DIM:DA_V_DIM + 1])
    o = outs[0] - lam * outs[1]
    return (_rms(o, g) * (1.0 - lam_init)).astype(BF16)


def _attn_kernel(q_ref, k_ref, v_ref, lam_ref, g_ref, o_ref, vaug_ref, pa_ref, pb_ref, *, lam_init):
    l = q_ref.shape[0]
    nt = l // TOK_TILE
    assert CTX_LEN == TOK_TILE and nt % 2 == 1 and nt >= 3
    lv = lam_ref[...]
    lam = (jnp.exp(jnp.sum(lv[0:1] * lv[1:2], axis=-1, keepdims=True))
           - jnp.exp(jnp.sum(lv[2:3] * lv[3:4], axis=-1, keepdims=True)) + lam_init)
    g = g_ref[...]
    vaug_ref[:, 0:DA_V_DIM] = v_ref[...]
    vaug_ref[:, DA_V_DIM:2 * DA_V_DIM] = jnp.ones((l, DA_V_DIM), BF16)

    def rows(t):
        return pl.ds(pl.multiple_of(t * TOK_TILE, TOK_TILE), TOK_TILE)

    def probs(t, p_ref, nkeys=l):
        _attn_probs(q_ref[rows(t), :], k_ref, p_ref, nkeys)

    def values(t, p_ref, nkeys=l):
        o_ref[rows(t), :] = _attn_values(p_ref, vaug_ref, nkeys, lam, g, lam_init)

    probs(0, pa_ref, CTX_LEN)
    probs(1, pb_ref)
    values(0, pa_ref, CTX_LEN)

    for t in range(2, nt - 1, 2):
        probs(t, pa_ref)
        values(t - 1, pb_ref)
        probs(t + 1, pb_ref)
        values(t, pa_ref)
    probs(nt - 1, pa_ref)
    values(nt - 2, pb_ref)
    values(nt - 1, pa_ref)


def _attention(q, k, v, lam_vec, subln_g, layer_idx):
    b, l, _ = q.shape
    lam_init = 0.8 - 0.6 * math.exp(-0.3 * layer_idx)
    seq = pl.BlockSpec((None, l, DA_V_DIM), lambda i, h: (i, 0, h))
    return pl.pallas_call(
        functools.partial(_attn_kernel, lam_init=lam_init),
        grid=(b, DA_HEADS),
        in_specs=[seq, seq, seq, _const_spec((4, DA_HEAD_DIM)), _const_spec((1, DA_V_DIM))],
        out_specs=seq,
        out_shape=jax.ShapeDtypeStruct((b, l, DA_WIDTH), BF16),
        scratch_shapes=[pltpu.VMEM((l, 2 * DA_V_DIM), BF16)] + [pltpu.VMEM((TOK_TILE, 2 * l), BF16)] * 2,
        compiler_params=_params("arbitrary", "arbitrary"),
    )(q, k, v, lam_vec, subln_g)


def _out_ffn_kernel(*refs, final):
    rg_ref, ssd_ref, da_ref, mod_ref, g2_ref, wo_ref, wg_ref, wu_ref, wd_ref, gf_ref, o_ref = refs[-11:]
    h_in = _load_stream(refs[:-11])
    g2 = g2_ref[...]
    nb = FFN_BATCH
    mods = [mod_ref[j] for j in range(nb)]
    mixes = []
    for j in range(nb):
        mix = jnp.dot(rg_ref[j], wo_ref[0:RG_WIDTH, :], preferred_element_type=F32)
        mix += jnp.dot(ssd_ref[j], wo_ref[RG_WIDTH:RG_WIDTH + SSD_WIDTH, :], preferred_element_type=F32)
        mix += jnp.dot(da_ref[j], wo_ref[RG_WIDTH + SSD_WIDTH:, :], preferred_element_type=F32)
        mixes.append(mix)
    hs = [h_in[j] + mods[j][2:3] * mixes[j] for j in range(nb)]
    ys = [(_rms(hs[j], g2) * (1.0 + mods[j][4:5]) + mods[j][3:4]).astype(BF16) for j in range(nb)]
    gates, ups = [], []
    for y in ys:
        gates.append(jnp.dot(y, wg_ref[...], preferred_element_type=F32))
        ups.append(jnp.dot(y, wu_ref[...], preferred_element_type=F32))
    acts = [(_silu(gates[j]) * ups[j]).astype(BF16) for j in range(nb)]
    downs = [jnp.dot(a, wd_ref[...], preferred_element_type=F32) for a in acts]
    for j in range(nb):
        h = hs[j] + mods[j][5:6] * downs[j]
        o_ref[j] = _rms(h, gf_ref[...]) if final else h


def _out_ffn(hs, rg, ssd, da, mod, norm2_g, w_out, w_gate, w_up, w_down, final_g, layer, final):
    b, l, _ = rg.shape
    nb = FFN_BATCH
    assert b % nb == 0
    t0 = CTX_LEN // TOK_TILE if final else 0
    nt = l // TOK_TILE - t0
    tok = lambda w: pl.BlockSpec((nb, TOK_TILE, w), lambda t, i: (i, t + t0, 0))
    out_rows = nt * TOK_TILE
    kwargs = {"input_output_aliases": {0: 0}} if (len(hs) == 1 and not final) else {}
    return pl.pallas_call(
        functools.partial(_out_ffn_kernel, final=final),
        grid=(nt, b // nb),
        in_specs=_stream_specs(hs, b, t0, nb) + [
            tok(RG_WIDTH), tok(SSD_WIDTH), tok(DA_WIDTH),
            pl.BlockSpec((nb, 6, D_MODEL), lambda t, i: (jnp.where(t + t0 == 0, b // nb, i), 0, 0)),
            _const_spec((1, D_MODEL)),
            _const_spec((D_MODEL, D_MODEL), layer),
            _const_spec((D_MODEL, D_FF), layer),
            _const_spec((D_MODEL, D_FF), layer),
            _const_spec((D_FF, D_MODEL), layer),
            _const_spec((1, D_MODEL)),
        ],
        out_specs=pl.BlockSpec((nb, TOK_TILE, D_MODEL), lambda t, i: (i, t, 0)),
        out_shape=jax.ShapeDtypeStruct((b, out_rows, D_MODEL), F32),
        compiler_params=_params("arbitrary", "arbitrary"),
        **kwargs,
    )(*hs, rg, ssd, da, mod, norm2_g, w_out, w_gate, w_up, w_down, final_g)


def _rope_tables(n_lat):
    half = DA_HEAD_DIM // 2
    inv_freq = jnp.power(ROPE_BASE, -jnp.arange(0, half, 2, dtype=F32) / half)
    t = jnp.arange(n_lat, dtype=jnp.int32)
    ang_r = (t // GRID_W).astype(F32)[:, None] * inv_freq
    ang_c = (t % GRID_W).astype(F32)[:, None] * inv_freq
    cos = jnp.concatenate([jnp.cos(ang_r), jnp.cos(ang_r), jnp.cos(ang_c), jnp.cos(ang_c)], axis=1)
    sin = jnp.concatenate([-jnp.sin(ang_r), jnp.sin(ang_r), -jnp.sin(ang_c), jnp.sin(ang_c)], axis=1)
    reps = DA_WIDTH // DA_HEAD_DIM
    cos = jnp.concatenate([jnp.ones((CTX_LEN, DA_HEAD_DIM), F32), cos], axis=0)
    sin = jnp.concatenate([jnp.zeros((CTX_LEN, DA_HEAD_DIM), F32), sin], axis=0)
    cos = jnp.tile(cos, (1, reps))
    sin = jnp.tile(sin, (1, reps))
    scale = DA_HEAD_DIM ** -0.5 * math.log2(math.e)
    return cos * scale, sin * scale, cos, sin


def _permute_w_in(w):
    offs = [0]
    for s in IN_SIZES:
        offs.append(offs[-1] + s)
    parts = [w[:, offs[i]:offs[i + 1]] for i in range(len(IN_SIZES))]
    pad = jnp.zeros((w.shape[0], LANES - IN_SIZES[4]), w.dtype)
    return jnp.concatenate(parts[0:4] + parts[5:8] + [parts[4], pad], axis=1).astype(BF16)


def _block_diag(w):
    eye = jnp.eye(RG_HEADS, dtype=w.dtype)
    return jnp.einsum("hij,hg->higj", w, eye).reshape(RG_WIDTH, RG_WIDTH)


def _lane_pad(v):
    return jnp.concatenate([v, jnp.zeros((LANES - v.shape[0],), v.dtype)])[None, :]


def kernel(x, c, ctx, c_ctx, w_mod, b_mod, norm1_g, w_in, rg_conv_w, rg_conv_b, rg_w_a, rg_b_a, rg_w_x, rg_b_x, rg_lambda, ssd_conv_w, ssd_conv_b, ssd_dt_bias, ssd_a_log, ssd_d, ssd_norm_g, da_lambda, da_subln_g, w_out, norm2_g, w_gate, w_up, w_down, final_norm_g):
    b, n_lat, _ = x.shape
    assert ctx.shape[1] == CTX_LEN and n_lat % TOK_TILE == 0
    n_ctx = max(INPROJ_BATCH, FFN_BATCH)
    rows = -(-(b + n_ctx) // SUBLANES) * SUBLANES
    c_all = jnp.concatenate([c] + [c_ctx[None, :]] * n_ctx
                            + [jnp.zeros((rows - b - n_ctx, D_MODEL), F32)], axis=0)
    mods = _modulation(c_all, w_mod, b_mod)
    tables = _rope_tables(n_lat)
    hs = (ctx, x)
    w_out16, w_gate16, w_up16, w_down16 = (w.astype(BF16) for w in (w_out, w_gate, w_up, w_down))
    out = None
    for l in range(DEPTH):
        final = l == DEPTH - 1
        rg, sz, sxbc, sdt, q, k, v = _in_proj(hs, mods[l], norm1_g[l][None, :], _permute_w_in(w_in[l]), tables)
        w_gates = (0.5 * jnp.concatenate(
            [_block_diag(w[l, d]) for d in range(2) for w in (rg_w_a, rg_w_x)], axis=1)).astype(BF16)
        b_gates = 0.5 * jnp.concatenate(
            [bb[l, d] for d in range(2) for bb in (rg_b_a, rg_b_x)])[None, :]
        y_rg = _rglru(rg, rg_conv_w[l], rg_conv_b[l][None, :], w_gates, b_gates, rg_lambda[l])
        y_ssd = _ssd(sz, sxbc, sdt, ssd_conv_w[l], ssd_conv_b[l][None, :],
                     _lane_pad(ssd_dt_bias[l].reshape(-1)),
                     _lane_pad(-jnp.exp(ssd_a_log[l].reshape(-1))),
                     jnp.repeat(ssd_d[l], SSD_HEADDIM)[None, :], ssd_norm_g[l][None, :])
        y_da = _attention(q, k, v, da_lambda[l], da_subln_g[l][None, :], l)
        res = _out_ffn(hs, y_rg, y_ssd, y_da, mods[l], norm2_g[l][None, :], w_out16, w_gate16, w_up16,
                       w_down16, final_norm_g[None, :], l, final)
        if final:
            out = res
        else:
            hs = (res,)
    return out
```

```python
import functools
import math

import jax
import jax.numpy as jnp
from jax import lax
from jax.experimental import pallas as pl
from jax.experimental.pallas import tpu as pltpu

F32 = jnp.float32
BF16 = jnp.bfloat16

D_MODEL = 1024
DEPTH = 2
CTX_LEN = 256
GRID_W = 64
NORM_EPS = 1e-6

RG_WIDTH = 256
RG_HEADS = 4
RG_BLOCK = 64
RG_C = 8.0

SSD_WIDTH = 256
SSD_HEADDIM = 64
SSD_HEADS = 4
SSD_GROUPS = 2
SSD_STATE = 64
SSD_CHUNK = 128
SSD_XBC = 512

DA_WIDTH = 512
DA_HEADS = 4
DA_HEAD_DIM = 64
DA_V_DIM = 128
ROPE_BASE = 10000.0

D_FF = 2816
IN_SIZES = (256, 256, 256, 512, 8, 512, 512, 512)
D_IN = sum(IN_SIZES)

LANES = 128
SUBLANES = 8
MXU_COLS = 256
TOK_TILE = 256
INPROJ_BATCH = 4
FFN_BATCH = 2
CONV_PAD = SUBLANES
VMEM_LIMIT = 56 * 1024 * 1024

QKV_COLS = 3 * DA_WIDTH
RGP_COLS = 2 * RG_WIDTH
S_XBC = 0
S_DT = 512
S_Z = 640
SSDP_COLS = 896
PROJ_ROWS = 256
PROJ_AHEAD = 2


def _silu(x):
    hx = 0.5 * x
    return hx * jnp.tanh(hx) + hx


def _softplus(x):
    return jnp.maximum(x, 0.0) + jnp.log(1.0 + jnp.exp(-jnp.abs(x)))


def _rms(x, g):
    ms = jnp.mean(x * x, axis=-1, keepdims=True)
    return x * lax.rsqrt(ms + NORM_EPS) * g


def _const_spec(shape, layer=None):
    nd = len(shape)
    if layer is None:
        return pl.BlockSpec(shape, lambda *_: (0,) * nd, pipeline_mode=pl.Buffered(1))
    return pl.BlockSpec((None,) + tuple(shape), lambda *_: (layer,) + (0,) * nd, pipeline_mode=pl.Buffered(1))


def _params(*sem):
    return pltpu.CompilerParams(dimension_semantics=sem, vmem_limit_bytes=VMEM_LIMIT)


def _mod_kernel(c_ref, w_ref, b_ref, o_ref):
    c = c_ref[...]
    a = _silu(c).astype(BF16)
    o_ref[...] = jnp.dot(a, w_ref[...].astype(BF16), preferred_element_type=F32) + b_ref[...]


def _modulation(c_all, w_mod, b_mod):
    r = c_all.shape[0]
    out = pl.pallas_call(
        _mod_kernel,
        grid=(DEPTH, 6),
        in_specs=[
            pl.BlockSpec((r, D_MODEL), lambda l, j: (0, 0)),
            pl.BlockSpec((None, D_MODEL, D_MODEL), lambda l, j: (l, 0, j)),
            pl.BlockSpec((None, 1, D_MODEL), lambda l, j: (l, 0, j)),
        ],
        out_specs=pl.BlockSpec((None, r, D_MODEL), lambda l, j: (l, 0, j)),
        out_shape=jax.ShapeDtypeStruct((DEPTH, r, 6 * D_MODEL), F32),
        compiler_params=_params("arbitrary", "arbitrary"),
    )(c_all, w_mod, b_mod.reshape(DEPTH, 1, 6 * D_MODEL))
    return out.reshape(DEPTH, r, 6, D_MODEL)


def _rope(x, cos, sin, first_half):
    width = x.shape[-1]
    partner = jnp.where(first_half, pltpu.roll(x, width - 16, 1), pltpu.roll(x, 16, 1))
    return x * cos + partner * sin


def _load_stream(h_refs):
    if len(h_refs) == 1:
        return h_refs[0][...]
    return jnp.where(pl.program_id(0) == 0, h_refs[0][...], h_refs[1][...])


def _stream_specs(hs, b, t0=0, nb=None):
    lead = nb
    if len(hs) == 1:
        return [pl.BlockSpec((lead, TOK_TILE, D_MODEL), lambda t, i: (i, t + t0, 0))]
    assert t0 == 0
    return [pl.BlockSpec((lead, TOK_TILE, D_MODEL), lambda t, i: (jnp.where(t == 0, i, 0), 0, 0)),
            pl.BlockSpec((lead, TOK_TILE, D_MODEL), lambda t, i: (i, jnp.maximum(t - 1, 0), 0))]


def _inproj_kernel(*refs):
    mod_ref, g_ref, w_ref, cq_ref, sq_ref, ck_ref, sk_ref, y_ref, q_ref, k_ref, v_ref = refs[-11:]
    h = _load_stream(refs[:-11])
    g = g_ref[...]
    ys = []
    for j in range(INPROJ_BATCH):
        mod = mod_ref[j]
        ys.append((_rms(h[j], g) * (1.0 + mod[1:2]) + mod[0:1]).astype(BF16))
    us = [jnp.dot(y, w_ref[...], preferred_element_type=F32) for y in ys]
    lane = lax.broadcasted_iota(jnp.int32, (TOK_TILE, DA_WIDTH), 1)
    first_half = (lane % 32) < 16
    for j, u in enumerate(us):
        y_ref[j] = ys[j]
        q_ref[j] = _rope(u[:, 0:DA_WIDTH], cq_ref[...], sq_ref[...], first_half).astype(BF16)
        k_ref[j] = _rope(u[:, DA_WIDTH:2 * DA_WIDTH], ck_ref[...], sk_ref[...], first_half).astype(BF16)
        v_ref[j] = u[:, 2 * DA_WIDTH:QKV_COLS].astype(BF16)


def _in_proj(hs, mod, norm_g, w_qkv, layer, tables):
    b = hs[0].shape[0]
    l = sum(h.shape[1] for h in hs)
    nt = l // TOK_TILE
    nb = INPROJ_BATCH
    assert b % nb == 0
    tok = lambda w: pl.BlockSpec((nb, TOK_TILE, w), lambda t, i: (i, t, 0))
    tab = pl.BlockSpec((TOK_TILE, DA_WIDTH), lambda t, i: (t, 0))
    widths = (D_MODEL, DA_WIDTH, DA_WIDTH, DA_WIDTH)
    return pl.pallas_call(
        _inproj_kernel,
        grid=(nt, b // nb),
        in_specs=_stream_specs(hs, b, nb=nb) + [
            pl.BlockSpec((nb, 6, D_MODEL), lambda t, i: (jnp.where(t == 0, b // nb, i), 0, 0)),
            _const_spec((1, D_MODEL)),
            _const_spec((D_MODEL, QKV_COLS), layer),
            tab, tab, tab, tab,
        ],
        out_specs=[tok(w) for w in widths],
        out_shape=[jax.ShapeDtypeStruct((b, l, w), BF16) for w in widths],
        compiler_params=_params("arbitrary", "arbitrary"),
    )(*hs, mod, norm_g, w_qkv, *tables)


def _project_rows(y_ref, w_ref, u_ref, i):
    r0 = i * PROJ_ROWS
    rows = pl.ds(r0 if isinstance(r0, int) else pl.multiple_of(r0, PROJ_ROWS), PROJ_ROWS)
    y = y_ref[rows, :]
    half = ((u_ref.shape[1] // MXU_COLS + 1) // 2) * MXU_COLS
    for cols in (slice(0, half), slice(half, u_ref.shape[1])):
        u_ref[rows, cols] = jnp.dot(y, w_ref[:, cols], preferred_element_type=F32)


def _conv_rows(x_ref, col0, width, r0, rows, w, bias):
    l = x_ref.shape[0]
    assert CTX_LEN % rows == 0
    span = rows + 2 * CONV_PAD
    lo = pl.multiple_of(jnp.maximum(r0 - CONV_PAD, 0), CONV_PAD)
    hi = pl.multiple_of(jnp.minimum(r0 + rows, l - CONV_PAD), CONV_PAD)
    cols = slice(col0, col0 + width)
    seg_start = jnp.logical_or(r0 == 0, r0 == CTX_LEN)
    seg_end = jnp.logical_or(r0 + rows == CTX_LEN, r0 + rows == l)
    xa = jnp.concatenate([jnp.where(seg_start, 0.0, x_ref[pl.ds(lo, CONV_PAD), cols]),
                          x_ref[pl.ds(r0, rows), cols],
                          jnp.where(seg_end, 0.0, x_ref[pl.ds(hi, CONV_PAD), cols])], axis=0)
    acc = None
    for tap in range(4):
        off = tap - 2
        sh = xa if off == 0 else pltpu.roll(xa, (-off) % span, 0)
        term = sh[CONV_PAD:CONV_PAD + rows] * w[tap:tap + 1]
        acc = term if acc is None else acc + term
    return acc + bias


RG_ROWS = 256


def _scan8(a, bx, reverse):
    row = lax.broadcasted_iota(jnp.int32, a.shape, 0)
    for s in (1, 2, 4):
        shift = (SUBLANES - s) if reverse else s
        a_sh = pltpu.roll(a, shift, 0)
        b_sh = pltpu.roll(bx, shift, 0)
        ok = (row < SUBLANES - s) if reverse else (row >= s)
        bx = jnp.where(ok, a * b_sh + bx, bx)
        a = jnp.where(ok, a * a_sh, a)
    return a, bx


def _rglru_kernel(y_ref, wp_ref, ws_ref, cw_ref, cb_ref, wg_ref, bg_ref, lam_ref, o_ref, us_ref,
                  rg_ref, af_ref, bf_ref, ab_ref, bb_ref):
    l = y_ref.shape[0]
    nproj = l // PROJ_ROWS
    assert RG_ROWS == PROJ_ROWS
    for i in range(PROJ_AHEAD):
        _project_rows(y_ref, wp_ref, rg_ref, i)
    cw = cw_ref[...]
    cb = cb_ref[...]
    bg = bg_ref[...]
    coef = (-0.5 * RG_C) * _softplus(-lam_ref[...])

    def coeffs(i, carry):
        r0 = i * RG_ROWS
        xc = _conv_rows(rg_ref, 0, RG_WIDTH, r0, RG_ROWS, cw, cb)
        t = jnp.tanh(jnp.dot(xc.astype(BF16), wg_ref[...], preferred_element_type=F32) + bg)
        hx = 0.5 * xc
        for d, (a_ref, b_ref) in enumerate(((af_ref, bf_ref), (ab_ref, bb_ref))):
            t_a = t[:, (2 * d) * RG_WIDTH:(2 * d + 1) * RG_WIDTH]
            t_x = t[:, (2 * d + 1) * RG_WIDTH:(2 * d + 2) * RG_WIDTH]
            log_a = coef[d:d + 1] * t_a + coef[d:d + 1]
            a = jnp.exp(log_a)
            one_m_a2 = jnp.tanh(log_a) * (-1.0 - a * a)
            a_ref[pl.ds(r0, RG_ROWS), :] = a
            b_ref[pl.ds(r0, RG_ROWS), :] = jnp.sqrt(one_m_a2) * (hx * t_x + hx)
        if i + PROJ_AHEAD < nproj:
            _project_rows(y_ref, wp_ref, rg_ref, i + PROJ_AHEAD)
        return carry

    for i in range(nproj):
        coeffs(i, 0)

    nblk = l // SUBLANES
    nctx = CTX_LEN // SUBLANES

    def scan_block(j, carry):
        hf, hb = carry
        rf = pl.multiple_of(j * SUBLANES, SUBLANES)
        jb = jnp.where(j < nctx, nctx - 1 - j, nblk + nctx - 1 - j)
        rb = pl.multiple_of(jb * SUBLANES, SUBLANES)
        a, bx = _scan8(af_ref[pl.ds(rf, SUBLANES), :], bf_ref[pl.ds(rf, SUBLANES), :], False)
        h = a * hf + bx
        af_ref[pl.ds(rf, SUBLANES), :] = h
        hf = h[SUBLANES - 1:SUBLANES]
        a, bx = _scan8(ab_ref[pl.ds(rb, SUBLANES), :], bb_ref[pl.ds(rb, SUBLANES), :], True)
        h = a * hb + bx
        ab_ref[pl.ds(rb, SUBLANES), :] = h
        hb = h[0:1]
        return hf, hb

    per_step = PROJ_ROWS // SUBLANES

    def scan(i, carry):
        _project_rows(y_ref, ws_ref, us_ref, i)
        for jj in range(per_step):
            carry = scan_block(i * per_step + jj, carry)
        return carry

    zero = jnp.zeros((1, RG_WIDTH), F32)
    lax.fori_loop(0, nproj, scan, (zero, zero))

    def finish(i, carry):
        r0 = pl.multiple_of(i * RG_ROWS, RG_ROWS)
        hsum = af_ref[pl.ds(r0, RG_ROWS), :] + ab_ref[pl.ds(r0, RG_ROWS), :]
        g = rg_ref[pl.ds(r0, RG_ROWS), RG_WIDTH:2 * RG_WIDTH]
        o_ref[pl.ds(r0, RG_ROWS), :] = (hsum * jax.nn.gelu(g, approximate=True)).astype(BF16)
        return carry

    lax.fori_loop(0, l // RG_ROWS, finish, 0)


def _rglru(y, w_proj, w_ssd, layer, conv_w, conv_b, w_gates, b_gates, lam):
    b, l, _ = y.shape
    seq = lambda w: pl.BlockSpec((None, l, w), lambda i: (i, 0, 0))
    scratch = [pltpu.VMEM((l, RGP_COLS), F32)] + [pltpu.VMEM((l, RG_WIDTH), F32)] * 4
    return pl.pallas_call(
        _rglru_kernel,
        grid=(b,),
        in_specs=[
            seq(D_MODEL),
            _const_spec((D_MODEL, RGP_COLS), layer),
            _const_spec((D_MODEL, SSDP_COLS), layer),
            _const_spec((4, RG_WIDTH)),
            _const_spec((1, RG_WIDTH)),
            _const_spec((RG_WIDTH, 4 * RG_WIDTH)),
            _const_spec((1, 4 * RG_WIDTH)),
            _const_spec((2, RG_WIDTH)),
        ],
        out_specs=[seq(RG_WIDTH), seq(SSDP_COLS)],
        out_shape=[jax.ShapeDtypeStruct((b, l, RG_WIDTH), BF16), jax.ShapeDtypeStruct((b, l, SSDP_COLS), F32)],
        scratch_shapes=scratch,
        compiler_params=_params("arbitrary"),
    )(y, w_proj, w_ssd, conv_w, conv_b, w_gates, b_gates, lam)


SSD_ROWS = 64
GROUP_W = SSD_WIDTH // SSD_GROUPS
HEADS_PER_GROUP = SSD_HEADS // SSD_GROUPS
STATE_ROWS = SSD_GROUPS * SSD_STATE
NT_DIMS = (((1,), (1,)), ((), ()))
SSD_GROUP_CHUNKS = 9


def _ssd_head_lane(direction, group, j):
    return direction * SSD_HEADS + group * HEADS_PER_GROUP + j


def _ssd_kernel(u_ref, cw_ref, cb_ref, dtb_ref, aneg_ref, dsk_ref, ng_ref, o_ref,
                xbc_ref, dts_ref, a3_ref, y_ref, ecc_ref, ds_ref, sent_ref, etot_ref):
    l = u_ref.shape[0]
    q = SSD_CHUNK
    assert HEADS_PER_GROUP == 2 and GROUP_W == LANES and STATE_ROWS == LANES
    cw = cw_ref[...]
    cb = cb_ref[...]
    dtb = dtb_ref[...]
    dsk = dsk_ref[...]
    aneg = aneg_ref[...] * math.log2(math.e)

    def prep(i, carry):
        r0 = pl.multiple_of(i * SSD_ROWS, SSD_ROWS)
        rows = pl.ds(r0, SSD_ROWS)
        xbc = _silu(_conv_rows(u_ref, S_XBC, SSD_XBC, r0, SSD_ROWS, cw, cb))
        xbc_ref[rows, :] = xbc
        dt = _softplus(u_ref[rows, S_DT:S_Z] + dtb)
        dts_ref[rows, :] = dt
        y_ref[rows, :] = xbc[:, 0:SSD_WIDTH] * dsk
        a = dt * aneg
        hi = a.astype(BF16).astype(F32)
        r1 = a - hi
        mid = r1.astype(BF16).astype(F32)
        lo = r1 - mid
        a3_ref[rows, :] = (hi + pltpu.roll(mid, 2 * SSD_HEADS, 1) + pltpu.roll(lo, 4 * SSD_HEADS, 1)).astype(BF16)
        return carry

    lax.fori_loop(0, l // SSD_ROWS, prep, 0, unroll=3)

    ri = lax.broadcasted_iota(jnp.int32, (q, q), 0)
    ci = lax.broadcasted_iota(jnp.int32, (q, q), 1)
    lower = ri >= ci
    upper = ci >= ri
    tri_lo = lower.astype(F32).astype(BF16)
    tri_up = upper.astype(F32).astype(BF16)
    eye16 = (ri == ci).astype(F32).astype(BF16)
    eye8 = (lax.broadcasted_iota(jnp.int32, (SUBLANES, LANES), 0)
            == lax.broadcasted_iota(jnp.int32, (SUBLANES, LANES), 1)).astype(F32)
    lane = lax.broadcasted_iota(jnp.int32, (q, LANES), 1)
    left = lane < SSD_HEADDIM
    lane_row = lax.broadcasted_iota(jnp.int32, (1, LANES), 1)
    left_row = lane_row < SSD_HEADDIM
    sub8 = lax.broadcasted_iota(jnp.int32, (SUBLANES, q), 0)
    left_state = lax.broadcasted_iota(jnp.int32, (SSD_STATE, GROUP_W), 1) < SSD_HEADDIM
    nchunk = l // q
    nctx = CTX_LEN // q

    def local(cg, carry):
        chunks = [cg * SSD_GROUP_CHUNKS + k for k in range(SSD_GROUP_CHUNKS)]
        rows = [pl.ds(pl.multiple_of(c * q, q), q) for c in chunks]
        hi = lax.Precision.HIGHEST

        def fold(cs):
            return cs + pltpu.roll(cs, LANES - 2 * SSD_HEADS, 1) + pltpu.roll(cs, LANES - 4 * SSD_HEADS, 1)

        x16, bm_t, grams, cs_lo, cs_up = [], [], [], [], []
        for r in rows:
            x16.append(xbc_ref[r, 0:SSD_WIDTH].astype(BF16))
            bm16 = xbc_ref[r, SSD_WIDTH:SSD_WIDTH + LANES].astype(BF16)
            cm = xbc_ref[r, SSD_WIDTH + LANES:SSD_XBC]
            bm_t.append(lax.dot_general(eye16, bm16, NT_DIMS, preferred_element_type=F32))
            gr = []
            for g in range(SSD_GROUPS):
                cm_g = jnp.where((lane < SSD_STATE) == (g == 0), cm, 0.0).astype(BF16)
                gr.append(lax.dot_general(cm_g, bm16, NT_DIMS, preferred_element_type=F32))
            grams.append(gr)
            a3 = a3_ref[r, :]
            cs_lo.append(jnp.dot(tri_lo, a3, preferred_element_type=F32))
            cs_up.append(jnp.dot(tri_up, a3, preferred_element_type=F32))

        c_col, c_row, dt_row = [], [], []
        for k, r in enumerate(rows):
            cc = jnp.where(lane < SSD_HEADS, fold(cs_lo[k]), fold(cs_up[k]))
            c_col.append(cc)
            c_row.append(lax.dot_general(eye8, cc, NT_DIMS, precision=hi, preferred_element_type=F32))
            dt_row.append(lax.dot_general(eye8, dts_ref[r, :], NT_DIMS, precision=hi,
                                          preferred_element_type=F32))

        zblock = jnp.zeros((SSD_STATE, GROUP_W), F32)
        for k, (c, r) in enumerate(zip(chunks, rows)):
            tot_col = jnp.where(sub8 < SSD_HEADS, c_row[k][:, q - 1:q], c_row[k][:, 0:1])
            dtw_row = dt_row[k] * jnp.exp2(tot_col - c_row[k])
            tot_row = jnp.where(lane_row < SSD_HEADS, c_col[k][q - 1:q, :], c_col[k][0:1, :])
            e_tot = jnp.exp2(tot_row)
            for d in range(2):
                mask = lower if d == 0 else upper
                ds = []
                etot = []
                for g in range(SSD_GROUPS):
                    xg16 = x16[k][:, g * GROUP_W:(g + 1) * GROUP_W]
                    la = _ssd_head_lane(d, g, 0)
                    lb = _ssd_head_lane(d, g, 1)
                    yd = []
                    dsh = []
                    ecc = []
                    for li in (la, lb):
                        ccb = jnp.broadcast_to(c_col[k][:, li:li + 1], (q, q))
                        decay = jnp.exp2(jnp.where(mask, ccb - c_row[k][li:li + 1, :], -1e30))
                        m = (grams[k][g] * decay * dt_row[k][li:li + 1, :]).astype(BF16)
                        yd.append(jnp.dot(m, xg16, preferred_element_type=F32))
                        ecc.append(jnp.exp2(ccb))
                        b_t = (bm_t[k][g * SSD_STATE:(g + 1) * SSD_STATE, :] * dtw_row[li:li + 1, :]).astype(BF16)
                        dsh.append(jnp.dot(b_t, xg16, preferred_element_type=F32))
                    cols = slice(g * GROUP_W, (g + 1) * GROUP_W)
                    y_ref[r, cols] += jnp.where(left, yd[0], yd[1])
                    ecc_ref[d, r, cols] = jnp.where(left, ecc[0], ecc[1])
                    dsg = jnp.where(left_state, dsh[0], dsh[1])
                    ds.append(jnp.concatenate([dsg, zblock] if g == 0 else [zblock, dsg], axis=1))
                    etot.append(jnp.where(left_row, e_tot[:, la:la + 1], e_tot[:, lb:lb + 1]))
                ds_ref[d, pl.ds(pl.multiple_of(c * STATE_ROWS, STATE_ROWS), STATE_ROWS), :] = (
                    jnp.concatenate(ds, axis=0))
                etot_ref[d, pl.ds(pl.multiple_of(c * SUBLANES, SUBLANES), SUBLANES), :] = (
                    jnp.broadcast_to(jnp.concatenate(etot, axis=1), (SUBLANES, SSD_WIDTH)))
        return carry

    assert nchunk % SSD_GROUP_CHUNKS == 0
    lax.fori_loop(0, nchunk // SSD_GROUP_CHUNKS, local, 0)

    def carry_state(j, state):
        sf, sb = state
        jb = jnp.where(j < nctx, nctx - 1 - j, nchunk + nctx - 1 - j)
        out = []
        for d, (c, s) in enumerate(((j, sf), (jb, sb))):
            srows = pl.ds(pl.multiple_of(c * STATE_ROWS, STATE_ROWS), STATE_ROWS)
            sent_ref[d, srows, :] = s.astype(BF16)
            e = etot_ref[d, pl.ds(pl.multiple_of(c * SUBLANES, SUBLANES), 1), :]
            out.append(e * s + ds_ref[d, srows, :])
        return tuple(out)

    zero = jnp.zeros((STATE_ROWS, SSD_WIDTH), F32)
    lax.fori_loop(0, nchunk, carry_state, (zero, zero))

    ng = ng_ref[...]

    def finish(c, carry):
        r0 = pl.multiple_of(c * q, q)
        rows = pl.ds(r0, q)
        srows = pl.ds(pl.multiple_of(c * STATE_ROWS, STATE_ROWS), STATE_ROWS)
        cm16 = xbc_ref[rows, SSD_WIDTH + LANES:SSD_XBC].astype(BF16)
        y = y_ref[rows, :]
        for d in range(2):
            y = y + ecc_ref[d, rows, :] * jnp.dot(cm16, sent_ref[d, srows, :], preferred_element_type=F32)
        v = y * _silu(u_ref[rows, S_Z:SSDP_COLS])
        parts = []
        for g in range(SSD_GROUPS):
            vg = v[:, g * GROUP_W:(g + 1) * GROUP_W]
            parts.append(vg * lax.rsqrt(jnp.mean(vg * vg, axis=-1, keepdims=True) + NORM_EPS))
        o_ref[rows, :] = (jnp.concatenate(parts, axis=1) * ng).astype(BF16)
        return carry

    lax.fori_loop(0, nchunk, finish, 0, unroll=3)


def _ssd(u, conv_w, conv_b, dt_bias, aneg, dskip, norm_g):
    b, l, _ = u.shape
    nchunk = l // SSD_CHUNK
    seq = lambda w: pl.BlockSpec((None, l, w), lambda i: (i, 0, 0))
    scratch = [
        pltpu.VMEM((l, SSD_XBC), F32),
        pltpu.VMEM((l, LANES), F32),
        pltpu.VMEM((l, LANES), BF16),
        pltpu.VMEM((l, SSD_WIDTH), F32),
        pltpu.VMEM((2, l, SSD_WIDTH), F32),
        pltpu.VMEM((2, nchunk * STATE_ROWS, SSD_WIDTH), F32),
        pltpu.VMEM((2, nchunk * STATE_ROWS, SSD_WIDTH), BF16),
        pltpu.VMEM((2, nchunk * SUBLANES, SSD_WIDTH), F32),
    ]
    return pl.pallas_call(
        _ssd_kernel,
        grid=(b,),
        in_specs=[
            seq(SSDP_COLS),
            _const_spec((4, SSD_XBC)),
            _const_spec((1, SSD_XBC)),
            _const_spec((1, LANES)),
            _const_spec((1, LANES)),
            _const_spec((1, SSD_WIDTH)),
            _const_spec((1, SSD_WIDTH)),
        ],
        out_specs=seq(SSD_WIDTH),
        out_shape=jax.ShapeDtypeStruct((b, l, SSD_WIDTH), BF16),
        scratch_shapes=scratch,
        compiler_params=_params("arbitrary"),
    )(u, conv_w, conv_b, dt_bias, aneg, dskip, norm_g)


def _attn_probs(q, k_ref, p_ref, nkeys):
    l = k_ref.shape[0]
    for c in range(2):
        qc = q[:, c * DA_HEAD_DIM:(c + 1) * DA_HEAD_DIM]
        kc = k_ref[0:nkeys, c * DA_HEAD_DIM:(c + 1) * DA_HEAD_DIM]
        s = lax.dot_general(qc, kc, (((1,), (1,)), ((), ())), preferred_element_type=F32)
        p_ref[:, c * l:c * l + nkeys] = jnp.exp2(s - jnp.max(s, axis=-1, keepdims=True)).astype(BF16)


def _attn_values(p_ref, vaug_ref, nkeys, lam, g, lam_init):
    l = vaug_ref.shape[0]
    outs = []
    for c in range(2):
        ov = jnp.dot(p_ref[:, c * l:c * l + nkeys], vaug_ref[0:nkeys, :], preferred_element_type=F32)
        outs.append(ov[:, 0:DA_V_DIM] / ov[:, DA_V_DIM:DA_V_DIM + 1])
    o = outs[0] - lam * outs[1]
    return (_rms(o, g) * (1.0 - lam_init)).astype(BF16)


def _attn_kernel(q_ref, k_ref, v_ref, lam_ref, g_ref, o_ref, vaug_ref, pa_ref, pb_ref, *, lam_init):
    l = q_ref.shape[0]
    nt = l // TOK_TILE
    assert CTX_LEN == TOK_TILE and nt % 2 == 1 and nt >= 3
    lv = lam_ref[...]
    lam = (jnp.exp(jnp.sum(lv[0:1] * lv[1:2], axis=-1, keepdims=True))
           - jnp.exp(jnp.sum(lv[2:3] * lv[3:4], axis=-1, keepdims=True)) + lam_init)
    g = g_ref[...]
    vaug_ref[:, 0:DA_V_DIM] = v_ref[...]
    vaug_ref[:, DA_V_DIM:2 * DA_V_DIM] = jnp.ones((l, DA_V_DIM), BF16)

    def rows(t):
        return pl.ds(pl.multiple_of(t * TOK_TILE, TOK_TILE), TOK_TILE)

    def probs(t, p_ref, nkeys=l):
        _attn_probs(q_ref[rows(t), :], k_ref, p_ref, nkeys)

    def values(t, p_ref, nkeys=l):
        o_ref[rows(t), :] = _attn_values(p_ref, vaug_ref, nkeys, lam, g, lam_init)

    probs(0, pa_ref, CTX_LEN)
    probs(1, pb_ref)
    values(0, pa_ref, CTX_LEN)

    for t in range(2, nt - 1, 2):
        probs(t, pa_ref)
        values(t - 1, pb_ref)
        probs(t + 1, pb_ref)
        values(t, pa_ref)
    probs(nt - 1, pa_ref)
    values(nt - 2, pb_ref)
    values(nt - 1, pa_ref)


def _attention(q, k, v, lam_vec, subln_g, layer_idx):
    b, l, _ = q.shape
    lam_init = 0.8 - 0.6 * math.exp(-0.3 * layer_idx)
    seq = pl.BlockSpec((None, l, DA_V_DIM), lambda i, h: (i, 0, h))
    return pl.pallas_call(
        functools.partial(_attn_kernel, lam_init=lam_init),
        grid=(b, DA_HEADS),
        in_specs=[seq, seq, seq, _const_spec((4, DA_HEAD_DIM)), _const_spec((1, DA_V_DIM))],
        out_specs=seq,
        out_shape=jax.ShapeDtypeStruct((b, l, DA_WIDTH), BF16),
        scratch_shapes=[pltpu.VMEM((l, 2 * DA_V_DIM), BF16)] + [pltpu.VMEM((TOK_TILE, 2 * l), BF16)] * 2,
        compiler_params=_params("arbitrary", "arbitrary"),
    )(q, k, v, lam_vec, subln_g)


def _out_ffn_kernel(*refs, final):
    rg_ref, ssd_ref, da_ref, mod_ref, g2_ref, wo_ref, wg_ref, wu_ref, wd_ref, gf_ref, o_ref = refs[-11:]
    h_in = _load_stream(refs[:-11])
    g2 = g2_ref[...]
    nb = FFN_BATCH
    mods = [mod_ref[j] for j in range(nb)]
    mixes = []
    for j in range(nb):
        mix = jnp.dot(rg_ref[j], wo_ref[0:RG_WIDTH, :], preferred_element_type=F32)
        mix += jnp.dot(ssd_ref[j], wo_ref[RG_WIDTH:RG_WIDTH + SSD_WIDTH, :], preferred_element_type=F32)
        mix += jnp.dot(da_ref[j], wo_ref[RG_WIDTH + SSD_WIDTH:, :], preferred_element_type=F32)
        mixes.append(mix)
    hs = [h_in[j] + mods[j][2:3] * mixes[j] for j in range(nb)]
    ys = [(_rms(hs[j], g2) * (1.0 + mods[j][4:5]) + mods[j][3:4]).astype(BF16) for j in range(nb)]
    gates, ups = [], []
    for y in ys:
        gates.append(jnp.dot(y, wg_ref[...], preferred_element_type=F32))
        ups.append(jnp.dot(y, wu_ref[...], preferred_element_type=F32))
    acts = [(_silu(gates[j]) * ups[j]).astype(BF16) for j in range(nb)]
    downs = [jnp.dot(a, wd_ref[...], preferred_element_type=F32) for a in acts]
    for j in range(nb):
        h = hs[j] + mods[j][5:6] * downs[j]
        o_ref[j] = _rms(h, gf_ref[...]) if final else h


def _out_ffn(hs, rg, ssd, da, mod, norm2_g, w_out, w_gate, w_up, w_down, final_g, layer, final):
    b, l, _ = rg.shape
    nb = FFN_BATCH
    assert b % nb == 0
    t0 = CTX_LEN // TOK_TILE if final else 0
    nt = l // TOK_TILE - t0
    tok = lambda w: pl.BlockSpec((nb, TOK_TILE, w), lambda t, i: (i, t + t0, 0))
    out_rows = nt * TOK_TILE
    kwargs = {"input_output_aliases": {0: 0}} if (len(hs) == 1 and not final) else {}
    return pl.pallas_call(
        functools.partial(_out_ffn_kernel, final=final),
        grid=(nt, b // nb),
        in_specs=_stream_specs(hs, b, t0, nb) + [
            tok(RG_WIDTH), tok(SSD_WIDTH), tok(DA_WIDTH),
            pl.BlockSpec((nb, 6, D_MODEL), lambda t, i: (jnp.where(t + t0 == 0, b // nb, i), 0, 0)),
            _const_spec((1, D_MODEL)),
            _const_spec((D_MODEL, D_MODEL), layer),
            _const_spec((D_MODEL, D_FF), layer),
            _const_spec((D_MODEL, D_FF), layer),
            _const_spec((D_FF, D_MODEL), layer),
            _const_spec((1, D_MODEL)),
        ],
        out_specs=pl.BlockSpec((nb, TOK_TILE, D_MODEL), lambda t, i: (i, t, 0)),
        out_shape=jax.ShapeDtypeStruct((b, out_rows, D_MODEL), F32),
        compiler_params=_params("arbitrary", "arbitrary"),
        **kwargs,
    )(*hs, rg, ssd, da, mod, norm2_g, w_out, w_gate, w_up, w_down, final_g)


def _rope_tables(n_lat):
    half = DA_HEAD_DIM // 2
    inv_freq = jnp.power(ROPE_BASE, -jnp.arange(0, half, 2, dtype=F32) / half)
    t = jnp.arange(n_lat, dtype=jnp.int32)
    ang_r = (t // GRID_W).astype(F32)[:, None] * inv_freq
    ang_c = (t % GRID_W).astype(F32)[:, None] * inv_freq
    cos = jnp.concatenate([jnp.cos(ang_r), jnp.cos(ang_r), jnp.cos(ang_c), jnp.cos(ang_c)], axis=1)
    sin = jnp.concatenate([-jnp.sin(ang_r), jnp.sin(ang_r), -jnp.sin(ang_c), jnp.sin(ang_c)], axis=1)
    reps = DA_WIDTH // DA_HEAD_DIM
    cos = jnp.concatenate([jnp.ones((CTX_LEN, DA_HEAD_DIM), F32), cos], axis=0)
    sin = jnp.concatenate([jnp.zeros((CTX_LEN, DA_HEAD_DIM), F32), sin], axis=0)
    cos = jnp.tile(cos, (1, reps))
    sin = jnp.tile(sin, (1, reps))
    scale = DA_HEAD_DIM ** -0.5 * math.log2(math.e)
    return cos * scale, sin * scale, cos, sin


def _split_w_in(w):
    offs = [0]
    for size in IN_SIZES:
        offs.append(offs[-1] + size)
    w_qkv = w[:, :, offs[5]:offs[8]]
    w_rg = w[:, :, offs[0]:offs[2]]
    pad = jnp.zeros(w.shape[:2] + (LANES - IN_SIZES[4],), w.dtype)
    w_ssd = jnp.concatenate([w[:, :, offs[3]:offs[5]], pad, w[:, :, offs[2]:offs[3]]], axis=2)
    return w_qkv.astype(BF16), w_rg.astype(BF16), w_ssd.astype(BF16)


def _block_diag(w):
    eye = jnp.eye(RG_HEADS, dtype=w.dtype)
    return jnp.einsum("hij,hg->higj", w, eye).reshape(RG_WIDTH, RG_WIDTH)


def _lane_pad(v):
    return jnp.concatenate([v, jnp.zeros((LANES - v.shape[0],), v.dtype)])[None, :]


def kernel(x, c, ctx, c_ctx, w_mod, b_mod, norm1_g, w_in, rg_conv_w, rg_conv_b, rg_w_a, rg_b_a, rg_w_x, rg_b_x, rg_lambda, ssd_conv_w, ssd_conv_b, ssd_dt_bias, ssd_a_log, ssd_d, ssd_norm_g, da_lambda, da_subln_g, w_out, norm2_g, w_gate, w_up, w_down, final_norm_g):
    b, n_lat, _ = x.shape
    assert ctx.shape[1] == CTX_LEN and n_lat % TOK_TILE == 0
    n_ctx = max(INPROJ_BATCH, FFN_BATCH)
    rows = -(-(b + n_ctx) // SUBLANES) * SUBLANES
    c_all = jnp.concatenate([c] + [c_ctx[None, :]] * n_ctx
                            + [jnp.zeros((rows - b - n_ctx, D_MODEL), F32)], axis=0)
    mods = _modulation(c_all, w_mod, b_mod)
    tables = _rope_tables(n_lat)
    hs = (ctx, x)
    w_out16, w_gate16, w_up16, w_down16 = (w.astype(BF16) for w in (w_out, w_gate, w_up, w_down))
    w_qkv16, w_rg16, w_ssd16 = _split_w_in(w_in)
    out = None
    for l in range(DEPTH):
        final = l == DEPTH - 1
        y, q, k, v = _in_proj(hs, mods[l], norm1_g[l][None, :], w_qkv16, l, tables)
        w_gates = (0.5 * jnp.concatenate(
            [_block_diag(w[l, d]) for d in range(2) for w in (rg_w_a, rg_w_x)], axis=1)).astype(BF16)
        b_gates = 0.5 * jnp.concatenate(
            [bb[l, d] for d in range(2) for bb in (rg_b_a, rg_b_x)])[None, :]
        y_rg, u_ssd = _rglru(y, w_rg16, w_ssd16, l, rg_conv_w[l], rg_conv_b[l][None, :], w_gates, b_gates, rg_lambda[l])
        y_ssd = _ssd(u_ssd, ssd_conv_w[l], ssd_conv_b[l][None, :],
                     _lane_pad(ssd_dt_bias[l].reshape(-1)),
                     _lane_pad(-jnp.exp(ssd_a_log[l].reshape(-1))),
                     jnp.repeat(ssd_d[l], SSD_HEADDIM)[None, :], ssd_norm_g[l][None, :])
        y_da = _attention(q, k, v, da_lambda[l], da_subln_g[l][None, :], l)
        res = _out_ffn(hs, y_rg, y_ssd, y_da, mods[l], norm2_g[l][None, :], w_out16, w_gate16, w_up16,
                       w_down16, final_norm_g[None, :], l, final)
        if final:
            out = res
        else:
            hs = (res,)
    return out
```

```python
import functools
import math

import jax
import jax.numpy as jnp
from jax import lax
from jax.experimental import pallas as pl
from jax.experimental.pallas import tpu as pltpu

F32 = jnp.float32
BF16 = jnp.bfloat16

D_MODEL = 1024
DEPTH = 2
CTX_LEN = 256
GRID_W = 64
NORM_EPS = 1e-6

RG_WIDTH = 256
RG_HEADS = 4
RG_BLOCK = 64
RG_C = 8.0

SSD_WIDTH = 256
SSD_HEADDIM = 64
SSD_HEADS = 4
SSD_GROUPS = 2
SSD_STATE = 64
SSD_CHUNK = 128
SSD_XBC = 512

DA_WIDTH = 512
DA_HEADS = 4
DA_HEAD_DIM = 64
DA_V_DIM = 128
ROPE_BASE = 10000.0

D_FF = 2816
IN_SIZES = (256, 256, 256, 512, 8, 512, 512, 512)
D_IN = sum(IN_SIZES)

LANES = 128
SUBLANES = 8
MXU_COLS = 256
TOK_TILE = 256
INPROJ_BATCH = 4
FFN_BATCH = 2
CONV_PAD = SUBLANES
VMEM_LIMIT = 56 * 1024 * 1024

QKV_COLS = 3 * DA_WIDTH
RGP_COLS = 3 * RG_WIDTH
S_XBC = 0
S_DT = 512
S_Z = 640
SSDP_COLS = 896
PROJ_ROWS = 256
PROJ_AHEAD = 2


def _silu(x):
    hx = 0.5 * x
    return hx * jnp.tanh(hx) + hx


def _softplus(x):
    return jnp.maximum(x, 0.0) + jnp.log(1.0 + jnp.exp(-jnp.abs(x)))


def _rms(x, g):
    ms = jnp.mean(x * x, axis=-1, keepdims=True)
    return x * lax.rsqrt(ms + NORM_EPS) * g


def _const_spec(shape, layer=None):
    nd = len(shape)
    if layer is None:
        return pl.BlockSpec(shape, lambda *_: (0,) * nd, pipeline_mode=pl.Buffered(1))
    return pl.BlockSpec((None,) + tuple(shape), lambda *_: (layer,) + (0,) * nd, pipeline_mode=pl.Buffered(1))


def _params(*sem):
    return pltpu.CompilerParams(dimension_semantics=sem, vmem_limit_bytes=VMEM_LIMIT)


def _mod_kernel(c_ref, w_ref, b_ref, o_ref):
    c = c_ref[...]
    a = _silu(c).astype(BF16)
    o_ref[...] = jnp.dot(a, w_ref[...].astype(BF16), preferred_element_type=F32) + b_ref[...]


def _modulation(c_all, w_mod, b_mod):
    r = c_all.shape[0]
    out = pl.pallas_call(
        _mod_kernel,
        grid=(DEPTH, 6),
        in_specs=[
            pl.BlockSpec((r, D_MODEL), lambda l, j: (0, 0)),
            pl.BlockSpec((None, D_MODEL, D_MODEL), lambda l, j: (l, 0, j)),
            pl.BlockSpec((None, 1, D_MODEL), lambda l, j: (l, 0, j)),
        ],
        out_specs=pl.BlockSpec((None, r, D_MODEL), lambda l, j: (l, 0, j)),
        out_shape=jax.ShapeDtypeStruct((DEPTH, r, 6 * D_MODEL), F32),
        compiler_params=_params("arbitrary", "arbitrary"),
    )(c_all, w_mod, b_mod.reshape(DEPTH, 1, 6 * D_MODEL))
    return out.reshape(DEPTH, r, 6, D_MODEL)


def _rope(x, cos, sin, first_half):
    width = x.shape[-1]
    partner = jnp.where(first_half, pltpu.roll(x, width - 16, 1), pltpu.roll(x, 16, 1))
    return x * cos + partner * sin


def _load_stream(h_refs):
    if len(h_refs) == 1:
        return h_refs[0][...]
    return jnp.where(pl.program_id(0) == 0, h_refs[0][...], h_refs[1][...])


def _stream_specs(hs, b, t0=0, nb=None):
    lead = nb
    if len(hs) == 1:
        return [pl.BlockSpec((lead, TOK_TILE, D_MODEL), lambda t, i: (i, t + t0, 0))]
    assert t0 == 0
    return [pl.BlockSpec((lead, TOK_TILE, D_MODEL), lambda t, i: (jnp.where(t == 0, i, 0), 0, 0)),
            pl.BlockSpec((lead, TOK_TILE, D_MODEL), lambda t, i: (i, jnp.maximum(t - 1, 0), 0))]


def _inproj_kernel(*refs):
    mod_ref, g_ref, w_ref, cq_ref, sq_ref, ck_ref, sk_ref, y_ref, q_ref, k_ref, v_ref = refs[-11:]
    h = _load_stream(refs[:-11])
    g = g_ref[...]
    ys = []
    for j in range(INPROJ_BATCH):
        mod = mod_ref[j]
        ys.append((_rms(h[j], g) * (1.0 + mod[1:2]) + mod[0:1]).astype(BF16))
    us = [jnp.dot(y, w_ref[...], preferred_element_type=F32) for y in ys]
    lane = lax.broadcasted_iota(jnp.int32, (TOK_TILE, DA_WIDTH), 1)
    first_half = (lane % 32) < 16
    for j, u in enumerate(us):
        y_ref[j] = ys[j]
        q_ref[j] = _rope(u[:, 0:DA_WIDTH], cq_ref[...], sq_ref[...], first_half).astype(BF16)
        k_ref[j] = _rope(u[:, DA_WIDTH:2 * DA_WIDTH], ck_ref[...], sk_ref[...], first_half).astype(BF16)
        v_ref[j] = u[:, 2 * DA_WIDTH:QKV_COLS].astype(BF16)


def _in_proj(hs, mod, norm_g, w_qkv, layer, tables):
    b = hs[0].shape[0]
    l = sum(h.shape[1] for h in hs)
    nt = l // TOK_TILE
    nb = INPROJ_BATCH
    assert b % nb == 0
    tok = lambda w: pl.BlockSpec((nb, TOK_TILE, w), lambda t, i: (i, t, 0))
    tab = pl.BlockSpec((TOK_TILE, DA_WIDTH), lambda t, i: (t, 0))
    widths = (D_MODEL, DA_WIDTH, DA_WIDTH, DA_WIDTH)
    return pl.pallas_call(
        _inproj_kernel,
        grid=(nt, b // nb),
        in_specs=_stream_specs(hs, b, nb=nb) + [
            pl.BlockSpec((nb, 6, D_MODEL), lambda t, i: (jnp.where(t == 0, b // nb, i), 0, 0)),
            _const_spec((1, D_MODEL)),
            _const_spec((D_MODEL, QKV_COLS), layer),
            tab, tab, tab, tab,
        ],
        out_specs=[tok(w) for w in widths],
        out_shape=[jax.ShapeDtypeStruct((b, l, w), BF16) for w in widths],
        compiler_params=_params("arbitrary", "arbitrary"),
    )(*hs, mod, norm_g, w_qkv, *tables)


def _project_rows(y_ref, w_ref, u_ref, i):
    r0 = i * PROJ_ROWS
    rows = pl.ds(r0 if isinstance(r0, int) else pl.multiple_of(r0, PROJ_ROWS), PROJ_ROWS)
    y = y_ref[rows, :]
    ncols = w_ref.shape[1]
    half = ((ncols // MXU_COLS + 1) // 2) * MXU_COLS
    for cols in (slice(0, half), slice(half, ncols)):
        u_ref[rows, cols] = jnp.dot(y, w_ref[:, cols], preferred_element_type=F32)


def _conv_rows(x_ref, col0, width, r0, rows, w, bias):
    l = x_ref.shape[0]
    assert CTX_LEN % rows == 0
    span = rows + 2 * CONV_PAD
    lo = pl.multiple_of(jnp.maximum(r0 - CONV_PAD, 0), CONV_PAD)
    hi = pl.multiple_of(jnp.minimum(r0 + rows, l - CONV_PAD), CONV_PAD)
    cols = slice(col0, col0 + width)
    seg_start = jnp.logical_or(r0 == 0, r0 == CTX_LEN)
    seg_end = jnp.logical_or(r0 + rows == CTX_LEN, r0 + rows == l)
    xa = jnp.concatenate([jnp.where(seg_start, 0.0, x_ref[pl.ds(lo, CONV_PAD), cols]),
                          x_ref[pl.ds(r0, rows), cols],
                          jnp.where(seg_end, 0.0, x_ref[pl.ds(hi, CONV_PAD), cols])], axis=0)
    acc = None
    for tap in range(4):
        off = tap - 2
        sh = xa if off == 0 else pltpu.roll(xa, (-off) % span, 0)
        term = sh[CONV_PAD:CONV_PAD + rows] * w[tap:tap + 1]
        acc = term if acc is None else acc + term
    return acc + bias


RG_ROWS = 256


def _scan8(a, bx, reverse):
    row = lax.broadcasted_iota(jnp.int32, a.shape, 0)
    for s in (1, 2, 4):
        shift = (SUBLANES - s) if reverse else s
        a_sh = pltpu.roll(a, shift, 0)
        b_sh = pltpu.roll(bx, shift, 0)
        ok = (row < SUBLANES - s) if reverse else (row >= s)
        bx = jnp.where(ok, a * b_sh + bx, bx)
        a = jnp.where(ok, a * a_sh, a)
    return a, bx


def _rglru_kernel(y_ref, wp_ref, ws_ref, cw_ref, cb_ref, wg_ref, bg_ref, lam_ref, o_ref, us_ref,
                  rg_ref, af_ref, bf_ref, ab_ref, bb_ref):
    l = y_ref.shape[0]
    nproj = l // PROJ_ROWS
    assert RG_ROWS == PROJ_ROWS
    for i in range(PROJ_AHEAD):
        _project_rows(y_ref, wp_ref, rg_ref, i)
    cw = cw_ref[...]
    cb = cb_ref[...]
    bg = bg_ref[...]
    coef = (-0.5 * RG_C) * _softplus(-lam_ref[...])

    def coeffs(i, carry):
        r0 = i * RG_ROWS
        xc = _conv_rows(rg_ref, 0, RG_WIDTH, r0, RG_ROWS, cw, cb)
        t = jnp.tanh(jnp.dot(xc.astype(BF16), wg_ref[...], preferred_element_type=F32) + bg)
        hx = 0.5 * xc
        for d, (a_ref, b_ref) in enumerate(((af_ref, bf_ref), (ab_ref, bb_ref))):
            t_a = t[:, (2 * d) * RG_WIDTH:(2 * d + 1) * RG_WIDTH]
            t_x = t[:, (2 * d + 1) * RG_WIDTH:(2 * d + 2) * RG_WIDTH]
            log_a = coef[d:d + 1] * t_a + coef[d:d + 1]
            a = jnp.exp(log_a)
            one_m_a2 = jnp.tanh(log_a) * (-1.0 - a * a)
            a_ref[pl.ds(r0, RG_ROWS), :] = a
            b_ref[pl.ds(r0, RG_ROWS), :] = jnp.sqrt(one_m_a2) * (hx * t_x + hx)
        if i + PROJ_AHEAD < nproj:
            _project_rows(y_ref, wp_ref, rg_ref, i + PROJ_AHEAD)
        return carry

    for i in range(nproj):
        coeffs(i, 0)

    nblk = l // SUBLANES
    nctx = CTX_LEN // SUBLANES

    def scan_block(j, carry):
        hf, hb = carry
        rf = pl.multiple_of(j * SUBLANES, SUBLANES)
        jb = jnp.where(j < nctx, nctx - 1 - j, nblk + nctx - 1 - j)
        rb = pl.multiple_of(jb * SUBLANES, SUBLANES)
        a, bx = _scan8(af_ref[pl.ds(rf, SUBLANES), :], bf_ref[pl.ds(rf, SUBLANES), :], False)
        h = a * hf + bx
        af_ref[pl.ds(rf, SUBLANES), :] = h
        hf = h[SUBLANES - 1:SUBLANES]
        a, bx = _scan8(ab_ref[pl.ds(rb, SUBLANES), :], bb_ref[pl.ds(rb, SUBLANES), :], True)
        h = a * hb + bx
        ab_ref[pl.ds(rb, SUBLANES), :] = h
        hb = h[0:1]
        return hf, hb

    per_step = PROJ_ROWS // SUBLANES

    def scan(i, carry):
        _project_rows(y_ref, ws_ref, us_ref, i)
        for jj in range(per_step):
            carry = scan_block(i * per_step + jj, carry)
        return carry

    zero = jnp.zeros((1, RG_WIDTH), F32)
    lax.fori_loop(0, nproj, scan, (zero, zero))

    def finish(i, carry):
        r0 = pl.multiple_of(i * RG_ROWS, RG_ROWS)
        hsum = af_ref[pl.ds(r0, RG_ROWS), :] + ab_ref[pl.ds(r0, RG_ROWS), :]
        g = rg_ref[pl.ds(r0, RG_ROWS), RG_WIDTH:2 * RG_WIDTH]
        o_ref[pl.ds(r0, RG_ROWS), :] = (hsum * jax.nn.gelu(g, approximate=True)).astype(BF16)
        us_ref[pl.ds(r0, RG_ROWS), S_Z:SSDP_COLS] = rg_ref[pl.ds(r0, RG_ROWS), 2 * RG_WIDTH:RGP_COLS]
        return carry

    lax.fori_loop(0, l // RG_ROWS, finish, 0)


def _rglru(y, w_proj, w_ssd, layer, conv_w, conv_b, w_gates, b_gates, lam):
    b, l, _ = y.shape
    seq = lambda w: pl.BlockSpec((None, l, w), lambda i: (i, 0, 0))
    scratch = [pltpu.VMEM((l, RGP_COLS), F32)] + [pltpu.VMEM((l, RG_WIDTH), F32)] * 4
    return pl.pallas_call(
        _rglru_kernel,
        grid=(b,),
        in_specs=[
            seq(D_MODEL),
            _const_spec((D_MODEL, RGP_COLS), layer),
            _const_spec((D_MODEL, S_Z), layer),
            _const_spec((4, RG_WIDTH)),
            _const_spec((1, RG_WIDTH)),
            _const_spec((RG_WIDTH, 4 * RG_WIDTH)),
            _const_spec((1, 4 * RG_WIDTH)),
            _const_spec((2, RG_WIDTH)),
        ],
        out_specs=[seq(RG_WIDTH), seq(SSDP_COLS)],
        out_shape=[jax.ShapeDtypeStruct((b, l, RG_WIDTH), BF16), jax.ShapeDtypeStruct((b, l, SSDP_COLS), F32)],
        scratch_shapes=scratch,
        compiler_params=_params("arbitrary"),
    )(y, w_proj, w_ssd, conv_w, conv_b, w_gates, b_gates, lam)


SSD_ROWS = 64
GROUP_W = SSD_WIDTH // SSD_GROUPS
HEADS_PER_GROUP = SSD_HEADS // SSD_GROUPS
STATE_ROWS = SSD_GROUPS * SSD_STATE
NT_DIMS = (((1,), (1,)), ((), ()))
SSD_GROUP_CHUNKS = 9


def _ssd_head_lane(direction, group, j):
    return direction * SSD_HEADS + group * HEADS_PER_GROUP + j


def _ssd_kernel(u_ref, cw_ref, cb_ref, dtb_ref, aneg_ref, dsk_ref, ng_ref, o_ref,
                xbc_ref, dts_ref, a3_ref, y_ref, ecc_ref, ds_ref, sent_ref, etot_ref):
    l = u_ref.shape[0]
    q = SSD_CHUNK
    assert HEADS_PER_GROUP == 2 and GROUP_W == LANES and STATE_ROWS == LANES
    cw = cw_ref[...]
    cb = cb_ref[...]
    dtb = dtb_ref[...]
    dsk = dsk_ref[...]
    aneg = aneg_ref[...] * math.log2(math.e)

    def prep(i, carry):
        r0 = pl.multiple_of(i * SSD_ROWS, SSD_ROWS)
        rows = pl.ds(r0, SSD_ROWS)
        xbc = _silu(_conv_rows(u_ref, S_XBC, SSD_XBC, r0, SSD_ROWS, cw, cb))
        xbc_ref[rows, :] = xbc
        dt = _softplus(u_ref[rows, S_DT:S_Z] + dtb)
        dts_ref[rows, :] = dt
        y_ref[rows, :] = xbc[:, 0:SSD_WIDTH] * dsk
        a = dt * aneg
        hi = a.astype(BF16).astype(F32)
        r1 = a - hi
        mid = r1.astype(BF16).astype(F32)
        lo = r1 - mid
        a3_ref[rows, :] = (hi + pltpu.roll(mid, 2 * SSD_HEADS, 1) + pltpu.roll(lo, 4 * SSD_HEADS, 1)).astype(BF16)
        return carry

    lax.fori_loop(0, l // SSD_ROWS, prep, 0, unroll=3)

    ri = lax.broadcasted_iota(jnp.int32, (q, q), 0)
    ci = lax.broadcasted_iota(jnp.int32, (q, q), 1)
    lower = ri >= ci
    upper = ci >= ri
    tri_both = jnp.concatenate([lower.astype(F32), upper.astype(F32)], axis=0).astype(BF16)
    eye16 = (ri == ci).astype(F32).astype(BF16)
    eye_rows = (lax.broadcasted_iota(jnp.int32, (2 * SUBLANES, LANES), 0)
                == lax.broadcasted_iota(jnp.int32, (2 * SUBLANES, LANES), 1)).astype(F32)
    lane = lax.broadcasted_iota(jnp.int32, (q, LANES), 1)
    left = lane < SSD_HEADDIM
    lane_row = lax.broadcasted_iota(jnp.int32, (1, LANES), 1)
    left_row = lane_row < SSD_HEADDIM
    sub8 = lax.broadcasted_iota(jnp.int32, (SUBLANES, q), 0)
    left_state = lax.broadcasted_iota(jnp.int32, (SSD_STATE, GROUP_W), 1) < SSD_HEADDIM
    nchunk = l // q
    nctx = CTX_LEN // q

    def local(cg, carry):
        chunks = [cg * SSD_GROUP_CHUNKS + k for k in range(SSD_GROUP_CHUNKS)]
        rows = [pl.ds(pl.multiple_of(c * q, q), q) for c in chunks]

        def fold(cs):
            return cs + pltpu.roll(cs, LANES - 2 * SSD_HEADS, 1) + pltpu.roll(cs, LANES - 4 * SSD_HEADS, 1)

        x16, bm_t, grams, cs = [], [], [], []
        for r in rows:
            x16.append(xbc_ref[r, 0:SSD_WIDTH].astype(BF16))
            bm16 = xbc_ref[r, SSD_WIDTH:SSD_WIDTH + LANES].astype(BF16)
            cm = xbc_ref[r, SSD_WIDTH + LANES:SSD_XBC]
            cm_g = [jnp.where((lane < SSD_STATE) == (g == 0), cm, 0.0).astype(BF16) for g in range(SSD_GROUPS)]
            nt = lax.dot_general(jnp.concatenate(cm_g + [eye16], axis=0), bm16, NT_DIMS,
                                 preferred_element_type=F32)
            grams.append([nt[g * q:(g + 1) * q] for g in range(SSD_GROUPS)])
            bm_t.append(nt[SSD_GROUPS * q:])
            cs.append(jnp.dot(tri_both, a3_ref[r, :], preferred_element_type=F32))

        c_col, c_row, dt_row = [], [], []
        for k, r in enumerate(rows):
            cc = jnp.where(lane < SSD_HEADS, fold(cs[k][0:q]), fold(cs[k][q:2 * q]))
            c_col.append(cc)
            both = jnp.where(lane < 2 * SSD_HEADS, cc, pltpu.roll(dts_ref[r, :], 2 * SSD_HEADS, 1))
            rows_t = lax.dot_general(eye_rows, both, NT_DIMS, precision=lax.Precision.HIGHEST,
                                     preferred_element_type=F32)
            c_row.append(rows_t[0:SUBLANES])
            dt_row.append(rows_t[SUBLANES:2 * SUBLANES])

        zblock = jnp.zeros((SSD_STATE, GROUP_W), F32)
        for k, (c, r) in enumerate(zip(chunks, rows)):
            tot_col = jnp.where(sub8 < SSD_HEADS, c_row[k][:, q - 1:q], c_row[k][:, 0:1])
            dtw_row = dt_row[k] * jnp.exp2(tot_col - c_row[k])
            tot_row = jnp.where(lane_row < SSD_HEADS, c_col[k][q - 1:q, :], c_col[k][0:1, :])
            e_tot = jnp.exp2(tot_row)
            ds = [[None] * SSD_GROUPS for _ in range(2)]
            etot = [[None] * SSD_GROUPS for _ in range(2)]
            for g in range(SSD_GROUPS):
                cols = slice(g * GROUP_W, (g + 1) * GROUP_W)
                lhs_m, lhs_b, ecc = [], [], []
                for d in range(2):
                    mask = lower if d == 0 else upper
                    for j in range(HEADS_PER_GROUP):
                        li = _ssd_head_lane(d, g, j)
                        ccb = jnp.broadcast_to(c_col[k][:, li:li + 1], (q, q))
                        decay = jnp.exp2(jnp.where(mask, ccb - c_row[k][li:li + 1, :], -1e30))
                        lhs_m.append((grams[k][g] * decay * dt_row[k][li:li + 1, :]).astype(BF16))
                        lhs_b.append((bm_t[k][g * SSD_STATE:(g + 1) * SSD_STATE, :]
                                      * dtw_row[li:li + 1, :]).astype(BF16))
                        ecc.append(jnp.exp2(ccb))
                prod = jnp.dot(jnp.concatenate(lhs_m + lhs_b, axis=0), x16[k][:, cols],
                               preferred_element_type=F32)
                nm = 2 * HEADS_PER_GROUP * q
                for d in range(2):
                    a, b = (2 * d) * q, (2 * d + 1) * q
                    y_ref[r, cols] += jnp.where(left, prod[a:a + q], prod[b:b + q])
                    ecc_ref[d, r, cols] = jnp.where(left, ecc[2 * d], ecc[2 * d + 1])
                    a, b = nm + (2 * d) * SSD_STATE, nm + (2 * d + 1) * SSD_STATE
                    dsg = jnp.where(left_state, prod[a:a + SSD_STATE], prod[b:b + SSD_STATE])
                    ds[d][g] = jnp.concatenate([dsg, zblock] if g == 0 else [zblock, dsg], axis=1)
                    la = _ssd_head_lane(d, g, 0)
                    lb = _ssd_head_lane(d, g, 1)
                    etot[d][g] = jnp.where(left_row, e_tot[:, la:la + 1], e_tot[:, lb:lb + 1])
            for d in range(2):
                ds_ref[d, pl.ds(pl.multiple_of(c * STATE_ROWS, STATE_ROWS), STATE_ROWS), :] = (
                    jnp.concatenate(ds[d], axis=0))
                etot_ref[d, pl.ds(pl.multiple_of(c * SUBLANES, SUBLANES), SUBLANES), :] = (
                    jnp.broadcast_to(jnp.concatenate(etot[d], axis=1), (SUBLANES, SSD_WIDTH)))
        return carry

    assert nchunk % SSD_GROUP_CHUNKS == 0
    lax.fori_loop(0, nchunk // SSD_GROUP_CHUNKS, local, 0)

    def carry_state(j, state):
        sf, sb = state
        jb = jnp.where(j < nctx, nctx - 1 - j, nchunk + nctx - 1 - j)
        out = []
        for d, (c, s) in enumerate(((j, sf), (jb, sb))):
            srows = pl.ds(pl.multiple_of(c * STATE_ROWS, STATE_ROWS), STATE_ROWS)
            sent_ref[d, srows, :] = s.astype(BF16)
            e = etot_ref[d, pl.ds(pl.multiple_of(c * SUBLANES, SUBLANES), 1), :]
            out.append(e * s + ds_ref[d, srows, :])
        return tuple(out)

    zero = jnp.zeros((STATE_ROWS, SSD_WIDTH), F32)
    lax.fori_loop(0, nchunk, carry_state, (zero, zero))

    ng = ng_ref[...]

    def finish(c, carry):
        r0 = pl.multiple_of(c * q, q)
        rows = pl.ds(r0, q)
        srows = pl.ds(pl.multiple_of(c * STATE_ROWS, STATE_ROWS), STATE_ROWS)
        cm16 = xbc_ref[rows, SSD_WIDTH + LANES:SSD_XBC].astype(BF16)
        y = y_ref[rows, :]
        for d in range(2):
            y = y + ecc_ref[d, rows, :] * jnp.dot(cm16, sent_ref[d, srows, :], preferred_element_type=F32)
        v = y * _silu(u_ref[rows, S_Z:SSDP_COLS])
        parts = []
        for g in range(SSD_GROUPS):
            vg = v[:, g * GROUP_W:(g + 1) * GROUP_W]
            parts.append(vg * lax.rsqrt(jnp.mean(vg * vg, axis=-1, keepdims=True) + NORM_EPS))
        o_ref[rows, :] = (jnp.concatenate(parts, axis=1) * ng).astype(BF16)
        return carry

    lax.fori_loop(0, nchunk, finish, 0, unroll=3)


def _ssd(u, conv_w, conv_b, dt_bias, aneg, dskip, norm_g):
    b, l, _ = u.shape
    nchunk = l // SSD_CHUNK
    seq = lambda w: pl.BlockSpec((None, l, w), lambda i: (i, 0, 0))
    scratch = [
        pltpu.VMEM((l, SSD_XBC), F32),
        pltpu.VMEM((l, LANES), F32),
        pltpu.VMEM((l, LANES), BF16),
        pltpu.VMEM((l, SSD_WIDTH), F32),
        pltpu.VMEM((2, l, SSD_WIDTH), F32),
        pltpu.VMEM((2, nchunk * STATE_ROWS, SSD_WIDTH), F32),
        pltpu.VMEM((2, nchunk * STATE_ROWS, SSD_WIDTH), BF16),
        pltpu.VMEM((2, nchunk * SUBLANES, SSD_WIDTH), F32),
    ]
    return pl.pallas_call(
        _ssd_kernel,
        grid=(b,),
        in_specs=[
            seq(SSDP_COLS),
            _const_spec((4, SSD_XBC)),
            _const_spec((1, SSD_XBC)),
            _const_spec((1, LANES)),
            _const_spec((1, LANES)),
            _const_spec((1, SSD_WIDTH)),
            _const_spec((1, SSD_WIDTH)),
        ],
        out_specs=seq(SSD_WIDTH),
        out_shape=jax.ShapeDtypeStruct((b, l, SSD_WIDTH), BF16),
        scratch_shapes=scratch,
        compiler_params=_params("arbitrary"),
    )(u, conv_w, conv_b, dt_bias, aneg, dskip, norm_g)


def _attn_probs(q, k_ref, p_ref, nkeys):
    l = k_ref.shape[0]
    for c in range(2):
        qc = q[:, c * DA_HEAD_DIM:(c + 1) * DA_HEAD_DIM]
        kc = k_ref[0:nkeys, c * DA_HEAD_DIM:(c + 1) * DA_HEAD_DIM]
        s = lax.dot_general(qc, kc, (((1,), (1,)), ((), ())), preferred_element_type=F32)
        p_ref[:, c * l:c * l + nkeys] = jnp.exp2(s - jnp.max(s, axis=-1, keepdims=True)).astype(BF16)


def _attn_values(p_ref, vaug_ref, nkeys, lam, g, lam_init):
    l = vaug_ref.shape[0]
    outs = []
    for c in range(2):
        ov = jnp.dot(p_ref[:, c * l:c * l + nkeys], vaug_ref[0:nkeys, :], preferred_element_type=F32)
        outs.append(ov[:, 0:DA_V_DIM] / ov[:, DA_V_DIM:DA_V_DIM + 1])
    o = outs[0] - lam * outs[1]
    return (_rms(o, g) * (1.0 - lam_init)).astype(BF16)


def _attn_kernel(q_ref, k_ref, v_ref, lam_ref, g_ref, o_ref, vaug_ref, pa_ref, pb_ref, *, lam_init):
    l = q_ref.shape[0]
    nt = l // TOK_TILE
    assert CTX_LEN == TOK_TILE and nt % 2 == 1 and nt >= 3
    lv = lam_ref[...]
    lam = (jnp.exp(jnp.sum(lv[0:1] * lv[1:2], axis=-1, keepdims=True))
           - jnp.exp(jnp.sum(lv[2:3] * lv[3:4], axis=-1, keepdims=True)) + lam_init)
    g = g_ref[...]
    vaug_ref[:, 0:DA_V_DIM] = v_ref[...]
    vaug_ref[:, DA_V_DIM:2 * DA_V_DIM] = jnp.ones((l, DA_V_DIM), BF16)

    def rows(t):
        return pl.ds(pl.multiple_of(t * TOK_TILE, TOK_TILE), TOK_TILE)

    def probs(t, p_ref, nkeys=l):
        _attn_probs(q_ref[rows(t), :], k_ref, p_ref, nkeys)

    def values(t, p_ref, nkeys=l):
        o_ref[rows(t), :] = _attn_values(p_ref, vaug_ref, nkeys, lam, g, lam_init)

    probs(0, pa_ref, CTX_LEN)
    probs(1, pb_ref)
    values(0, pa_ref, CTX_LEN)

    for t in range(2, nt - 1, 2):
        probs(t, pa_ref)
        values(t - 1, pb_ref)
        probs(t + 1, pb_ref)
        values(t, pa_ref)
    probs(nt - 1, pa_ref)
    values(nt - 2, pb_ref)
    values(nt - 1, pa_ref)


def _attention(q, k, v, lam_vec, subln_g, layer_idx):
    b, l, _ = q.shape
    lam_init = 0.8 - 0.6 * math.exp(-0.3 * layer_idx)
    seq = pl.BlockSpec((None, l, DA_V_DIM), lambda i, h: (i, 0, h))
    return pl.pallas_call(
        functools.partial(_attn_kernel, lam_init=lam_init),
        grid=(b, DA_HEADS),
        in_specs=[seq, seq, seq, _const_spec((4, DA_HEAD_DIM)), _const_spec((1, DA_V_DIM))],
        out_specs=seq,
        out_shape=jax.ShapeDtypeStruct((b, l, DA_WIDTH), BF16),
        scratch_shapes=[pltpu.VMEM((l, 2 * DA_V_DIM), BF16)] + [pltpu.VMEM((TOK_TILE, 2 * l), BF16)] * 2,
        compiler_params=_params("arbitrary", "arbitrary"),
    )(q, k, v, lam_vec, subln_g)


def _out_ffn_kernel(*refs, final):
    rg_ref, ssd_ref, da_ref, mod_ref, g2_ref, wo_ref, wg_ref, wu_ref, wd_ref, gf_ref, o_ref = refs[-11:]
    h_in = _load_stream(refs[:-11])
    g2 = g2_ref[...]
    nb = FFN_BATCH
    mods = [mod_ref[j] for j in range(nb)]
    mixes = []
    for j in range(nb):
        mix = jnp.dot(rg_ref[j], wo_ref[0:RG_WIDTH, :], preferred_element_type=F32)
        mix += jnp.dot(ssd_ref[j], wo_ref[RG_WIDTH:RG_WIDTH + SSD_WIDTH, :], preferred_element_type=F32)
        mix += jnp.dot(da_ref[j], wo_ref[RG_WIDTH + SSD_WIDTH:, :], preferred_element_type=F32)
        mixes.append(mix)
    hs = [h_in[j] + mods[j][2:3] * mixes[j] for j in range(nb)]
    ys = [(_rms(hs[j], g2) * (1.0 + mods[j][4:5]) + mods[j][3:4]).astype(BF16) for j in range(nb)]
    gates, ups = [], []
    for y in ys:
        gates.append(jnp.dot(y, wg_ref[...], preferred_element_type=F32))
        ups.append(jnp.dot(y, wu_ref[...], preferred_element_type=F32))
    acts = [(_silu(gates[j]) * ups[j]).astype(BF16) for j in range(nb)]
    downs = [jnp.dot(a, wd_ref[...], preferred_element_type=F32) for a in acts]
    for j in range(nb):
        h = hs[j] + mods[j][5:6] * downs[j]
        o_ref[j] = _rms(h, gf_ref[...]) if final else h


def _out_ffn(hs, rg, ssd, da, mod, norm2_g, w_out, w_gate, w_up, w_down, final_g, layer, final):
    b, l, _ = rg.shape
    nb = FFN_BATCH
    assert b % nb == 0
    t0 = CTX_LEN // TOK_TILE if final else 0
    nt = l // TOK_TILE - t0
    tok = lambda w: pl.BlockSpec((nb, TOK_TILE, w), lambda t, i: (i, t + t0, 0))
    out_rows = nt * TOK_TILE
    kwargs = {"input_output_aliases": {0: 0}} if (len(hs) == 1 and not final) else {}
    return pl.pallas_call(
        functools.partial(_out_ffn_kernel, final=final),
        grid=(nt, b // nb),
        in_specs=_stream_specs(hs, b, t0, nb) + [
            tok(RG_WIDTH), tok(SSD_WIDTH), tok(DA_WIDTH),
            pl.BlockSpec((nb, 6, D_MODEL), lambda t, i: (jnp.where(t + t0 == 0, b // nb, i), 0, 0)),
            _const_spec((1, D_MODEL)),
            _const_spec((D_MODEL, D_MODEL), layer),
            _const_spec((D_MODEL, D_FF), layer),
            _const_spec((D_MODEL, D_FF), layer),
            _const_spec((D_FF, D_MODEL), layer),
            _const_spec((1, D_MODEL)),
        ],
        out_specs=pl.BlockSpec((nb, TOK_TILE, D_MODEL), lambda t, i: (i, t, 0)),
        out_shape=jax.ShapeDtypeStruct((b, out_rows, D_MODEL), F32),
        compiler_params=_params("arbitrary", "arbitrary"),
        **kwargs,
    )(*hs, rg, ssd, da, mod, norm2_g, w_out, w_gate, w_up, w_down, final_g)


def _rope_tables(n_lat):
    half = DA_HEAD_DIM // 2
    inv_freq = jnp.power(ROPE_BASE, -jnp.arange(0, half, 2, dtype=F32) / half)
    t = jnp.arange(n_lat, dtype=jnp.int32)
    ang_r = (t // GRID_W).astype(F32)[:, None] * inv_freq
    ang_c = (t % GRID_W).astype(F32)[:, None] * inv_freq
    cos = jnp.concatenate([jnp.cos(ang_r), jnp.cos(ang_r), jnp.cos(ang_c), jnp.cos(ang_c)], axis=1)
    sin = jnp.concatenate([-jnp.sin(ang_r), jnp.sin(ang_r), -jnp.sin(ang_c), jnp.sin(ang_c)], axis=1)
    reps = DA_WIDTH // DA_HEAD_DIM
    cos = jnp.concatenate([jnp.ones((CTX_LEN, DA_HEAD_DIM), F32), cos], axis=0)
    sin = jnp.concatenate([jnp.zeros((CTX_LEN, DA_HEAD_DIM), F32), sin], axis=0)
    cos = jnp.tile(cos, (1, reps))
    sin = jnp.tile(sin, (1, reps))
    scale = DA_HEAD_DIM ** -0.5 * math.log2(math.e)
    return cos * scale, sin * scale, cos, sin


def _split_w_in(w):
    offs = [0]
    for size in IN_SIZES:
        offs.append(offs[-1] + size)
    w_qkv = w[:, :, offs[5]:offs[8]]
    w_rg = w[:, :, offs[0]:offs[3]]
    pad = jnp.zeros(w.shape[:2] + (LANES - IN_SIZES[4],), w.dtype)
    w_ssd = jnp.concatenate([w[:, :, offs[3]:offs[5]], pad], axis=2)
    return w_qkv.astype(BF16), w_rg.astype(BF16), w_ssd.astype(BF16)


def _block_diag(w):
    eye = jnp.eye(RG_HEADS, dtype=w.dtype)
    return jnp.einsum("hij,hg->higj", w, eye).reshape(RG_WIDTH, RG_WIDTH)


def _lane_pad(v):
    return jnp.concatenate([v, jnp.zeros((LANES - v.shape[0],), v.dtype)])[None, :]


def kernel(x, c, ctx, c_ctx, w_mod, b_mod, norm1_g, w_in, rg_conv_w, rg_conv_b, rg_w_a, rg_b_a, rg_w_x, rg_b_x, rg_lambda, ssd_conv_w, ssd_conv_b, ssd_dt_bias, ssd_a_log, ssd_d, ssd_norm_g, da_lambda, da_subln_g, w_out, norm2_g, w_gate, w_up, w_down, final_norm_g):
    b, n_lat, _ = x.shape
    assert ctx.shape[1] == CTX_LEN and n_lat % TOK_TILE == 0
    n_ctx = max(INPROJ_BATCH, FFN_BATCH)
    rows = -(-(b + n_ctx) // SUBLANES) * SUBLANES
    c_all = jnp.concatenate([c] + [c_ctx[None, :]] * n_ctx
                            + [jnp.zeros((rows - b - n_ctx, D_MODEL), F32)], axis=0)
    mods = _modulation(c_all, w_mod, b_mod)
    tables = _rope_tables(n_lat)
    hs = (ctx, x)
    w_out16, w_gate16, w_up16, w_down16 = (w.astype(BF16) for w in (w_out, w_gate, w_up, w_down))
    w_qkv16, w_rg16, w_ssd16 = _split_w_in(w_in)
    out = None
    for l in range(DEPTH):
        final = l == DEPTH - 1
        y, q, k, v = _in_proj(hs, mods[l], norm1_g[l][None, :], w_qkv16, l, tables)
        w_gates = (0.5 * jnp.concatenate(
            [_block_diag(w[l, d]) for d in range(2) for w in (rg_w_a, rg_w_x)], axis=1)).astype(BF16)
        b_gates = 0.5 * jnp.concatenate(
            [bb[l, d] for d in range(2) for bb in (rg_b_a, rg_b_x)])[None, :]
        y_rg, u_ssd = _rglru(y, w_rg16, w_ssd16, l, rg_conv_w[l], rg_conv_b[l][None, :], w_gates, b_gates, rg_lambda[l])
        y_ssd = _ssd(u_ssd, ssd_conv_w[l], ssd_conv_b[l][None, :],
                     _lane_pad(ssd_dt_bias[l].reshape(-1)),
                     _lane_pad(-jnp.exp(ssd_a_log[l].reshape(-1))),
                     jnp.repeat(ssd_d[l], SSD_HEADDIM)[None, :], ssd_norm_g[l][None, :])
        y_da = _attention(q, k, v, da_lambda[l], da_subln_g[l][None, :], l)
        res = _out_ffn(hs, y_rg, y_ssd, y_da, mods[l], norm2_g[l][None, :], w_out16, w_gate16, w_up16,
                       w_down16, final_norm_g[None, :], l, final)
        if final:
            out = res
        else:
            hs = (res,)
    return out
```

```python
import functools
import math

import jax
import jax.numpy as jnp
from jax import lax
from jax.experimental import pallas as pl
from jax.experimental.pallas import tpu as pltpu

F32 = jnp.float32
BF16 = jnp.bfloat16

D_MODEL = 1024
DEPTH = 2
CTX_LEN = 256
GRID_W = 64
NORM_EPS = 1e-6

RG_WIDTH = 256
RG_HEADS = 4
RG_BLOCK = 64
RG_C = 8.0

SSD_WIDTH = 256
SSD_HEADDIM = 64
SSD_HEADS = 4
SSD_GROUPS = 2
SSD_STATE = 64
SSD_CHUNK = 128
SSD_XBC = 512

DA_WIDTH = 512
DA_HEADS = 4
DA_HEAD_DIM = 64
DA_V_DIM = 128
ROPE_BASE = 10000.0

D_FF = 2816
IN_SIZES = (256, 256, 256, 512, 8, 512, 512, 512)
D_IN = sum(IN_SIZES)

LANES = 128
SUBLANES = 8
MXU_COLS = 256
TOK_TILE = 256
INPROJ_BATCH = 4
FFN_BATCH = 2
CONV_PAD = SUBLANES
VMEM_LIMIT = 56 * 1024 * 1024

QKV_COLS = 3 * DA_WIDTH
RGP_COLS = 3 * RG_WIDTH
S_XBC = 0
S_DT = 512
S_Z = 640
SSDP_COLS = 896
PROJ_ROWS = 256
PROJ_AHEAD = 2


def _silu(x):
    hx = 0.5 * x
    return hx * jnp.tanh(hx) + hx


def _softplus(x):
    return jnp.maximum(x, 0.0) + jnp.log(1.0 + jnp.exp(-jnp.abs(x)))


def _rms(x, g):
    ms = jnp.mean(x * x, axis=-1, keepdims=True)
    return x * lax.rsqrt(ms + NORM_EPS) * g


def _const_spec(shape, layer=None):
    nd = len(shape)
    if layer is None:
        return pl.BlockSpec(shape, lambda *_: (0,) * nd, pipeline_mode=pl.Buffered(1))
    return pl.BlockSpec((None,) + tuple(shape), lambda *_: (layer,) + (0,) * nd, pipeline_mode=pl.Buffered(1))


def _params(*sem):
    return pltpu.CompilerParams(dimension_semantics=sem, vmem_limit_bytes=VMEM_LIMIT)


def _mod_kernel(c_ref, w_ref, b_ref, o_ref):
    c = c_ref[...]
    a = _silu(c).astype(BF16)
    o_ref[...] = jnp.dot(a, w_ref[...].astype(BF16), preferred_element_type=F32) + b_ref[...]


def _modulation(c_all, w_mod, b_mod):
    r = c_all.shape[0]
    out = pl.pallas_call(
        _mod_kernel,
        grid=(DEPTH, 6),
        in_specs=[
            pl.BlockSpec((r, D_MODEL), lambda l, j: (0, 0)),
            pl.BlockSpec((None, D_MODEL, D_MODEL), lambda l, j: (l, 0, j)),
            pl.BlockSpec((None, 1, D_MODEL), lambda l, j: (l, 0, j)),
        ],
        out_specs=pl.BlockSpec((None, r, D_MODEL), lambda l, j: (l, 0, j)),
        out_shape=jax.ShapeDtypeStruct((DEPTH, r, 6 * D_MODEL), F32),
        compiler_params=_params("arbitrary", "arbitrary"),
    )(c_all, w_mod, b_mod.reshape(DEPTH, 1, 6 * D_MODEL))
    return out.reshape(DEPTH, r, 6, D_MODEL)


def _rope(x, cos, sin, first_half):
    width = x.shape[-1]
    partner = jnp.where(first_half, pltpu.roll(x, width - 16, 1), pltpu.roll(x, 16, 1))
    return x * cos + partner * sin


def _load_stream(h_refs):
    if len(h_refs) == 1:
        return h_refs[0][...]
    return jnp.where(pl.program_id(0) == 0, h_refs[0][...], h_refs[1][...])


def _stream_specs(hs, b, t0=0, nb=None):
    lead = nb
    if len(hs) == 1:
        return [pl.BlockSpec((lead, TOK_TILE, D_MODEL), lambda t, i: (i, t + t0, 0))]
    assert t0 == 0
    return [pl.BlockSpec((lead, TOK_TILE, D_MODEL), lambda t, i: (jnp.where(t == 0, i, 0), 0, 0)),
            pl.BlockSpec((lead, TOK_TILE, D_MODEL), lambda t, i: (i, jnp.maximum(t - 1, 0), 0))]


def _inproj_kernel(*refs):
    mod_ref, g_ref, w_ref, cq_ref, sq_ref, ck_ref, sk_ref, y_ref, q_ref, k_ref, v_ref = refs[-11:]
    h = _load_stream(refs[:-11])
    g = g_ref[...]
    ys = []
    for j in range(INPROJ_BATCH):
        mod = mod_ref[j]
        ys.append((_rms(h[j], g) * (1.0 + mod[1:2]) + mod[0:1]).astype(BF16))
    us = [jnp.dot(y, w_ref[...], preferred_element_type=F32) for y in ys]
    lane = lax.broadcasted_iota(jnp.int32, (TOK_TILE, DA_WIDTH), 1)
    first_half = (lane % 32) < 16
    for j, u in enumerate(us):
        y_ref[j] = ys[j]
        q_ref[j] = _rope(u[:, 0:DA_WIDTH], cq_ref[...], sq_ref[...], first_half).astype(BF16)
        k_ref[j] = _rope(u[:, DA_WIDTH:2 * DA_WIDTH], ck_ref[...], sk_ref[...], first_half).astype(BF16)
        v_ref[j] = u[:, 2 * DA_WIDTH:QKV_COLS].astype(BF16)


def _in_proj(hs, mod, norm_g, w_qkv, layer, tables):
    b = hs[0].shape[0]
    l = sum(h.shape[1] for h in hs)
    nt = l // TOK_TILE
    nb = INPROJ_BATCH
    assert b % nb == 0
    tok = lambda w: pl.BlockSpec((nb, TOK_TILE, w), lambda t, i: (i, t, 0))
    tab = pl.BlockSpec((TOK_TILE, DA_WIDTH), lambda t, i: (t, 0))
    widths = (D_MODEL, DA_WIDTH, DA_WIDTH, DA_WIDTH)
    return pl.pallas_call(
        _inproj_kernel,
        grid=(nt, b // nb),
        in_specs=_stream_specs(hs, b, nb=nb) + [
            pl.BlockSpec((nb, 6, D_MODEL), lambda t, i: (jnp.where(t == 0, b // nb, i), 0, 0)),
            _const_spec((1, D_MODEL)),
            _const_spec((D_MODEL, QKV_COLS), layer),
            tab, tab, tab, tab,
        ],
        out_specs=[tok(w) for w in widths],
        out_shape=[jax.ShapeDtypeStruct((b, l, w), BF16) for w in widths],
        compiler_params=_params("arbitrary", "arbitrary"),
    )(*hs, mod, norm_g, w_qkv, *tables)


def _project_rows(y_ref, w_ref, u_ref, i):
    r0 = i * PROJ_ROWS
    rows = pl.ds(r0 if isinstance(r0, int) else pl.multiple_of(r0, PROJ_ROWS), PROJ_ROWS)
    y = y_ref[rows, :]
    ncols = w_ref.shape[1]
    half = ((ncols // MXU_COLS + 1) // 2) * MXU_COLS
    for cols in (slice(0, half), slice(half, ncols)):
        u_ref[rows, cols] = jnp.dot(y, w_ref[:, cols], preferred_element_type=F32)


def _conv_rows(x_ref, col0, width, r0, rows, w, bias):
    l = x_ref.shape[0]
    assert CTX_LEN % rows == 0
    span = rows + 2 * CONV_PAD
    lo = pl.multiple_of(jnp.maximum(r0 - CONV_PAD, 0), CONV_PAD)
    hi = pl.multiple_of(jnp.minimum(r0 + rows, l - CONV_PAD), CONV_PAD)
    cols = slice(col0, col0 + width)
    seg_start = jnp.logical_or(r0 == 0, r0 == CTX_LEN)
    seg_end = jnp.logical_or(r0 + rows == CTX_LEN, r0 + rows == l)
    xa = jnp.concatenate([jnp.where(seg_start, 0.0, x_ref[pl.ds(lo, CONV_PAD), cols]),
                          x_ref[pl.ds(r0, rows), cols],
                          jnp.where(seg_end, 0.0, x_ref[pl.ds(hi, CONV_PAD), cols])], axis=0)
    acc = None
    for tap in range(4):
        off = tap - 2
        sh = xa if off == 0 else pltpu.roll(xa, (-off) % span, 0)
        term = sh[CONV_PAD:CONV_PAD + rows] * w[tap:tap + 1]
        acc = term if acc is None else acc + term
    return acc + bias


RG_ROWS = 256


def _scan8(a, bx, reverse):
    row = lax.broadcasted_iota(jnp.int32, a.shape, 0)
    for s in (1, 2, 4):
        shift = (SUBLANES - s) if reverse else s
        a_sh = pltpu.roll(a, shift, 0)
        b_sh = pltpu.roll(bx, shift, 0)
        ok = (row < SUBLANES - s) if reverse else (row >= s)
        bx = jnp.where(ok, a * b_sh + bx, bx)
        a = jnp.where(ok, a * a_sh, a)
    return a, bx


def _rglru_kernel(y_ref, wp_ref, ws_ref, cw_ref, cb_ref, wg_ref, bg_ref, lam_ref, o_ref, us_ref,
                  rg_ref, af_ref, bf_ref, ab_ref, bb_ref):
    l = y_ref.shape[0]
    nproj = l // PROJ_ROWS
    assert RG_ROWS == PROJ_ROWS
    for i in range(PROJ_AHEAD):
        _project_rows(y_ref, wp_ref, rg_ref, i)
    cw = cw_ref[...]
    cb = cb_ref[...]
    bg = bg_ref[...]
    coef = (-0.5 * RG_C) * _softplus(-lam_ref[...])

    def coeffs(i, carry):
        r0 = i * RG_ROWS
        xc = _conv_rows(rg_ref, 0, RG_WIDTH, r0, RG_ROWS, cw, cb)
        t = jnp.tanh(jnp.dot(xc.astype(BF16), wg_ref[...], preferred_element_type=F32) + bg)
        hx = 0.5 * xc
        for d, (a_ref, b_ref) in enumerate(((af_ref, bf_ref), (ab_ref, bb_ref))):
            t_a = t[:, (2 * d) * RG_WIDTH:(2 * d + 1) * RG_WIDTH]
            t_x = t[:, (2 * d + 1) * RG_WIDTH:(2 * d + 2) * RG_WIDTH]
            log_a = coef[d:d + 1] * t_a + coef[d:d + 1]
            a = jnp.exp(log_a)
            one_m_a2 = jnp.tanh(log_a) * (-1.0 - a * a)
            a_ref[pl.ds(r0, RG_ROWS), :] = a
            b_ref[pl.ds(r0, RG_ROWS), :] = jnp.sqrt(one_m_a2) * (hx * t_x + hx)
        if i + PROJ_AHEAD < nproj:
            _project_rows(y_ref, wp_ref, rg_ref, i + PROJ_AHEAD)
        return carry

    for i in range(nproj):
        coeffs(i, 0)

    nblk = l // SUBLANES
    nctx = CTX_LEN // SUBLANES

    def scan_block(j, carry):
        hf, hb = carry
        rf = pl.multiple_of(j * SUBLANES, SUBLANES)
        jb = jnp.where(j < nctx, nctx - 1 - j, nblk + nctx - 1 - j)
        rb = pl.multiple_of(jb * SUBLANES, SUBLANES)
        a, bx = _scan8(af_ref[pl.ds(rf, SUBLANES), :], bf_ref[pl.ds(rf, SUBLANES), :], False)
        h = a * hf + bx
        af_ref[pl.ds(rf, SUBLANES), :] = h
        hf = h[SUBLANES - 1:SUBLANES]
        a, bx = _scan8(ab_ref[pl.ds(rb, SUBLANES), :], bb_ref[pl.ds(rb, SUBLANES), :], True)
        h = a * hb + bx
        ab_ref[pl.ds(rb, SUBLANES), :] = h
        hb = h[0:1]
        return hf, hb

    per_step = PROJ_ROWS // SUBLANES

    def scan(i, carry):
        _project_rows(y_ref, ws_ref, us_ref, i)
        for jj in range(per_step):
            carry = scan_block(i * per_step + jj, carry)
        return carry

    zero = jnp.zeros((1, RG_WIDTH), F32)
    lax.fori_loop(0, nproj, scan, (zero, zero))

    def finish(i, carry):
        r0 = pl.multiple_of(i * RG_ROWS, RG_ROWS)
        hsum = af_ref[pl.ds(r0, RG_ROWS), :] + ab_ref[pl.ds(r0, RG_ROWS), :]
        g = rg_ref[pl.ds(r0, RG_ROWS), RG_WIDTH:2 * RG_WIDTH]
        o_ref[pl.ds(r0, RG_ROWS), :] = (hsum * jax.nn.gelu(g, approximate=True)).astype(BF16)
        us_ref[pl.ds(r0, RG_ROWS), S_Z:SSDP_COLS] = rg_ref[pl.ds(r0, RG_ROWS), 2 * RG_WIDTH:RGP_COLS]
        return carry

    lax.fori_loop(0, l // RG_ROWS, finish, 0)


def _rglru(y, w_proj, w_ssd, layer, conv_w, conv_b, w_gates, b_gates, lam):
    b, l, _ = y.shape
    seq = lambda w: pl.BlockSpec((None, l, w), lambda i: (i, 0, 0))
    scratch = [pltpu.VMEM((l, RGP_COLS), F32)] + [pltpu.VMEM((l, RG_WIDTH), F32)] * 4
    return pl.pallas_call(
        _rglru_kernel,
        grid=(b,),
        in_specs=[
            seq(D_MODEL),
            _const_spec((D_MODEL, RGP_COLS), layer),
            _const_spec((D_MODEL, S_Z), layer),
            _const_spec((4, RG_WIDTH)),
            _const_spec((1, RG_WIDTH)),
            _const_spec((RG_WIDTH, 4 * RG_WIDTH)),
            _const_spec((1, 4 * RG_WIDTH)),
            _const_spec((2, RG_WIDTH)),
        ],
        out_specs=[seq(RG_WIDTH), seq(SSDP_COLS)],
        out_shape=[jax.ShapeDtypeStruct((b, l, RG_WIDTH), BF16), jax.ShapeDtypeStruct((b, l, SSDP_COLS), F32)],
        scratch_shapes=scratch,
        compiler_params=_params("arbitrary"),
    )(y, w_proj, w_ssd, conv_w, conv_b, w_gates, b_gates, lam)


SSD_ROWS = 64
GROUP_W = SSD_WIDTH // SSD_GROUPS
HEADS_PER_GROUP = SSD_HEADS // SSD_GROUPS
STATE_ROWS = SSD_GROUPS * SSD_STATE
NT_DIMS = (((1,), (1,)), ((), ()))
SSD_GROUP_CHUNKS = 9


def _ssd_head_lane(direction, group, j):
    return direction * SSD_HEADS + group * HEADS_PER_GROUP + j


def _ssd_kernel(u_ref, cw_ref, cb_ref, dtb_ref, aneg_ref, dsk_ref, ng_ref, o_ref,
                xbc_ref, dts_ref, a3_ref, y_ref, ecc_ref, ds_ref, sent_ref, etot_ref):
    l = u_ref.shape[0]
    q = SSD_CHUNK
    assert HEADS_PER_GROUP == 2 and GROUP_W == LANES and STATE_ROWS == LANES
    cw = cw_ref[...]
    cb = cb_ref[...]
    dtb = dtb_ref[...]
    dsk = dsk_ref[...]
    aneg = aneg_ref[...] * math.log2(math.e)

    def prep(i, carry):
        r0 = pl.multiple_of(i * SSD_ROWS, SSD_ROWS)
        rows = pl.ds(r0, SSD_ROWS)
        xbc = _silu(_conv_rows(u_ref, S_XBC, SSD_XBC, r0, SSD_ROWS, cw, cb))
        xbc_ref[rows, :] = xbc
        dt = _softplus(u_ref[rows, S_DT:S_Z] + dtb)
        dts_ref[rows, :] = dt
        y_ref[rows, :] = xbc[:, 0:SSD_WIDTH] * dsk
        a = dt * aneg
        hi = a.astype(BF16).astype(F32)
        r1 = a - hi
        mid = r1.astype(BF16).astype(F32)
        lo = r1 - mid
        a3_ref[rows, :] = (hi + pltpu.roll(mid, 2 * SSD_HEADS, 1) + pltpu.roll(lo, 4 * SSD_HEADS, 1)).astype(BF16)
        return carry

    lax.fori_loop(0, l // SSD_ROWS, prep, 0, unroll=3)

    ri = lax.broadcasted_iota(jnp.int32, (q, q), 0)
    ci = lax.broadcasted_iota(jnp.int32, (q, q), 1)
    lower = ri >= ci
    upper = ci >= ri
    tri_both = jnp.concatenate([lower.astype(F32), upper.astype(F32)], axis=0).astype(BF16)
    eye16 = (ri == ci).astype(F32).astype(BF16)
    eye_rows = (lax.broadcasted_iota(jnp.int32, (2 * SUBLANES, LANES), 0)
                == lax.broadcasted_iota(jnp.int32, (2 * SUBLANES, LANES), 1)).astype(F32)
    lane = lax.broadcasted_iota(jnp.int32, (q, LANES), 1)
    left = lane < SSD_HEADDIM
    lane_row = lax.broadcasted_iota(jnp.int32, (1, LANES), 1)
    left_row = lane_row < SSD_HEADDIM
    sub8 = lax.broadcasted_iota(jnp.int32, (SUBLANES, q), 0)
    left_state = lax.broadcasted_iota(jnp.int32, (SSD_STATE, GROUP_W), 1) < SSD_HEADDIM
    nchunk = l // q
    nctx = CTX_LEN // q

    def local(cg, carry):
        chunks = [cg * SSD_GROUP_CHUNKS + k for k in range(SSD_GROUP_CHUNKS)]
        rows = [pl.ds(pl.multiple_of(c * q, q), q) for c in chunks]

        def fold(cs):
            return cs + pltpu.roll(cs, LANES - 2 * SSD_HEADS, 1) + pltpu.roll(cs, LANES - 4 * SSD_HEADS, 1)

        x16, bm_t, grams, cs = [], [], [], []
        for r in rows:
            x16.append(xbc_ref[r, 0:SSD_WIDTH].astype(BF16))
            bm16 = xbc_ref[r, SSD_WIDTH:SSD_WIDTH + LANES].astype(BF16)
            cm = xbc_ref[r, SSD_WIDTH + LANES:SSD_XBC]
            cm_g = [jnp.where((lane < SSD_STATE) == (g == 0), cm, 0.0).astype(BF16) for g in range(SSD_GROUPS)]
            nt = lax.dot_general(jnp.concatenate(cm_g + [eye16], axis=0), bm16, NT_DIMS,
                                 preferred_element_type=F32)
            grams.append([nt[g * q:(g + 1) * q] for g in range(SSD_GROUPS)])
            bm_t.append(nt[SSD_GROUPS * q:])
            cs.append(jnp.dot(tri_both, a3_ref[r, :], preferred_element_type=F32))

        c_col, c_row, dt_row = [], [], []
        for k, r in enumerate(rows):
            cc = jnp.where(lane < SSD_HEADS, fold(cs[k][0:q]), fold(cs[k][q:2 * q]))
            c_col.append(cc)
            both = jnp.where(lane < 2 * SSD_HEADS, cc, pltpu.roll(dts_ref[r, :], 2 * SSD_HEADS, 1))
            rows_t = lax.dot_general(eye_rows, both, NT_DIMS, precision=lax.Precision.HIGHEST,
                                     preferred_element_type=F32)
            c_row.append(rows_t[0:SUBLANES])
            dt_row.append(rows_t[SUBLANES:2 * SUBLANES])

        zblock = jnp.zeros((SSD_STATE, GROUP_W), F32)
        for k, (c, r) in enumerate(zip(chunks, rows)):
            tot_col = jnp.where(sub8 < SSD_HEADS, c_row[k][:, q - 1:q], c_row[k][:, 0:1])
            dtw_row = dt_row[k] * jnp.exp2(tot_col - c_row[k])
            tot_row = jnp.where(lane_row < SSD_HEADS, c_col[k][q - 1:q, :], c_col[k][0:1, :])
            e_tot = jnp.exp2(tot_row)
            ds = [[None] * SSD_GROUPS for _ in range(2)]
            etot = [[None] * SSD_GROUPS for _ in range(2)]
            for g in range(SSD_GROUPS):
                cols = slice(g * GROUP_W, (g + 1) * GROUP_W)
                lhs_m, lhs_b, ecc = [], [], []
                for d in range(2):
                    mask = lower if d == 0 else upper
                    for j in range(HEADS_PER_GROUP):
                        li = _ssd_head_lane(d, g, j)
                        ccb = jnp.broadcast_to(c_col[k][:, li:li + 1], (q, q))
                        decay = jnp.exp2(jnp.where(mask, ccb - c_row[k][li:li + 1, :], -1e30))
                        lhs_m.append((grams[k][g] * decay * dt_row[k][li:li + 1, :]).astype(BF16))
                        lhs_b.append((bm_t[k][g * SSD_STATE:(g + 1) * SSD_STATE, :]
                                      * dtw_row[li:li + 1, :]).astype(BF16))
                        ecc.append(jnp.exp2(ccb))
                prod = jnp.dot(jnp.concatenate(lhs_m + lhs_b, axis=0), x16[k][:, cols],
                               preferred_element_type=F32)
                nm = 2 * HEADS_PER_GROUP * q
                for d in range(2):
                    a, b = (2 * d) * q, (2 * d + 1) * q
                    y_ref[r, cols] += jnp.where(left, prod[a:a + q], prod[b:b + q])
                    ecc_ref[d, r, cols] = jnp.where(left, ecc[2 * d], ecc[2 * d + 1])
                    a, b = nm + (2 * d) * SSD_STATE, nm + (2 * d + 1) * SSD_STATE
                    dsg = jnp.where(left_state, prod[a:a + SSD_STATE], prod[b:b + SSD_STATE])
                    ds[d][g] = jnp.concatenate([dsg, zblock] if g == 0 else [zblock, dsg], axis=1)
                    la = _ssd_head_lane(d, g, 0)
                    lb = _ssd_head_lane(d, g, 1)
                    etot[d][g] = jnp.where(left_row, e_tot[:, la:la + 1], e_tot[:, lb:lb + 1])
            for d in range(2):
                ds_ref[d, pl.ds(pl.multiple_of(c * STATE_ROWS, STATE_ROWS), STATE_ROWS), :] = (
                    jnp.concatenate(ds[d], axis=0))
                etot_ref[d, pl.ds(pl.multiple_of(c * SUBLANES, SUBLANES), SUBLANES), :] = (
                    jnp.broadcast_to(jnp.concatenate(etot[d], axis=1), (SUBLANES, SSD_WIDTH)))
        return carry

    assert nchunk % SSD_GROUP_CHUNKS == 0
    lax.fori_loop(0, nchunk // SSD_GROUP_CHUNKS, local, 0)

    def carry_state(j, state):
        sf, sb = state
        jb = jnp.where(j < nctx, nctx - 1 - j, nchunk + nctx - 1 - j)
        out = []
        for d, (c, s) in enumerate(((j, sf), (jb, sb))):
            srows = pl.ds(pl.multiple_of(c * STATE_ROWS, STATE_ROWS), STATE_ROWS)
            sent_ref[d, srows, :] = s.astype(BF16)
            e = etot_ref[d, pl.ds(pl.multiple_of(c * SUBLANES, SUBLANES), 1), :]
            out.append(e * s + ds_ref[d, srows, :])
        return tuple(out)

    zero = jnp.zeros((STATE_ROWS, SSD_WIDTH), F32)
    lax.fori_loop(0, nchunk, carry_state, (zero, zero))

    ng = ng_ref[...]

    def finish(c, carry):
        r0 = pl.multiple_of(c * q, q)
        rows = pl.ds(r0, q)
        srows = pl.ds(pl.multiple_of(c * STATE_ROWS, STATE_ROWS), STATE_ROWS)
        cm16 = xbc_ref[rows, SSD_WIDTH + LANES:SSD_XBC].astype(BF16)
        y = y_ref[rows, :]
        for d in range(2):
            y = y + ecc_ref[d, rows, :] * jnp.dot(cm16, sent_ref[d, srows, :], preferred_element_type=F32)
        v = y * _silu(u_ref[rows, S_Z:SSDP_COLS])
        parts = []
        for g in range(SSD_GROUPS):
            vg = v[:, g * GROUP_W:(g + 1) * GROUP_W]
            parts.append(vg * lax.rsqrt(jnp.mean(vg * vg, axis=-1, keepdims=True) + NORM_EPS))
        o_ref[rows, :] = (jnp.concatenate(parts, axis=1) * ng).astype(BF16)
        return carry

    lax.fori_loop(0, nchunk, finish, 0, unroll=3)


def _ssd(u, conv_w, conv_b, dt_bias, aneg, dskip, norm_g):
    b, l, _ = u.shape
    nchunk = l // SSD_CHUNK
    seq = lambda w: pl.BlockSpec((None, l, w), lambda i: (i, 0, 0))
    scratch = [
        pltpu.VMEM((l, SSD_XBC), F32),
        pltpu.VMEM((l, LANES), F32),
        pltpu.VMEM((l, LANES), BF16),
        pltpu.VMEM((l, SSD_WIDTH), F32),
        pltpu.VMEM((2, l, SSD_WIDTH), F32),
        pltpu.VMEM((2, nchunk * STATE_ROWS, SSD_WIDTH), F32),
        pltpu.VMEM((2, nchunk * STATE_ROWS, SSD_WIDTH), BF16),
        pltpu.VMEM((2, nchunk * SUBLANES, SSD_WIDTH), F32),
    ]
    return pl.pallas_call(
        _ssd_kernel,
        grid=(b,),
        in_specs=[
            seq(SSDP_COLS),
            _const_spec((4, SSD_XBC)),
            _const_spec((1, SSD_XBC)),
            _const_spec((1, LANES)),
            _const_spec((1, LANES)),
            _const_spec((1, SSD_WIDTH)),
            _const_spec((1, SSD_WIDTH)),
        ],
        out_specs=seq(SSD_WIDTH),
        out_shape=jax.ShapeDtypeStruct((b, l, SSD_WIDTH), BF16),
        scratch_shapes=scratch,
        compiler_params=_params("arbitrary"),
    )(u, conv_w, conv_b, dt_bias, aneg, dskip, norm_g)


def _attn_probs(q, k_ref, p_ref, nkeys):
    l = k_ref.shape[0]
    for c in range(2):
        qc = q[:, c * DA_HEAD_DIM:(c + 1) * DA_HEAD_DIM]
        kc = k_ref[0:nkeys, c * DA_HEAD_DIM:(c + 1) * DA_HEAD_DIM]
        s = lax.dot_general(qc, kc, (((1,), (1,)), ((), ())), preferred_element_type=F32)
        p_ref[:, c * l:c * l + nkeys] = jnp.exp2(s - jnp.max(s, axis=-1, keepdims=True)).astype(BF16)


def _attn_values(p_ref, vaug_ref, nkeys, lam, g, lam_init):
    l = vaug_ref.shape[0]
    outs = []
    for c in range(2):
        ov = jnp.dot(p_ref[:, c * l:c * l + nkeys], vaug_ref[0:nkeys, :], preferred_element_type=F32)
        outs.append(ov[:, 0:DA_V_DIM] / ov[:, DA_V_DIM:DA_V_DIM + 1])
    o = outs[0] - lam * outs[1]
    return (_rms(o, g) * (1.0 - lam_init)).astype(BF16)


def _attn_kernel(q_ref, k_ref, v_ref, lam_ref, g_ref, *refs, lam_init):
    nw = len(FFN_WEIGHT_SHAPES)
    w_refs, o_ref, w16_refs = refs[:nw], refs[nw], refs[nw + 1:2 * nw + 1]
    vaug_ref, pa_ref, pb_ref = refs[2 * nw + 1:]

    @pl.when(pl.program_id(1) == 0)
    def _():
        for w_ref, w16_ref in zip(w_refs, w16_refs):
            w16_ref[...] = w_ref[...].astype(BF16)

    l = q_ref.shape[0]
    nt = l // TOK_TILE
    assert CTX_LEN == TOK_TILE and nt % 2 == 1 and nt >= 3
    lv = lam_ref[...]
    lam = (jnp.exp(jnp.sum(lv[0:1] * lv[1:2], axis=-1, keepdims=True))
           - jnp.exp(jnp.sum(lv[2:3] * lv[3:4], axis=-1, keepdims=True)) + lam_init)
    g = g_ref[...]
    vaug_ref[:, 0:DA_V_DIM] = v_ref[...]
    vaug_ref[:, DA_V_DIM:2 * DA_V_DIM] = jnp.ones((l, DA_V_DIM), BF16)

    def rows(t):
        return pl.ds(pl.multiple_of(t * TOK_TILE, TOK_TILE), TOK_TILE)

    def probs(t, p_ref, nkeys=l):
        _attn_probs(q_ref[rows(t), :], k_ref, p_ref, nkeys)

    def values(t, p_ref, nkeys=l):
        o_ref[rows(t), :] = _attn_values(p_ref, vaug_ref, nkeys, lam, g, lam_init)

    probs(0, pa_ref, CTX_LEN)
    probs(1, pb_ref)
    values(0, pa_ref, CTX_LEN)

    for t in range(2, nt - 1, 2):
        probs(t, pa_ref)
        values(t - 1, pb_ref)
        probs(t + 1, pb_ref)
        values(t, pa_ref)
    probs(nt - 1, pa_ref)
    values(nt - 2, pb_ref)
    values(nt - 1, pa_ref)


FFN_WEIGHT_SHAPES = ((D_MODEL, D_MODEL), (D_MODEL, D_FF), (D_MODEL, D_FF), (D_FF, D_MODEL))


def _attention(q, k, v, lam_vec, subln_g, layer_idx, ffn_weights):
    b, l, _ = q.shape
    lam_init = 0.8 - 0.6 * math.exp(-0.3 * layer_idx)
    seq = pl.BlockSpec((None, l, DA_V_DIM), lambda i, h: (i, 0, h))
    w_in_specs, w_out_specs, w_out_shapes = [], [], []
    for rows, cols in FFN_WEIGHT_SHAPES:
        assert rows % b == 0 and (rows // b) % (2 * SUBLANES) == 0
        w_in_specs.append(pl.BlockSpec((None, rows // b, cols), lambda i, h: (layer_idx, i, 0)))
        w_out_specs.append(pl.BlockSpec((rows // b, cols), lambda i, h: (i, 0)))
        w_out_shapes.append(jax.ShapeDtypeStruct((rows, cols), BF16))
    res = pl.pallas_call(
        functools.partial(_attn_kernel, lam_init=lam_init),
        grid=(b, DA_HEADS),
        in_specs=[seq, seq, seq, _const_spec((4, DA_HEAD_DIM)), _const_spec((1, DA_V_DIM))] + w_in_specs,
        out_specs=[seq] + w_out_specs,
        out_shape=[jax.ShapeDtypeStruct((b, l, DA_WIDTH), BF16)] + w_out_shapes,
        scratch_shapes=[pltpu.VMEM((l, 2 * DA_V_DIM), BF16)] + [pltpu.VMEM((TOK_TILE, 2 * l), BF16)] * 2,
        compiler_params=_params("arbitrary", "arbitrary"),
    )(q, k, v, lam_vec, subln_g, *ffn_weights)
    return res[0], res[1:]


def _out_ffn_kernel(*refs, final):
    rg_ref, ssd_ref, da_ref, mod_ref, g2_ref, wo_ref, wg_ref, wu_ref, wd_ref, gf_ref, o_ref = refs[-11:]
    h_in = _load_stream(refs[:-11])
    g2 = g2_ref[...]
    nb = FFN_BATCH
    mods = [mod_ref[j] for j in range(nb)]
    mixes = []
    for j in range(nb):
        mix = jnp.dot(rg_ref[j], wo_ref[0:RG_WIDTH, :], preferred_element_type=F32)
        mix += jnp.dot(ssd_ref[j], wo_ref[RG_WIDTH:RG_WIDTH + SSD_WIDTH, :], preferred_element_type=F32)
        mix += jnp.dot(da_ref[j], wo_ref[RG_WIDTH + SSD_WIDTH:, :], preferred_element_type=F32)
        mixes.append(mix)
    hs = [h_in[j] + mods[j][2:3] * mixes[j] for j in range(nb)]
    ys = [(_rms(hs[j], g2) * (1.0 + mods[j][4:5]) + mods[j][3:4]).astype(BF16) for j in range(nb)]
    gates, ups = [], []
    for y in ys:
        gates.append(jnp.dot(y, wg_ref[...], preferred_element_type=F32))
        ups.append(jnp.dot(y, wu_ref[...], preferred_element_type=F32))
    acts = [(_silu(gates[j]) * ups[j]).astype(BF16) for j in range(nb)]
    downs = [jnp.dot(a, wd_ref[...], preferred_element_type=F32) for a in acts]
    for j in range(nb):
        h = hs[j] + mods[j][5:6] * downs[j]
        o_ref[j] = _rms(h, gf_ref[...]) if final else h


def _out_ffn(hs, rg, ssd, da, mod, norm2_g, w_out, w_gate, w_up, w_down, final_g, final):
    b, l, _ = rg.shape
    nb = FFN_BATCH
    assert b % nb == 0
    t0 = CTX_LEN // TOK_TILE if final else 0
    nt = l // TOK_TILE - t0
    tok = lambda w: pl.BlockSpec((nb, TOK_TILE, w), lambda t, i: (i, t + t0, 0))
    out_rows = nt * TOK_TILE
    kwargs = {"input_output_aliases": {0: 0}} if (len(hs) == 1 and not final) else {}
    return pl.pallas_call(
        functools.partial(_out_ffn_kernel, final=final),
        grid=(nt, b // nb),
        in_specs=_stream_specs(hs, b, t0, nb) + [
            tok(RG_WIDTH), tok(SSD_WIDTH), tok(DA_WIDTH),
            pl.BlockSpec((nb, 6, D_MODEL), lambda t, i: (jnp.where(t + t0 == 0, b // nb, i), 0, 0)),
            _const_spec((1, D_MODEL)),
            _const_spec((D_MODEL, D_MODEL)),
            _const_spec((D_MODEL, D_FF)),
            _const_spec((D_MODEL, D_FF)),
            _const_spec((D_FF, D_MODEL)),
            _const_spec((1, D_MODEL)),
        ],
        out_specs=pl.BlockSpec((nb, TOK_TILE, D_MODEL), lambda t, i: (i, t, 0)),
        out_shape=jax.ShapeDtypeStruct((b, out_rows, D_MODEL), F32),
        compiler_params=_params("arbitrary", "arbitrary"),
        **kwargs,
    )(*hs, rg, ssd, da, mod, norm2_g, w_out, w_gate, w_up, w_down, final_g)


def _rope_tables(n_lat):
    half = DA_HEAD_DIM // 2
    inv_freq = jnp.power(ROPE_BASE, -jnp.arange(0, half, 2, dtype=F32) / half)
    t = jnp.arange(n_lat, dtype=jnp.int32)
    ang_r = (t // GRID_W).astype(F32)[:, None] * inv_freq
    ang_c = (t % GRID_W).astype(F32)[:, None] * inv_freq
    cos = jnp.concatenate([jnp.cos(ang_r), jnp.cos(ang_r), jnp.cos(ang_c), jnp.cos(ang_c)], axis=1)
    sin = jnp.concatenate([-jnp.sin(ang_r), jnp.sin(ang_r), -jnp.sin(ang_c), jnp.sin(ang_c)], axis=1)
    reps = DA_WIDTH // DA_HEAD_DIM
    cos = jnp.concatenate([jnp.ones((CTX_LEN, DA_HEAD_DIM), F32), cos], axis=0)
    sin = jnp.concatenate([jnp.zeros((CTX_LEN, DA_HEAD_DIM), F32), sin], axis=0)
    cos = jnp.tile(cos, (1, reps))
    sin = jnp.tile(sin, (1, reps))
    scale = DA_HEAD_DIM ** -0.5 * math.log2(math.e)
    return cos * scale, sin * scale, cos, sin


def _split_w_in(w):
    offs = [0]
    for size in IN_SIZES:
        offs.append(offs[-1] + size)
    w_qkv = w[:, :, offs[5]:offs[8]]
    w_rg = w[:, :, offs[0]:offs[3]]
    pad = jnp.zeros(w.shape[:2] + (LANES - IN_SIZES[4],), w.dtype)
    w_ssd = jnp.concatenate([w[:, :, offs[3]:offs[5]], pad], axis=2)
    return w_qkv.astype(BF16), w_rg.astype(BF16), w_ssd.astype(BF16)


def _block_diag(w):
    eye = jnp.eye(RG_HEADS, dtype=w.dtype)
    return jnp.einsum("hij,hg->higj", w, eye).reshape(RG_WIDTH, RG_WIDTH)


def _lane_pad(v):
    return jnp.concatenate([v, jnp.zeros((LANES - v.shape[0],), v.dtype)])[None, :]


def kernel(x, c, ctx, c_ctx, w_mod, b_mod, norm1_g, w_in, rg_conv_w, rg_conv_b, rg_w_a, rg_b_a, rg_w_x, rg_b_x, rg_lambda, ssd_conv_w, ssd_conv_b, ssd_dt_bias, ssd_a_log, ssd_d, ssd_norm_g, da_lambda, da_subln_g, w_out, norm2_g, w_gate, w_up, w_down, final_norm_g):
    b, n_lat, _ = x.shape
    assert ctx.shape[1] == CTX_LEN and n_lat % TOK_TILE == 0
    n_ctx = max(INPROJ_BATCH, FFN_BATCH)
    rows = -(-(b + n_ctx) // SUBLANES) * SUBLANES
    c_all = jnp.concatenate([c] + [c_ctx[None, :]] * n_ctx
                            + [jnp.zeros((rows - b - n_ctx, D_MODEL), F32)], axis=0)
    mods = _modulation(c_all, w_mod, b_mod)
    tables = _rope_tables(n_lat)
    hs = (ctx, x)
    w_qkv16, w_rg16, w_ssd16 = _split_w_in(w_in)
    out = None
    for l in range(DEPTH):
        final = l == DEPTH - 1
        y, q, k, v = _in_proj(hs, mods[l], norm1_g[l][None, :], w_qkv16, l, tables)
        w_gates = (0.5 * jnp.concatenate(
            [_block_diag(w[l, d]) for d in range(2) for w in (rg_w_a, rg_w_x)], axis=1)).astype(BF16)
        b_gates = 0.5 * jnp.concatenate(
            [bb[l, d] for d in range(2) for bb in (rg_b_a, rg_b_x)])[None, :]
        y_rg, u_ssd = _rglru(y, w_rg16, w_ssd16, l, rg_conv_w[l], rg_conv_b[l][None, :], w_gates, b_gates, rg_lambda[l])
        y_ssd = _ssd(u_ssd, ssd_conv_w[l], ssd_conv_b[l][None, :],
                     _lane_pad(ssd_dt_bias[l].reshape(-1)),
                     _lane_pad(-jnp.exp(ssd_a_log[l].reshape(-1))),
                     jnp.repeat(ssd_d[l], SSD_HEADDIM)[None, :], ssd_norm_g[l][None, :])
        y_da, ffn16 = _attention(q, k, v, da_lambda[l], da_subln_g[l][None, :], l,
                                 (w_out, w_gate, w_up, w_down))
        res = _out_ffn(hs, y_rg, y_ssd, y_da, mods[l], norm2_g[l][None, :], *ffn16,
                       final_norm_g[None, :], final)
        if final:
            out = res
        else:
            hs = (res,)
    return out
```

```python
import functools
import math

import jax
import jax.numpy as jnp
from jax import lax
from jax.experimental import pallas as pl
from jax.experimental.pallas import tpu as pltpu

F32 = jnp.float32
BF16 = jnp.bfloat16

D_MODEL = 1024
DEPTH = 2
CTX_LEN = 256
GRID_W = 64
NORM_EPS = 1e-6

RG_WIDTH = 256
RG_HEADS = 4
RG_BLOCK = 64
RG_C = 8.0

SSD_WIDTH = 256
SSD_HEADDIM = 64
SSD_HEADS = 4
SSD_GROUPS = 2
SSD_STATE = 64
SSD_CHUNK = 128
SSD_XBC = 512

DA_WIDTH = 512
DA_HEADS = 4
DA_HEAD_DIM = 64
DA_V_DIM = 128
ROPE_BASE = 10000.0

D_FF = 2816
IN_SIZES = (256, 256, 256, 512, 8, 512, 512, 512)
D_IN = sum(IN_SIZES)

LANES = 128
SUBLANES = 8
MXU_COLS = 256
TOK_TILE = 256
INPROJ_BATCH = 4
FFN_BATCH = 2
CONV_PAD = SUBLANES
VMEM_LIMIT = 56 * 1024 * 1024

QKV_COLS = 3 * DA_WIDTH
RGP_COLS = 3 * RG_WIDTH
S_XBC = 0
S_DT = 512
S_Z = 640
SSDP_COLS = 896
PROJ_ROWS = 256
PROJ_AHEAD = 2


def _silu(x):
    hx = 0.5 * x
    return hx * jnp.tanh(hx) + hx


def _softplus(x):
    return jnp.maximum(x, 0.0) + jnp.log(1.0 + jnp.exp(-jnp.abs(x)))


def _rms(x, g):
    ms = jnp.mean(x * x, axis=-1, keepdims=True)
    return x * lax.rsqrt(ms + NORM_EPS) * g


def _const_spec(shape, layer=None):
    nd = len(shape)
    if layer is None:
        return pl.BlockSpec(shape, lambda *_: (0,) * nd, pipeline_mode=pl.Buffered(1))
    return pl.BlockSpec((None,) + tuple(shape), lambda *_: (layer,) + (0,) * nd, pipeline_mode=pl.Buffered(1))


def _params(*sem):
    return pltpu.CompilerParams(dimension_semantics=sem, vmem_limit_bytes=VMEM_LIMIT)


def _mod_kernel(c_ref, w_ref, b_ref, o_ref):
    c = c_ref[...]
    a = _silu(c).astype(BF16)
    o_ref[...] = jnp.dot(a, w_ref[...].astype(BF16), preferred_element_type=F32) + b_ref[...]


def _modulation(c_all, w_mod, b_mod):
    r = c_all.shape[0]
    out = pl.pallas_call(
        _mod_kernel,
        grid=(DEPTH, 6),
        in_specs=[
            pl.BlockSpec((r, D_MODEL), lambda l, j: (0, 0)),
            pl.BlockSpec((None, D_MODEL, D_MODEL), lambda l, j: (l, 0, j)),
            pl.BlockSpec((None, 1, D_MODEL), lambda l, j: (l, 0, j)),
        ],
        out_specs=pl.BlockSpec((None, r, D_MODEL), lambda l, j: (l, 0, j)),
        out_shape=jax.ShapeDtypeStruct((DEPTH, r, 6 * D_MODEL), F32),
        compiler_params=_params("arbitrary", "arbitrary"),
    )(c_all, w_mod, b_mod.reshape(DEPTH, 1, 6 * D_MODEL))
    return out.reshape(DEPTH, r, 6, D_MODEL)


def _rope(x, cos, sin, first_half):
    width = x.shape[-1]
    partner = jnp.where(first_half, pltpu.roll(x, width - 16, 1), pltpu.roll(x, 16, 1))
    return x * cos + partner * sin


def _load_stream(h_refs):
    if len(h_refs) == 1:
        return h_refs[0][...]
    return jnp.where(pl.program_id(0) == 0, h_refs[0][...], h_refs[1][...])


def _stream_specs(hs, b, t0=0, nb=None):
    lead = nb
    if len(hs) == 1:
        return [pl.BlockSpec((lead, TOK_TILE, D_MODEL), lambda t, i: (i, t + t0, 0))]
    assert t0 == 0
    return [pl.BlockSpec((lead, TOK_TILE, D_MODEL), lambda t, i: (jnp.where(t == 0, i, 0), 0, 0)),
            pl.BlockSpec((lead, TOK_TILE, D_MODEL), lambda t, i: (i, jnp.maximum(t - 1, 0), 0))]


def _inproj_kernel(*refs):
    mod_ref, g_ref, w_ref, cq_ref, sq_ref, ck_ref, sk_ref, y_ref, q_ref, k_ref, v_ref = refs[-11:]
    h = _load_stream(refs[:-11])
    g = g_ref[...]
    ys = []
    for j in range(INPROJ_BATCH):
        mod = mod_ref[j]
        ys.append((_rms(h[j], g) * (1.0 + mod[1:2]) + mod[0:1]).astype(BF16))
    us = [jnp.dot(y, w_ref[...], preferred_element_type=F32) for y in ys]
    lane = lax.broadcasted_iota(jnp.int32, (TOK_TILE, DA_WIDTH), 1)
    first_half = (lane % 32) < 16
    for j, u in enumerate(us):
        y_ref[j] = ys[j]
        q_ref[j] = _rope(u[:, 0:DA_WIDTH], cq_ref[...], sq_ref[...], first_half).astype(BF16)
        k_ref[j] = _rope(u[:, DA_WIDTH:2 * DA_WIDTH], ck_ref[...], sk_ref[...], first_half).astype(BF16)
        v_ref[j] = u[:, 2 * DA_WIDTH:QKV_COLS].astype(BF16)


def _in_proj(hs, mod, norm_g, w_qkv, layer, tables):
    b = hs[0].shape[0]
    l = sum(h.shape[1] for h in hs)
    nt = l // TOK_TILE
    nb = INPROJ_BATCH
    assert b % nb == 0
    tok = lambda w: pl.BlockSpec((nb, TOK_TILE, w), lambda t, i: (i, t, 0))
    tab = pl.BlockSpec((TOK_TILE, DA_WIDTH), lambda t, i: (t, 0))
    widths = (D_MODEL, DA_WIDTH, DA_WIDTH, DA_WIDTH)
    return pl.pallas_call(
        _inproj_kernel,
        grid=(nt, b // nb),
        in_specs=_stream_specs(hs, b, nb=nb) + [
            pl.BlockSpec((nb, 6, D_MODEL), lambda t, i: (jnp.where(t == 0, b // nb, i), 0, 0)),
            _const_spec((1, D_MODEL)),
            _const_spec((D_MODEL, QKV_COLS), layer),
            tab, tab, tab, tab,
        ],
        out_specs=[tok(w) for w in widths],
        out_shape=[jax.ShapeDtypeStruct((b, l, w), BF16) for w in widths],
        compiler_params=_params("arbitrary", "arbitrary"),
    )(*hs, mod, norm_g, w_qkv, *tables)


def _project_rows(y_ref, w_ref, u_ref, i):
    r0 = i * PROJ_ROWS
    rows = pl.ds(r0 if isinstance(r0, int) else pl.multiple_of(r0, PROJ_ROWS), PROJ_ROWS)
    y = y_ref[rows, :]
    ncols = w_ref.shape[1]
    half = ((ncols // MXU_COLS + 1) // 2) * MXU_COLS
    for cols in (slice(0, half), slice(half, ncols)):
        u_ref[rows, cols] = jnp.dot(y, w_ref[:, cols], preferred_element_type=F32)


def _conv_rows(x_ref, col0, width, r0, rows, w, bias):
    l = x_ref.shape[0]
    assert CTX_LEN % rows == 0
    span = rows + 2 * CONV_PAD
    lo = pl.multiple_of(jnp.maximum(r0 - CONV_PAD, 0), CONV_PAD)
    hi = pl.multiple_of(jnp.minimum(r0 + rows, l - CONV_PAD), CONV_PAD)
    cols = slice(col0, col0 + width)
    seg_start = jnp.logical_or(r0 == 0, r0 == CTX_LEN)
    seg_end = jnp.logical_or(r0 + rows == CTX_LEN, r0 + rows == l)
    xa = jnp.concatenate([jnp.where(seg_start, 0.0, x_ref[pl.ds(lo, CONV_PAD), cols]),
                          x_ref[pl.ds(r0, rows), cols],
                          jnp.where(seg_end, 0.0, x_ref[pl.ds(hi, CONV_PAD), cols])], axis=0)
    acc = None
    for tap in range(4):
        off = tap - 2
        sh = xa if off == 0 else pltpu.roll(xa, (-off) % span, 0)
        term = sh[CONV_PAD:CONV_PAD + rows] * w[tap:tap + 1]
        acc = term if acc is None else acc + term
    return acc + bias


RG_ROWS = 256


def _scan8(a, bx, reverse):
    row = lax.broadcasted_iota(jnp.int32, a.shape, 0)
    for s in (1, 2, 4):
        shift = (SUBLANES - s) if reverse else s
        a_sh = pltpu.roll(a, shift, 0)
        b_sh = pltpu.roll(bx, shift, 0)
        ok = (row < SUBLANES - s) if reverse else (row >= s)
        bx = jnp.where(ok, a * b_sh + bx, bx)
        a = jnp.where(ok, a * a_sh, a)
    return a, bx


def _rglru_kernel(y_ref, wp_ref, ws_ref, cw_ref, cb_ref, wg_ref, bg_ref, lam_ref, o_ref, us_ref,
                  rg_ref, af_ref, bf_ref, ab_ref, bb_ref):
    l = y_ref.shape[0]
    nproj = l // PROJ_ROWS
    assert RG_ROWS == PROJ_ROWS
    for i in range(PROJ_AHEAD):
        _project_rows(y_ref, wp_ref, rg_ref, i)
    cw = cw_ref[...]
    cb = cb_ref[...]
    bg = bg_ref[...]
    coef = (-0.5 * RG_C) * _softplus(-lam_ref[...])

    def coeffs(i, carry):
        r0 = i * RG_ROWS
        xc = _conv_rows(rg_ref, 0, RG_WIDTH, r0, RG_ROWS, cw, cb)
        t = jnp.tanh(jnp.dot(xc.astype(BF16), wg_ref[...], preferred_element_type=F32) + bg)
        hx = 0.5 * xc
        for d, (a_ref, b_ref) in enumerate(((af_ref, bf_ref), (ab_ref, bb_ref))):
            t_a = t[:, (2 * d) * RG_WIDTH:(2 * d + 1) * RG_WIDTH]
            t_x = t[:, (2 * d + 1) * RG_WIDTH:(2 * d + 2) * RG_WIDTH]
            log_a = coef[d:d + 1] * t_a + coef[d:d + 1]
            a = jnp.exp(log_a)
            one_m_a2 = jnp.tanh(log_a) * (-1.0 - a * a)
            a_ref[pl.ds(r0, RG_ROWS), :] = a
            b_ref[pl.ds(r0, RG_ROWS), :] = jnp.sqrt(one_m_a2) * (hx * t_x + hx)
        if i + PROJ_AHEAD < nproj:
            _project_rows(y_ref, wp_ref, rg_ref, i + PROJ_AHEAD)
        return carry

    for i in range(nproj):
        coeffs(i, 0)

    nblk = l // SUBLANES
    nctx = CTX_LEN // SUBLANES

    def scan_block(j, carry):
        hf, hb = carry
        rf = pl.multiple_of(j * SUBLANES, SUBLANES)
        jb = jnp.where(j < nctx, nctx - 1 - j, nblk + nctx - 1 - j)
        rb = pl.multiple_of(jb * SUBLANES, SUBLANES)
        a, bx = _scan8(af_ref[pl.ds(rf, SUBLANES), :], bf_ref[pl.ds(rf, SUBLANES), :], False)
        h = a * hf + bx
        af_ref[pl.ds(rf, SUBLANES), :] = h
        hf = h[SUBLANES - 1:SUBLANES]
        a, bx = _scan8(ab_ref[pl.ds(rb, SUBLANES), :], bb_ref[pl.ds(rb, SUBLANES), :], True)
        h = a * hb + bx
        ab_ref[pl.ds(rb, SUBLANES), :] = h
        hb = h[0:1]
        return hf, hb

    per_step = PROJ_ROWS // SUBLANES

    def scan(i, carry):
        _project_rows(y_ref, ws_ref, us_ref, i)
        for jj in range(per_step):
            carry = scan_block(i * per_step + jj, carry)
        return carry

    zero = jnp.zeros((1, RG_WIDTH), F32)
    lax.fori_loop(0, nproj, scan, (zero, zero))

    def finish(i, carry):
        r0 = pl.multiple_of(i * RG_ROWS, RG_ROWS)
        hsum = af_ref[pl.ds(r0, RG_ROWS), :] + ab_ref[pl.ds(r0, RG_ROWS), :]
        g = rg_ref[pl.ds(r0, RG_ROWS), RG_WIDTH:2 * RG_WIDTH]
        o_ref[pl.ds(r0, RG_ROWS), :] = (hsum * jax.nn.gelu(g, approximate=True)).astype(BF16)
        us_ref[pl.ds(r0, RG_ROWS), S_Z:SSDP_COLS] = rg_ref[pl.ds(r0, RG_ROWS), 2 * RG_WIDTH:RGP_COLS]
        return carry

    lax.fori_loop(0, l // RG_ROWS, finish, 0)


def _rglru(y, w_proj, w_ssd, layer, conv_w, conv_b, w_gates, b_gates, lam):
    b, l, _ = y.shape
    seq = lambda w: pl.BlockSpec((None, l, w), lambda i: (i, 0, 0))
    scratch = [pltpu.VMEM((l, RGP_COLS), F32)] + [pltpu.VMEM((l, RG_WIDTH), F32)] * 4
    return pl.pallas_call(
        _rglru_kernel,
        grid=(b,),
        in_specs=[
            seq(D_MODEL),
            _const_spec((D_MODEL, RGP_COLS), layer),
            _const_spec((D_MODEL, S_Z), layer),
            _const_spec((4, RG_WIDTH)),
            _const_spec((1, RG_WIDTH)),
            _const_spec((RG_WIDTH, 4 * RG_WIDTH)),
            _const_spec((1, 4 * RG_WIDTH)),
            _const_spec((2, RG_WIDTH)),
        ],
        out_specs=[seq(RG_WIDTH), seq(SSDP_COLS)],
        out_shape=[jax.ShapeDtypeStruct((b, l, RG_WIDTH), BF16), jax.ShapeDtypeStruct((b, l, SSDP_COLS), F32)],
        scratch_shapes=scratch,
        compiler_params=_params("arbitrary"),
    )(y, w_proj, w_ssd, conv_w, conv_b, w_gates, b_gates, lam)


SSD_ROWS = 64
GROUP_W = SSD_WIDTH // SSD_GROUPS
HEADS_PER_GROUP = SSD_HEADS // SSD_GROUPS
STATE_ROWS = SSD_GROUPS * SSD_STATE
NT_DIMS = (((1,), (1,)), ((), ()))
SSD_GROUP_CHUNKS = 9


def _ssd_head_lane(direction, group, j):
    return direction * SSD_HEADS + group * HEADS_PER_GROUP + j


def _ssd_kernel(u_ref, cw_ref, cb_ref, dtb_ref, aneg_ref, dsk_ref, ng_ref, o_ref,
                xbc_ref, dts_ref, a3_ref, y_ref, ecc_ref, ds_ref, sent_ref, etot_ref):
    l = u_ref.shape[0]
    q = SSD_CHUNK
    assert HEADS_PER_GROUP == 2 and GROUP_W == LANES and STATE_ROWS == LANES
    cw = cw_ref[...]
    cb = cb_ref[...]
    dtb = dtb_ref[...]
    dsk = dsk_ref[...]
    aneg = aneg_ref[...] * math.log2(math.e)

    def prep(i, carry):
        r0 = pl.multiple_of(i * SSD_ROWS, SSD_ROWS)
        rows = pl.ds(r0, SSD_ROWS)
        xbc = _silu(_conv_rows(u_ref, S_XBC, SSD_XBC, r0, SSD_ROWS, cw, cb))
        xbc_ref[rows, :] = xbc
        dt = _softplus(u_ref[rows, S_DT:S_Z] + dtb)
        dts_ref[rows, :] = dt
        y_ref[rows, :] = xbc[:, 0:SSD_WIDTH] * dsk
        a = dt * aneg
        hi = a.astype(BF16).astype(F32)
        r1 = a - hi
        mid = r1.astype(BF16).astype(F32)
        lo = r1 - mid
        a3_ref[rows, :] = (hi + pltpu.roll(mid, 2 * SSD_HEADS, 1) + pltpu.roll(lo, 4 * SSD_HEADS, 1)).astype(BF16)
        return carry

    lax.fori_loop(0, l // SSD_ROWS, prep, 0, unroll=9)

    ri = lax.broadcasted_iota(jnp.int32, (q, q), 0)
    ci = lax.broadcasted_iota(jnp.int32, (q, q), 1)
    lower = ri >= ci
    upper = ci >= ri
    tri_both = jnp.concatenate([lower.astype(F32), upper.astype(F32)], axis=0).astype(BF16)
    eye16 = (ri == ci).astype(F32).astype(BF16)
    eye_rows = (lax.broadcasted_iota(jnp.int32, (2 * SUBLANES, LANES), 0)
                == lax.broadcasted_iota(jnp.int32, (2 * SUBLANES, LANES), 1)).astype(F32)
    lane = lax.broadcasted_iota(jnp.int32, (q, LANES), 1)
    left = lane < SSD_HEADDIM
    lane_row = lax.broadcasted_iota(jnp.int32, (1, LANES), 1)
    left_row = lane_row < SSD_HEADDIM
    sub8 = lax.broadcasted_iota(jnp.int32, (SUBLANES, q), 0)
    left_state = lax.broadcasted_iota(jnp.int32, (SSD_STATE, GROUP_W), 1) < SSD_HEADDIM
    nchunk = l // q
    nctx = CTX_LEN // q

    def local(cg, carry):
        chunks = [cg * SSD_GROUP_CHUNKS + k for k in range(SSD_GROUP_CHUNKS)]
        rows = [pl.ds(pl.multiple_of(c * q, q), q) for c in chunks]

        def fold(cs):
            return cs + pltpu.roll(cs, LANES - 2 * SSD_HEADS, 1) + pltpu.roll(cs, LANES - 4 * SSD_HEADS, 1)

        x16, bm_t, grams, cs = [], [], [], []
        for r in rows:
            x16.append(xbc_ref[r, 0:SSD_WIDTH].astype(BF16))
            bm16 = xbc_ref[r, SSD_WIDTH:SSD_WIDTH + LANES].astype(BF16)
            cm = xbc_ref[r, SSD_WIDTH + LANES:SSD_XBC]
            cm_g = [jnp.where((lane < SSD_STATE) == (g == 0), cm, 0.0).astype(BF16) for g in range(SSD_GROUPS)]
            nt = lax.dot_general(jnp.concatenate(cm_g + [eye16], axis=0), bm16, NT_DIMS,
                                 preferred_element_type=F32)
            grams.append([nt[g * q:(g + 1) * q] for g in range(SSD_GROUPS)])
            bm_t.append(nt[SSD_GROUPS * q:])
            cs.append(jnp.dot(tri_both, a3_ref[r, :], preferred_element_type=F32))

        c_col, c_row, dt_row = [], [], []
        for k, r in enumerate(rows):
            cc = jnp.where(lane < SSD_HEADS, fold(cs[k][0:q]), fold(cs[k][q:2 * q]))
            c_col.append(cc)
            both = jnp.where(lane < 2 * SSD_HEADS, cc, pltpu.roll(dts_ref[r, :], 2 * SSD_HEADS, 1))
            rows_t = lax.dot_general(eye_rows, both, NT_DIMS, precision=lax.Precision.HIGHEST,
                                     preferred_element_type=F32)
            c_row.append(rows_t[0:SUBLANES])
            dt_row.append(rows_t[SUBLANES:2 * SUBLANES])

        zblock = jnp.zeros((SSD_STATE, GROUP_W), F32)
        for k, (c, r) in enumerate(zip(chunks, rows)):
            tot_col = jnp.where(sub8 < SSD_HEADS, c_row[k][:, q - 1:q], c_row[k][:, 0:1])
            dtw_row = dt_row[k] * jnp.exp2(tot_col - c_row[k])
            tot_row = jnp.where(lane_row < SSD_HEADS, c_col[k][q - 1:q, :], c_col[k][0:1, :])
            e_tot = jnp.exp2(tot_row)
            ds = [[None] * SSD_GROUPS for _ in range(2)]
            etot = [[None] * SSD_GROUPS for _ in range(2)]
            for g in range(SSD_GROUPS):
                cols = slice(g * GROUP_W, (g + 1) * GROUP_W)
                lhs_m, lhs_b, ecc = [], [], []
                for d in range(2):
                    mask = lower if d == 0 else upper
                    for j in range(HEADS_PER_GROUP):
                        li = _ssd_head_lane(d, g, j)
                        ccb = jnp.broadcast_to(c_col[k][:, li:li + 1], (q, q))
                        decay = jnp.exp2(jnp.where(mask, ccb - c_row[k][li:li + 1, :], -1e30))
                        lhs_m.append((grams[k][g] * decay * dt_row[k][li:li + 1, :]).astype(BF16))
                        lhs_b.append((bm_t[k][g * SSD_STATE:(g + 1) * SSD_STATE, :]
                                      * dtw_row[li:li + 1, :]).astype(BF16))
                        ecc.append(jnp.exp2(ccb))
                prod = jnp.dot(jnp.concatenate(lhs_m + lhs_b, axis=0), x16[k][:, cols],
                               preferred_element_type=F32)
                nm = 2 * HEADS_PER_GROUP * q
                for d in range(2):
                    a, b = (2 * d) * q, (2 * d + 1) * q
                    y_ref[r, cols] += jnp.where(left, prod[a:a + q], prod[b:b + q])
                    ecc_ref[d, r, cols] = jnp.where(left, ecc[2 * d], ecc[2 * d + 1])
                    a, b = nm + (2 * d) * SSD_STATE, nm + (2 * d + 1) * SSD_STATE
                    dsg = jnp.where(left_state, prod[a:a + SSD_STATE], prod[b:b + SSD_STATE])
                    ds[d][g] = jnp.concatenate([dsg, zblock] if g == 0 else [zblock, dsg], axis=1)
                    la = _ssd_head_lane(d, g, 0)
                    lb = _ssd_head_lane(d, g, 1)
                    etot[d][g] = jnp.where(left_row, e_tot[:, la:la + 1], e_tot[:, lb:lb + 1])
            for d in range(2):
                ds_ref[d, pl.ds(pl.multiple_of(c * STATE_ROWS, STATE_ROWS), STATE_ROWS), :] = (
                    jnp.concatenate(ds[d], axis=0))
                etot_ref[d, pl.ds(pl.multiple_of(c * SUBLANES, SUBLANES), SUBLANES), :] = (
                    jnp.broadcast_to(jnp.concatenate(etot[d], axis=1), (SUBLANES, SSD_WIDTH)))
        return carry

    assert nchunk % SSD_GROUP_CHUNKS == 0
    lax.fori_loop(0, nchunk // SSD_GROUP_CHUNKS, local, 0)

    def carry_state(j, state):
        sf, sb = state
        jb = jnp.where(j < nctx, nctx - 1 - j, nchunk + nctx - 1 - j)
        out = []
        for d, (c, s) in enumerate(((j, sf), (jb, sb))):
            srows = pl.ds(pl.multiple_of(c * STATE_ROWS, STATE_ROWS), STATE_ROWS)
            sent_ref[d, srows, :] = s.astype(BF16)
            e = etot_ref[d, pl.ds(pl.multiple_of(c * SUBLANES, SUBLANES), 1), :]
            out.append(e * s + ds_ref[d, srows, :])
        return tuple(out)

    zero = jnp.zeros((STATE_ROWS, SSD_WIDTH), F32)
    lax.fori_loop(0, nchunk, carry_state, (zero, zero))

    ng = ng_ref[...]

    def finish(c, carry):
        r0 = pl.multiple_of(c * q, q)
        rows = pl.ds(r0, q)
        srows = pl.ds(pl.multiple_of(c * STATE_ROWS, STATE_ROWS), STATE_ROWS)
        cm16 = xbc_ref[rows, SSD_WIDTH + LANES:SSD_XBC].astype(BF16)
        y = y_ref[rows, :]
        for d in range(2):
            y = y + ecc_ref[d, rows, :] * jnp.dot(cm16, sent_ref[d, srows, :], preferred_element_type=F32)
        v = y * _silu(u_ref[rows, S_Z:SSDP_COLS])
        parts = []
        for g in range(SSD_GROUPS):
            vg = v[:, g * GROUP_W:(g + 1) * GROUP_W]
            parts.append(vg * lax.rsqrt(jnp.mean(vg * vg, axis=-1, keepdims=True) + NORM_EPS))
        o_ref[rows, :] = (jnp.concatenate(parts, axis=1) * ng).astype(BF16)
        return carry

    lax.fori_loop(0, nchunk, finish, 0, unroll=6)


def _ssd(u, conv_w, conv_b, dt_bias, aneg, dskip, norm_g):
    b, l, _ = u.shape
    nchunk = l // SSD_CHUNK
    seq = lambda w: pl.BlockSpec((None, l, w), lambda i: (i, 0, 0))
    scratch = [
        pltpu.VMEM((l, SSD_XBC), F32),
        pltpu.VMEM((l, LANES), F32),
        pltpu.VMEM((l, LANES), BF16),
        pltpu.VMEM((l, SSD_WIDTH), F32),
        pltpu.VMEM((2, l, SSD_WIDTH), F32),
        pltpu.VMEM((2, nchunk * STATE_ROWS, SSD_WIDTH), F32),
        pltpu.VMEM((2, nchunk * STATE_ROWS, SSD_WIDTH), BF16),
        pltpu.VMEM((2, nchunk * SUBLANES, SSD_WIDTH), F32),
    ]
    return pl.pallas_call(
        _ssd_kernel,
        grid=(b,),
        in_specs=[
            seq(SSDP_COLS),
            _const_spec((4, SSD_XBC)),
            _const_spec((1, SSD_XBC)),
            _const_spec((1, LANES)),
            _const_spec((1, LANES)),
            _const_spec((1, SSD_WIDTH)),
            _const_spec((1, SSD_WIDTH)),
        ],
        out_specs=seq(SSD_WIDTH),
        out_shape=jax.ShapeDtypeStruct((b, l, SSD_WIDTH), BF16),
        scratch_shapes=scratch,
        compiler_params=_params("arbitrary"),
    )(u, conv_w, conv_b, dt_bias, aneg, dskip, norm_g)


def _attn_probs(q, k_ref, p_ref, nkeys):
    l = k_ref.shape[0]
    for c in range(2):
        qc = q[:, c * DA_HEAD_DIM:(c + 1) * DA_HEAD_DIM]
        kc = k_ref[0:nkeys, c * DA_HEAD_DIM:(c + 1) * DA_HEAD_DIM]
        s = lax.dot_general(qc, kc, (((1,), (1,)), ((), ())), preferred_element_type=F32)
        p_ref[:, c * l:c * l + nkeys] = jnp.exp2(s - jnp.max(s, axis=-1, keepdims=True)).astype(BF16)


def _attn_values(p_ref, vaug_ref, nkeys, lam, g, lam_init):
    l = vaug_ref.shape[0]
    outs = []
    for c in range(2):
        ov = jnp.dot(p_ref[:, c * l:c * l + nkeys], vaug_ref[0:nkeys, :], preferred_element_type=F32)
        outs.append(ov[:, 0:DA_V_DIM] / ov[:, DA_V_DIM:DA_V_DIM + 1])
    o = outs[0] - lam * outs[1]
    return (_rms(o, g) * (1.0 - lam_init)).astype(BF16)


def _attn_kernel(q_ref, k_ref, v_ref, lam_ref, g_ref, *refs, lam_init):
    nw = len(FFN_WEIGHT_SHAPES)
    w_refs, o_ref, w16_refs = refs[:nw], refs[nw], refs[nw + 1:2 * nw + 1]
    vaug_ref, pa_ref, pb_ref = refs[2 * nw + 1:]

    @pl.when(pl.program_id(1) == 0)
    def _():
        for w_ref, w16_ref in zip(w_refs, w16_refs):
            w16_ref[...] = w_ref[...].astype(BF16)

    l = q_ref.shape[0]
    nt = l // TOK_TILE
    assert CTX_LEN == TOK_TILE and nt % 2 == 1 and nt >= 3
    lv = lam_ref[...]
    lam = (jnp.exp(jnp.sum(lv[0:1] * lv[1:2], axis=-1, keepdims=True))
           - jnp.exp(jnp.sum(lv[2:3] * lv[3:4], axis=-1, keepdims=True)) + lam_init)
    g = g_ref[...]
    vaug_ref[:, 0:DA_V_DIM] = v_ref[...]
    vaug_ref[:, DA_V_DIM:2 * DA_V_DIM] = jnp.ones((l, DA_V_DIM), BF16)

    def rows(t):
        return pl.ds(pl.multiple_of(t * TOK_TILE, TOK_TILE), TOK_TILE)

    def probs(t, p_ref, nkeys=l):
        _attn_probs(q_ref[rows(t), :], k_ref, p_ref, nkeys)

    def values(t, p_ref, nkeys=l):
        o_ref[rows(t), :] = _attn_values(p_ref, vaug_ref, nkeys, lam, g, lam_init)

    probs(0, pa_ref, CTX_LEN)
    probs(1, pb_ref)
    values(0, pa_ref, CTX_LEN)

    for t in range(2, nt - 1, 2):
        probs(t, pa_ref)
        values(t - 1, pb_ref)
        probs(t + 1, pb_ref)
        values(t, pa_ref)
    probs(nt - 1, pa_ref)
    values(nt - 2, pb_ref)
    values(nt - 1, pa_ref)


FFN_WEIGHT_SHAPES = ((D_MODEL, D_MODEL), (D_MODEL, D_FF), (D_MODEL, D_FF), (D_FF, D_MODEL))


def _attention(q, k, v, lam_vec, subln_g, layer_idx, ffn_weights):
    b, l, _ = q.shape
    lam_init = 0.8 - 0.6 * math.exp(-0.3 * layer_idx)
    seq = pl.BlockSpec((None, l, DA_V_DIM), lambda i, h: (i, 0, h))
    w_in_specs, w_out_specs, w_out_shapes = [], [], []
    for rows, cols in FFN_WEIGHT_SHAPES:
        assert rows % b == 0 and (rows // b) % (2 * SUBLANES) == 0
        w_in_specs.append(pl.BlockSpec((None, rows // b, cols), lambda i, h: (layer_idx, i, 0)))
        w_out_specs.append(pl.BlockSpec((rows // b, cols), lambda i, h: (i, 0)))
        w_out_shapes.append(jax.ShapeDtypeStruct((rows, cols), BF16))
    res = pl.pallas_call(
        functools.partial(_attn_kernel, lam_init=lam_init),
        grid=(b, DA_HEADS),
        in_specs=[seq, seq, seq, _const_spec((4, DA_HEAD_DIM)), _const_spec((1, DA_V_DIM))] + w_in_specs,
        out_specs=[seq] + w_out_specs,
        out_shape=[jax.ShapeDtypeStruct((b, l, DA_WIDTH), BF16)] + w_out_shapes,
        scratch_shapes=[pltpu.VMEM((l, 2 * DA_V_DIM), BF16)] + [pltpu.VMEM((TOK_TILE, 2 * l), BF16)] * 2,
        compiler_params=_params("arbitrary", "arbitrary"),
    )(q, k, v, lam_vec, subln_g, *ffn_weights)
    return res[0], res[1:]


def _out_ffn_kernel(*refs, final):
    rg_ref, ssd_ref, da_ref, mod_ref, g2_ref, wo_ref, wg_ref, wu_ref, wd_ref, gf_ref, o_ref = refs[-11:]
    h_in = _load_stream(refs[:-11])
    g2 = g2_ref[...]
    nb = FFN_BATCH
    mods = [mod_ref[j] for j in range(nb)]
    mixes = []
    for j in range(nb):
        mix = jnp.dot(rg_ref[j], wo_ref[0:RG_WIDTH, :], preferred_element_type=F32)
        mix += jnp.dot(ssd_ref[j], wo_ref[RG_WIDTH:RG_WIDTH + SSD_WIDTH, :], preferred_element_type=F32)
        mix += jnp.dot(da_ref[j], wo_ref[RG_WIDTH + SSD_WIDTH:, :], preferred_element_type=F32)
        mixes.append(mix)
    hs = [h_in[j] + mods[j][2:3] * mixes[j] for j in range(nb)]
    ys = [(_rms(hs[j], g2) * (1.0 + mods[j][4:5]) + mods[j][3:4]).astype(BF16) for j in range(nb)]
    gates, ups = [], []
    for y in ys:
        gates.append(jnp.dot(y, wg_ref[...], preferred_element_type=F32))
        ups.append(jnp.dot(y, wu_ref[...], preferred_element_type=F32))
    acts = [(_silu(gates[j]) * ups[j]).astype(BF16) for j in range(nb)]
    downs = [jnp.dot(a, wd_ref[...], preferred_element_type=F32) for a in acts]
    for j in range(nb):
        h = hs[j] + mods[j][5:6] * downs[j]
        o_ref[j] = _rms(h, gf_ref[...]) if final else h


def _out_ffn(hs, rg, ssd, da, mod, norm2_g, w_out, w_gate, w_up, w_down, final_g, final):
    b, l, _ = rg.shape
    nb = FFN_BATCH
    assert b % nb == 0
    t0 = CTX_LEN // TOK_TILE if final else 0
    nt = l // TOK_TILE - t0
    tok = lambda w: pl.BlockSpec((nb, TOK_TILE, w), lambda t, i: (i, t + t0, 0))
    out_rows = nt * TOK_TILE
    kwargs = {"input_output_aliases": {0: 0}} if (len(hs) == 1 and not final) else {}
    return pl.pallas_call(
        functools.partial(_out_ffn_kernel, final=final),
        grid=(nt, b // nb),
        in_specs=_stream_specs(hs, b, t0, nb) + [
            tok(RG_WIDTH), tok(SSD_WIDTH), tok(DA_WIDTH),
            pl.BlockSpec((nb, 6, D_MODEL), lambda t, i: (jnp.where(t + t0 == 0, b // nb, i), 0, 0)),
            _const_spec((1, D_MODEL)),
            _const_spec((D_MODEL, D_MODEL)),
            _const_spec((D_MODEL, D_FF)),
            _const_spec((D_MODEL, D_FF)),
            _const_spec((D_FF, D_MODEL)),
            _const_spec((1, D_MODEL)),
        ],
        out_specs=pl.BlockSpec((nb, TOK_TILE, D_MODEL), lambda t, i: (i, t, 0)),
        out_shape=jax.ShapeDtypeStruct((b, out_rows, D_MODEL), F32),
        compiler_params=_params("arbitrary", "arbitrary"),
        **kwargs,
    )(*hs, rg, ssd, da, mod, norm2_g, w_out, w_gate, w_up, w_down, final_g)


def _rope_tables(n_lat):
    half = DA_HEAD_DIM // 2
    inv_freq = jnp.power(ROPE_BASE, -jnp.arange(0, half, 2, dtype=F32) / half)
    t = jnp.arange(n_lat, dtype=jnp.int32)
    ang_r = (t // GRID_W).astype(F32)[:, None] * inv_freq
    ang_c = (t % GRID_W).astype(F32)[:, None] * inv_freq
    cos = jnp.concatenate([jnp.cos(ang_r), jnp.cos(ang_r), jnp.cos(ang_c), jnp.cos(ang_c)], axis=1)
    sin = jnp.concatenate([-jnp.sin(ang_r), jnp.sin(ang_r), -jnp.sin(ang_c), jnp.sin(ang_c)], axis=1)
    reps = DA_WIDTH // DA_HEAD_DIM
    cos = jnp.concatenate([jnp.ones((CTX_LEN, DA_HEAD_DIM), F32), cos], axis=0)
    sin = jnp.concatenate([jnp.zeros((CTX_LEN, DA_HEAD_DIM), F32), sin], axis=0)
    cos = jnp.tile(cos, (1, reps))
    sin = jnp.tile(sin, (1, reps))
    scale = DA_HEAD_DIM ** -0.5 * math.log2(math.e)
    return cos * scale, sin * scale, cos, sin


def _split_w_in(w):
    offs = [0]
    for size in IN_SIZES:
        offs.append(offs[-1] + size)
    w_qkv = w[:, :, offs[5]:offs[8]]
    w_rg = w[:, :, offs[0]:offs[3]]
    pad = jnp.zeros(w.shape[:2] + (LANES - IN_SIZES[4],), w.dtype)
    w_ssd = jnp.concatenate([w[:, :, offs[3]:offs[5]], pad], axis=2)
    return w_qkv.astype(BF16), w_rg.astype(BF16), w_ssd.astype(BF16)


def _block_diag(w):
    eye = jnp.eye(RG_HEADS, dtype=w.dtype)
    return jnp.einsum("hij,hg->higj", w, eye).reshape(RG_WIDTH, RG_WIDTH)


def _lane_pad(v):
    return jnp.concatenate([v, jnp.zeros((LANES - v.shape[0],), v.dtype)])[None, :]


def kernel(x, c, ctx, c_ctx, w_mod, b_mod, norm1_g, w_in, rg_conv_w, rg_conv_b, rg_w_a, rg_b_a, rg_w_x, rg_b_x, rg_lambda, ssd_conv_w, ssd_conv_b, ssd_dt_bias, ssd_a_log, ssd_d, ssd_norm_g, da_lambda, da_subln_g, w_out, norm2_g, w_gate, w_up, w_down, final_norm_g):
    b, n_lat, _ = x.shape
    assert ctx.shape[1] == CTX_LEN and n_lat % TOK_TILE == 0
    n_ctx = max(INPROJ_BATCH, FFN_BATCH)
    rows = -(-(b + n_ctx) // SUBLANES) * SUBLANES
    c_all = jnp.concatenate([c] + [c_ctx[None, :]] * n_ctx
                            + [jnp.zeros((rows - b - n_ctx, D_MODEL), F32)], axis=0)
    mods = _modulation(c_all, w_mod, b_mod)
    tables = _rope_tables(n_lat)
    hs = (ctx, x)
    w_qkv16, w_rg16, w_ssd16 = _split_w_in(w_in)
    out = None
    for l in range(DEPTH):
        final = l == DEPTH - 1
        y, q, k, v = _in_proj(hs, mods[l], norm1_g[l][None, :], w_qkv16, l, tables)
        w_gates = (0.5 * jnp.concatenate(
            [_block_diag(w[l, d]) for d in range(2) for w in (rg_w_a, rg_w_x)], axis=1)).astype(BF16)
        b_gates = 0.5 * jnp.concatenate(
            [bb[l, d] for d in range(2) for bb in (rg_b_a, rg_b_x)])[None, :]
        y_rg, u_ssd = _rglru(y, w_rg16, w_ssd16, l, rg_conv_w[l], rg_conv_b[l][None, :], w_gates, b_gates, rg_lambda[l])
        y_ssd = _ssd(u_ssd, ssd_conv_w[l], ssd_conv_b[l][None, :],
                     _lane_pad(ssd_dt_bias[l].reshape(-1)),
                     _lane_pad(-jnp.exp(ssd_a_log[l].reshape(-1))),
                     jnp.repeat(ssd_d[l], SSD_HEADDIM)[None, :], ssd_norm_g[l][None, :])
        y_da, ffn16 = _attention(q, k, v, da_lambda[l], da_subln_g[l][None, :], l,
                                 (w_out, w_gate, w_up, w_down))
        res = _out_ffn(hs, y_rg, y_ssd, y_da, mods[l], norm2_g[l][None, :], *ffn16,
                       final_norm_g[None, :], final)
        if final:
            out = res
        else:
            hs = (res,)
    return out
```

```python
import functools
import math

import jax
import jax.numpy as jnp
from jax import lax
from jax.experimental import pallas as pl
from jax.experimental.pallas import tpu as pltpu

F32 = jnp.float32
BF16 = jnp.bfloat16

D_MODEL = 1024
DEPTH = 2
CTX_LEN = 256
GRID_W = 64
NORM_EPS = 1e-6

RG_WIDTH = 256
RG_HEADS = 4
RG_BLOCK = 64
RG_C = 8.0

SSD_WIDTH = 256
SSD_HEADDIM = 64
SSD_HEADS = 4
SSD_GROUPS = 2
SSD_STATE = 64
SSD_CHUNK = 128
SSD_XBC = 512

DA_WIDTH = 512
DA_HEADS = 4
DA_HEAD_DIM = 64
DA_V_DIM = 128
ROPE_BASE = 10000.0

D_FF = 2816
IN_SIZES = (256, 256, 256, 512, 8, 512, 512, 512)
D_IN = sum(IN_SIZES)

LANES = 128
SUBLANES = 8
MXU_COLS = 256
TOK_TILE = 256
INPROJ_BATCH = 4
FFN_BATCH = 2
CONV_PAD = SUBLANES
VMEM_LIMIT = 56 * 1024 * 1024

QKV_COLS = 3 * DA_WIDTH
RGP_COLS = 3 * RG_WIDTH
S_XBC = 0
S_DT = 512
S_Z = 640
SSDP_COLS = 896
PROJ_ROWS = 256
PROJ_AHEAD = 2


def _silu(x):
    hx = 0.5 * x
    return hx * jnp.tanh(hx) + hx


def _softplus(x):
    return jnp.maximum(x, 0.0) + jnp.log(1.0 + jnp.exp(-jnp.abs(x)))


def _rms(x, g):
    ms = jnp.mean(x * x, axis=-1, keepdims=True)
    return x * lax.rsqrt(ms + NORM_EPS) * g


def _const_spec(shape, layer=None):
    nd = len(shape)
    if layer is None:
        return pl.BlockSpec(shape, lambda *_: (0,) * nd, pipeline_mode=pl.Buffered(1))
    return pl.BlockSpec((None,) + tuple(shape), lambda *_: (layer,) + (0,) * nd, pipeline_mode=pl.Buffered(1))


def _params(*sem):
    return pltpu.CompilerParams(dimension_semantics=sem, vmem_limit_bytes=VMEM_LIMIT)


def _mod_kernel(c_ref, w_ref, b_ref, o_ref):
    c = c_ref[...]
    a = _silu(c).astype(BF16)
    o_ref[...] = jnp.dot(a, w_ref[...].astype(BF16), preferred_element_type=F32) + b_ref[...]


def _modulation(c_all, w_mod, b_mod):
    r = c_all.shape[0]
    out = pl.pallas_call(
        _mod_kernel,
        grid=(DEPTH, 6),
        in_specs=[
            pl.BlockSpec((r, D_MODEL), lambda l, j: (0, 0)),
            pl.BlockSpec((None, D_MODEL, D_MODEL), lambda l, j: (l, 0, j)),
            pl.BlockSpec((None, 1, D_MODEL), lambda l, j: (l, 0, j)),
        ],
        out_specs=pl.BlockSpec((None, r, D_MODEL), lambda l, j: (l, 0, j)),
        out_shape=jax.ShapeDtypeStruct((DEPTH, r, 6 * D_MODEL), F32),
        compiler_params=_params("arbitrary", "arbitrary"),
    )(c_all, w_mod, b_mod.reshape(DEPTH, 1, 6 * D_MODEL))
    return out.reshape(DEPTH, r, 6, D_MODEL)


def _rope(x, cos, sin, first_half):
    out = []
    for blk in range(x.shape[-1] // LANES):
        xb = x[:, blk * LANES:(blk + 1) * LANES]
        partner = jnp.where(first_half, pltpu.roll(xb, LANES - 16, 1), pltpu.roll(xb, 16, 1))
        out.append(xb * cos + partner * sin)
    return jnp.concatenate(out, axis=1)


def _load_stream(h_refs):
    if len(h_refs) == 1:
        return h_refs[0][...]
    return jnp.where(pl.program_id(0) == 0, h_refs[0][...], h_refs[1][...])


def _stream_specs(hs, b, t0=0, nb=None):
    lead = nb
    if len(hs) == 1:
        return [pl.BlockSpec((lead, TOK_TILE, D_MODEL), lambda t, i: (i, t + t0, 0))]
    assert t0 == 0
    return [pl.BlockSpec((lead, TOK_TILE, D_MODEL), lambda t, i: (jnp.where(t == 0, i, 0), 0, 0)),
            pl.BlockSpec((lead, TOK_TILE, D_MODEL), lambda t, i: (i, jnp.maximum(t - 1, 0), 0))]


def _inproj_kernel(*refs):
    mod_ref, g_ref, w_ref, cq_ref, sq_ref, ck_ref, sk_ref, y_ref, q_ref, k_ref, v_ref = refs[-11:]
    h = _load_stream(refs[:-11])
    g = g_ref[...]
    ys = []
    for j in range(INPROJ_BATCH):
        mod = mod_ref[j]
        ys.append((_rms(h[j], g) * (1.0 + mod[1:2]) + mod[0:1]).astype(BF16))
    us = [jnp.dot(y, w_ref[...], preferred_element_type=F32) for y in ys]
    lane = lax.broadcasted_iota(jnp.int32, (TOK_TILE, LANES), 1)
    first_half = (lane % 32) < 16
    for j, u in enumerate(us):
        y_ref[j] = ys[j]
        q_ref[j] = _rope(u[:, 0:DA_WIDTH], cq_ref[...], sq_ref[...], first_half).astype(BF16)
        k_ref[j] = _rope(u[:, DA_WIDTH:2 * DA_WIDTH], ck_ref[...], sk_ref[...], first_half).astype(BF16)
        v_ref[j] = u[:, 2 * DA_WIDTH:QKV_COLS].astype(BF16)


def _in_proj(hs, mod, norm_g, w_qkv, layer, tables):
    b = hs[0].shape[0]
    l = sum(h.shape[1] for h in hs)
    nt = l // TOK_TILE
    nb = INPROJ_BATCH
    assert b % nb == 0
    tok = lambda w: pl.BlockSpec((nb, TOK_TILE, w), lambda t, i: (i, t, 0))
    tab = pl.BlockSpec((TOK_TILE, LANES), lambda t, i: (t, 0))
    widths = (D_MODEL, DA_WIDTH, DA_WIDTH, DA_WIDTH)
    return pl.pallas_call(
        _inproj_kernel,
        grid=(nt, b // nb),
        in_specs=_stream_specs(hs, b, nb=nb) + [
            pl.BlockSpec((nb, 6, D_MODEL), lambda t, i: (jnp.where(t == 0, b // nb, i), 0, 0)),
            _const_spec((1, D_MODEL)),
            _const_spec((D_MODEL, QKV_COLS), layer),
            tab, tab, tab, tab,
        ],
        out_specs=[tok(w) for w in widths],
        out_shape=[jax.ShapeDtypeStruct((b, l, w), BF16) for w in widths],
        compiler_params=_params("arbitrary", "arbitrary"),
    )(*hs, mod, norm_g, w_qkv, *tables)


def _project_rows(y_ref, w_ref, u_ref, i):
    r0 = i * PROJ_ROWS
    rows = pl.ds(r0 if isinstance(r0, int) else pl.multiple_of(r0, PROJ_ROWS), PROJ_ROWS)
    y = y_ref[rows, :]
    ncols = w_ref.shape[1]
    half = ((ncols // MXU_COLS + 1) // 2) * MXU_COLS
    for cols in (slice(0, half), slice(half, ncols)):
        u_ref[rows, cols] = jnp.dot(y, w_ref[:, cols], preferred_element_type=F32)


def _conv_rows(x_ref, col0, width, r0, rows, w, bias):
    l = x_ref.shape[0]
    assert CTX_LEN % rows == 0
    span = rows + 2 * CONV_PAD
    lo = pl.multiple_of(jnp.maximum(r0 - CONV_PAD, 0), CONV_PAD)
    hi = pl.multiple_of(jnp.minimum(r0 + rows, l - CONV_PAD), CONV_PAD)
    cols = slice(col0, col0 + width)
    seg_start = jnp.logical_or(r0 == 0, r0 == CTX_LEN)
    seg_end = jnp.logical_or(r0 + rows == CTX_LEN, r0 + rows == l)
    xa = jnp.concatenate([jnp.where(seg_start, 0.0, x_ref[pl.ds(lo, CONV_PAD), cols]),
                          x_ref[pl.ds(r0, rows), cols],
                          jnp.where(seg_end, 0.0, x_ref[pl.ds(hi, CONV_PAD), cols])], axis=0)
    acc = None
    for tap in range(4):
        off = tap - 2
        sh = xa if off == 0 else pltpu.roll(xa, (-off) % span, 0)
        term = sh[CONV_PAD:CONV_PAD + rows] * w[tap:tap + 1]
        acc = term if acc is None else acc + term
    return acc + bias


RG_ROWS = 256


def _scan8(a, bx, reverse):
    row = lax.broadcasted_iota(jnp.int32, a.shape, 0)
    for s in (1, 2, 4):
        shift = (SUBLANES - s) if reverse else s
        a_sh = pltpu.roll(a, shift, 0)
        b_sh = pltpu.roll(bx, shift, 0)
        ok = (row < SUBLANES - s) if reverse else (row >= s)
        bx = jnp.where(ok, a * b_sh + bx, bx)
        a = jnp.where(ok, a * a_sh, a)
    return a, bx


def _rglru_kernel(y_ref, wp_ref, ws_ref, cw_ref, cb_ref, wg_ref, bg_ref, lam_ref, o_ref, us_ref,
                  rg_ref, af_ref, bf_ref, ab_ref, bb_ref):
    l = y_ref.shape[0]
    nproj = l // PROJ_ROWS
    assert RG_ROWS == PROJ_ROWS
    for i in range(PROJ_AHEAD):
        _project_rows(y_ref, wp_ref, rg_ref, i)
    cw = cw_ref[...]
    cb = cb_ref[...]
    bg = bg_ref[...]
    coef = (-0.5 * RG_C) * _softplus(-lam_ref[...])

    def coeffs(i, carry):
        r0 = i * RG_ROWS
        xc = _conv_rows(rg_ref, 0, RG_WIDTH, r0, RG_ROWS, cw, cb)
        t = jnp.tanh(jnp.dot(xc.astype(BF16), wg_ref[...], preferred_element_type=F32) + bg)
        hx = 0.5 * xc
        for d, (a_ref, b_ref) in enumerate(((af_ref, bf_ref), (ab_ref, bb_ref))):
            t_a = t[:, (2 * d) * RG_WIDTH:(2 * d + 1) * RG_WIDTH]
            t_x = t[:, (2 * d + 1) * RG_WIDTH:(2 * d + 2) * RG_WIDTH]
            log_a = coef[d:d + 1] * t_a + coef[d:d + 1]
            a = jnp.exp(log_a)
            one_m_a2 = jnp.tanh(log_a) * (-1.0 - a * a)
            a_ref[pl.ds(r0, RG_ROWS), :] = a
            b_ref[pl.ds(r0, RG_ROWS), :] = jnp.sqrt(one_m_a2) * (hx * t_x + hx)
        if i + PROJ_AHEAD < nproj:
            _project_rows(y_ref, wp_ref, rg_ref, i + PROJ_AHEAD)
        return carry

    for i in range(nproj):
        coeffs(i, 0)

    nblk = l // SUBLANES
    nctx = CTX_LEN // SUBLANES

    def scan_block(j, carry):
        hf, hb = carry
        rf = pl.multiple_of(j * SUBLANES, SUBLANES)
        jb = jnp.where(j < nctx, nctx - 1 - j, nblk + nctx - 1 - j)
        rb = pl.multiple_of(jb * SUBLANES, SUBLANES)
        a, bx = _scan8(af_ref[pl.ds(rf, SUBLANES), :], bf_ref[pl.ds(rf, SUBLANES), :], False)
        h = a * hf + bx
        af_ref[pl.ds(rf, SUBLANES), :] = h
        hf = h[SUBLANES - 1:SUBLANES]
        a, bx = _scan8(ab_ref[pl.ds(rb, SUBLANES), :], bb_ref[pl.ds(rb, SUBLANES), :], True)
        h = a * hb + bx
        ab_ref[pl.ds(rb, SUBLANES), :] = h
        hb = h[0:1]
        return hf, hb

    per_step = PROJ_ROWS // SUBLANES

    def scan(i, carry):
        _project_rows(y_ref, ws_ref, us_ref, i)
        for jj in range(per_step):
            carry = scan_block(i * per_step + jj, carry)
        return carry

    zero = jnp.zeros((1, RG_WIDTH), F32)
    lax.fori_loop(0, nproj, scan, (zero, zero))

    def finish(i, carry):
        r0 = pl.multiple_of(i * RG_ROWS, RG_ROWS)
        hsum = af_ref[pl.ds(r0, RG_ROWS), :] + ab_ref[pl.ds(r0, RG_ROWS), :]
        g = rg_ref[pl.ds(r0, RG_ROWS), RG_WIDTH:2 * RG_WIDTH]
        o_ref[pl.ds(r0, RG_ROWS), :] = (hsum * jax.nn.gelu(g, approximate=True)).astype(BF16)
        us_ref[pl.ds(r0, RG_ROWS), S_Z:SSDP_COLS] = rg_ref[pl.ds(r0, RG_ROWS), 2 * RG_WIDTH:RGP_COLS]
        return carry

    lax.fori_loop(0, l // RG_ROWS, finish, 0)


def _rglru(y, w_proj, w_ssd, layer, conv_w, conv_b, w_gates, b_gates, lam):
    b, l, _ = y.shape
    seq = lambda w: pl.BlockSpec((None, l, w), lambda i: (i, 0, 0))
    scratch = [pltpu.VMEM((l, RGP_COLS), F32)] + [pltpu.VMEM((l, RG_WIDTH), F32)] * 4
    return pl.pallas_call(
        _rglru_kernel,
        grid=(b,),
        in_specs=[
            seq(D_MODEL),
            _const_spec((D_MODEL, RGP_COLS), layer),
            _const_spec((D_MODEL, S_Z), layer),
            _const_spec((4, RG_WIDTH)),
            _const_spec((1, RG_WIDTH)),
            _const_spec((RG_WIDTH, 4 * RG_WIDTH)),
            _const_spec((1, 4 * RG_WIDTH)),
            _const_spec((2, RG_WIDTH)),
        ],
        out_specs=[seq(RG_WIDTH), seq(SSDP_COLS)],
        out_shape=[jax.ShapeDtypeStruct((b, l, RG_WIDTH), BF16), jax.ShapeDtypeStruct((b, l, SSDP_COLS), F32)],
        scratch_shapes=scratch,
        compiler_params=_params("arbitrary"),
    )(y, w_proj, w_ssd, conv_w, conv_b, w_gates, b_gates, lam)


SSD_ROWS = 64
GROUP_W = SSD_WIDTH // SSD_GROUPS
HEADS_PER_GROUP = SSD_HEADS // SSD_GROUPS
STATE_ROWS = SSD_GROUPS * SSD_STATE
NT_DIMS = (((1,), (1,)), ((), ()))
SSD_GROUP_CHUNKS = 9


def _ssd_head_lane(direction, group, j):
    return direction * SSD_HEADS + group * HEADS_PER_GROUP + j


def _ssd_kernel(u_ref, cw_ref, cb_ref, dtb_ref, aneg_ref, dsk_ref, ng_ref, o_ref,
                xbc_ref, dts_ref, a3_ref, y_ref, ecc_ref, ds_ref, sent_ref, etot_ref):
    l = u_ref.shape[0]
    q = SSD_CHUNK
    assert HEADS_PER_GROUP == 2 and GROUP_W == LANES and STATE_ROWS == LANES
    cw = cw_ref[...]
    cb = cb_ref[...]
    dtb = dtb_ref[...]
    dsk = dsk_ref[...]
    aneg = aneg_ref[...] * math.log2(math.e)

    def prep(i, carry):
        r0 = pl.multiple_of(i * SSD_ROWS, SSD_ROWS)
        rows = pl.ds(r0, SSD_ROWS)
        xbc = _silu(_conv_rows(u_ref, S_XBC, SSD_XBC, r0, SSD_ROWS, cw, cb))
        xbc_ref[rows, :] = xbc
        dt = _softplus(u_ref[rows, S_DT:S_Z] + dtb)
        dts_ref[rows, :] = dt
        y_ref[rows, :] = xbc[:, 0:SSD_WIDTH] * dsk
        a = dt * aneg
        hi = a.astype(BF16).astype(F32)
        r1 = a - hi
        mid = r1.astype(BF16).astype(F32)
        lo = r1 - mid
        a3_ref[rows, :] = (hi + pltpu.roll(mid, 2 * SSD_HEADS, 1) + pltpu.roll(lo, 4 * SSD_HEADS, 1)).astype(BF16)
        return carry

    lax.fori_loop(0, l // SSD_ROWS, prep, 0, unroll=9)

    ri = lax.broadcasted_iota(jnp.int32, (q, q), 0)
    ci = lax.broadcasted_iota(jnp.int32, (q, q), 1)
    lower = ri >= ci
    upper = ci >= ri
    tri_both = jnp.concatenate([lower.astype(F32), upper.astype(F32)], axis=0).astype(BF16)
    eye16 = (ri == ci).astype(F32).astype(BF16)
    eye_rows = (lax.broadcasted_iota(jnp.int32, (2 * SUBLANES, LANES), 0)
                == lax.broadcasted_iota(jnp.int32, (2 * SUBLANES, LANES), 1)).astype(F32)
    lane = lax.broadcasted_iota(jnp.int32, (q, LANES), 1)
    left = lane < SSD_HEADDIM
    lane_row = lax.broadcasted_iota(jnp.int32, (1, LANES), 1)
    left_row = lane_row < SSD_HEADDIM
    sub8 = lax.broadcasted_iota(jnp.int32, (SUBLANES, q), 0)
    left_state = lax.broadcasted_iota(jnp.int32, (SSD_STATE, GROUP_W), 1) < SSD_HEADDIM
    nchunk = l // q
    nctx = CTX_LEN // q

    def local(cg, carry):
        chunks = [cg * SSD_GROUP_CHUNKS + k for k in range(SSD_GROUP_CHUNKS)]
        rows = [pl.ds(pl.multiple_of(c * q, q), q) for c in chunks]

        def fold(cs):
            return cs + pltpu.roll(cs, LANES - 2 * SSD_HEADS, 1) + pltpu.roll(cs, LANES - 4 * SSD_HEADS, 1)

        x16, bm_t, grams, cs = [], [], [], []
        for r in rows:
            x16.append(xbc_ref[r, 0:SSD_WIDTH].astype(BF16))
            bm16 = xbc_ref[r, SSD_WIDTH:SSD_WIDTH + LANES].astype(BF16)
            cm = xbc_ref[r, SSD_WIDTH + LANES:SSD_XBC]
            cm_g = [jnp.where((lane < SSD_STATE) == (g == 0), cm, 0.0).astype(BF16) for g in range(SSD_GROUPS)]
            nt = lax.dot_general(jnp.concatenate(cm_g + [eye16], axis=0), bm16, NT_DIMS,
                                 preferred_element_type=F32)
            grams.append([nt[g * q:(g + 1) * q] for g in range(SSD_GROUPS)])
            bm_t.append(nt[SSD_GROUPS * q:])
            cs.append(jnp.dot(tri_both, a3_ref[r, :], preferred_element_type=F32))

        c_col, c_row, dt_row = [], [], []
        for k, r in enumerate(rows):
            cc = jnp.where(lane < SSD_HEADS, fold(cs[k][0:q]), fold(cs[k][q:2 * q]))
            c_col.append(cc)
            both = jnp.where(lane < 2 * SSD_HEADS, cc, pltpu.roll(dts_ref[r, :], 2 * SSD_HEADS, 1))
            rows_t = lax.dot_general(eye_rows, both, NT_DIMS, precision=lax.Precision.HIGHEST,
                                     preferred_element_type=F32)
            c_row.append(rows_t[0:SUBLANES])
            dt_row.append(rows_t[SUBLANES:2 * SUBLANES])

        zblock = jnp.zeros((SSD_STATE, GROUP_W), F32)
        for k, (c, r) in enumerate(zip(chunks, rows)):
            tot_col = jnp.where(sub8 < SSD_HEADS, c_row[k][:, q - 1:q], c_row[k][:, 0:1])
            dtw_row = dt_row[k] * jnp.exp2(tot_col - c_row[k])
            tot_row = jnp.where(lane_row < SSD_HEADS, c_col[k][q - 1:q, :], c_col[k][0:1, :])
            e_tot = jnp.exp2(tot_row)
            ds = [[None] * SSD_GROUPS for _ in range(2)]
            etot = [[None] * SSD_GROUPS for _ in range(2)]
            for g in range(SSD_GROUPS):
                cols = slice(g * GROUP_W, (g + 1) * GROUP_W)
                lhs_m, lhs_b, ecc = [], [], []
                for d in range(2):
                    mask = lower if d == 0 else upper
                    for j in range(HEADS_PER_GROUP):
                        li = _ssd_head_lane(d, g, j)
                        ccb = jnp.broadcast_to(c_col[k][:, li:li + 1], (q, q))
                        decay = jnp.exp2(jnp.where(mask, ccb - c_row[k][li:li + 1, :], -1e30))
                        lhs_m.append((grams[k][g] * decay * dt_row[k][li:li + 1, :]).astype(BF16))
                        lhs_b.append((bm_t[k][g * SSD_STATE:(g + 1) * SSD_STATE, :]
                                      * dtw_row[li:li + 1, :]).astype(BF16))
                        ecc.append(jnp.exp2(ccb))
                prod = jnp.dot(jnp.concatenate(lhs_m + lhs_b, axis=0), x16[k][:, cols],
                               preferred_element_type=F32)
                nm = 2 * HEADS_PER_GROUP * q
                for d in range(2):
                    a, b = (2 * d) * q, (2 * d + 1) * q
                    y_ref[r, cols] += jnp.where(left, prod[a:a + q], prod[b:b + q])
                    ecc_ref[d, r, cols] = jnp.where(left, ecc[2 * d], ecc[2 * d + 1])
                    a, b = nm + (2 * d) * SSD_STATE, nm + (2 * d + 1) * SSD_STATE
                    dsg = jnp.where(left_state, prod[a:a + SSD_STATE], prod[b:b + SSD_STATE])
                    ds[d][g] = jnp.concatenate([dsg, zblock] if g == 0 else [zblock, dsg], axis=1)
                    la = _ssd_head_lane(d, g, 0)
                    lb = _ssd_head_lane(d, g, 1)
                    etot[d][g] = jnp.where(left_row, e_tot[:, la:la + 1], e_tot[:, lb:lb + 1])
            for d in range(2):
                ds_ref[d, pl.ds(pl.multiple_of(c * STATE_ROWS, STATE_ROWS), STATE_ROWS), :] = (
                    jnp.concatenate(ds[d], axis=0))
                etot_ref[d, pl.ds(pl.multiple_of(c * SUBLANES, SUBLANES), SUBLANES), :] = (
                    jnp.broadcast_to(jnp.concatenate(etot[d], axis=1), (SUBLANES, SSD_WIDTH)))
        return carry

    assert nchunk % SSD_GROUP_CHUNKS == 0
    lax.fori_loop(0, nchunk // SSD_GROUP_CHUNKS, local, 0)

    def carry_state(j, state):
        sf, sb = state
        jb = jnp.where(j < nctx, nctx - 1 - j, nchunk + nctx - 1 - j)
        out = []
        for d, (c, s) in enumerate(((j, sf), (jb, sb))):
            srows = pl.ds(pl.multiple_of(c * STATE_ROWS, STATE_ROWS), STATE_ROWS)
            sent_ref[d, srows, :] = s.astype(BF16)
            e = etot_ref[d, pl.ds(pl.multiple_of(c * SUBLANES, SUBLANES), 1), :]
            out.append(e * s + ds_ref[d, srows, :])
        return tuple(out)

    zero = jnp.zeros((STATE_ROWS, SSD_WIDTH), F32)
    lax.fori_loop(0, nchunk, carry_state, (zero, zero))

    ng = ng_ref[...]

    def finish(c, carry):
        r0 = pl.multiple_of(c * q, q)
        rows = pl.ds(r0, q)
        srows = pl.ds(pl.multiple_of(c * STATE_ROWS, STATE_ROWS), STATE_ROWS)
        cm16 = xbc_ref[rows, SSD_WIDTH + LANES:SSD_XBC].astype(BF16)
        y = y_ref[rows, :]
        for d in range(2):
            y = y + ecc_ref[d, rows, :] * jnp.dot(cm16, sent_ref[d, srows, :], preferred_element_type=F32)
        v = y * _silu(u_ref[rows, S_Z:SSDP_COLS])
        parts = []
        for g in range(SSD_GROUPS):
            vg = v[:, g * GROUP_W:(g + 1) * GROUP_W]
            parts.append(vg * lax.rsqrt(jnp.mean(vg * vg, axis=-1, keepdims=True) + NORM_EPS))
        o_ref[rows, :] = (jnp.concatenate(parts, axis=1) * ng).astype(BF16)
        return carry

    lax.fori_loop(0, nchunk, finish, 0, unroll=6)


def _ssd(u, conv_w, conv_b, dt_bias, aneg, dskip, norm_g):
    b, l, _ = u.shape
    nchunk = l // SSD_CHUNK
    seq = lambda w: pl.BlockSpec((None, l, w), lambda i: (i, 0, 0))
    scratch = [
        pltpu.VMEM((l, SSD_XBC), F32),
        pltpu.VMEM((l, LANES), F32),
        pltpu.VMEM((l, LANES), BF16),
        pltpu.VMEM((l, SSD_WIDTH), F32),
        pltpu.VMEM((2, l, SSD_WIDTH), F32),
        pltpu.VMEM((2, nchunk * STATE_ROWS, SSD_WIDTH), F32),
        pltpu.VMEM((2, nchunk * STATE_ROWS, SSD_WIDTH), BF16),
        pltpu.VMEM((2, nchunk * SUBLANES, SSD_WIDTH), F32),
    ]
    return pl.pallas_call(
        _ssd_kernel,
        grid=(b,),
        in_specs=[
            seq(SSDP_COLS),
            _const_spec((4, SSD_XBC)),
            _const_spec((1, SSD_XBC)),
            _const_spec((1, LANES)),
            _const_spec((1, LANES)),
            _const_spec((1, SSD_WIDTH)),
            _const_spec((1, SSD_WIDTH)),
        ],
        out_specs=seq(SSD_WIDTH),
        out_shape=jax.ShapeDtypeStruct((b, l, SSD_WIDTH), BF16),
        scratch_shapes=scratch,
        compiler_params=_params("arbitrary"),
    )(u, conv_w, conv_b, dt_bias, aneg, dskip, norm_g)


def _attn_probs(q, k_ref, p_ref, nkeys):
    l = k_ref.shape[0]
    for c in range(2):
        qc = q[:, c * DA_HEAD_DIM:(c + 1) * DA_HEAD_DIM]
        kc = k_ref[0:nkeys, c * DA_HEAD_DIM:(c + 1) * DA_HEAD_DIM]
        s = lax.dot_general(qc, kc, (((1,), (1,)), ((), ())), preferred_element_type=F32)
        p_ref[:, c * l:c * l + nkeys] = jnp.exp2(s - jnp.max(s, axis=-1, keepdims=True)).astype(BF16)


def _attn_values(p_ref, vaug_ref, nkeys, lam, g, lam_init):
    l = vaug_ref.shape[0]
    outs = []
    for c in range(2):
        ov = jnp.dot(p_ref[:, c * l:c * l + nkeys], vaug_ref[0:nkeys, :], preferred_element_type=F32)
        outs.append(ov[:, 0:DA_V_DIM] / ov[:, DA_V_DIM:DA_V_DIM + 1])
    o = outs[0] - lam * outs[1]
    return (_rms(o, g) * (1.0 - lam_init)).astype(BF16)


def _attn_kernel(q_ref, k_ref, v_ref, lam_ref, g_ref, *refs, lam_init, need_ctx):
    nw = len(FFN_WEIGHT_SHAPES)
    w_refs, o_ref, w16_refs = refs[:nw], refs[nw], refs[nw + 1:2 * nw + 1]
    vaug_ref, pa_ref, pb_ref = refs[2 * nw + 1:]

    @pl.when(pl.program_id(1) == 0)
    def _():
        for w_ref, w16_ref in zip(w_refs, w16_refs):
            w16_ref[...] = w_ref[...].astype(BF16)

    l = q_ref.shape[0]
    nt = l // TOK_TILE
    assert CTX_LEN % TOK_TILE == 0 and nt >= 2
    lv = lam_ref[...]
    lam = (jnp.exp(jnp.sum(lv[0:1] * lv[1:2], axis=-1, keepdims=True))
           - jnp.exp(jnp.sum(lv[2:3] * lv[3:4], axis=-1, keepdims=True)) + lam_init)
    g = g_ref[...]
    vaug_ref[:, 0:DA_V_DIM] = v_ref[...]
    vaug_ref[:, DA_V_DIM:2 * DA_V_DIM] = jnp.ones((l, DA_V_DIM), BF16)

    def rows(t):
        return pl.ds(pl.multiple_of(t * TOK_TILE, TOK_TILE), TOK_TILE)

    def probs(t, p_ref, nkeys=l):
        _attn_probs(q_ref[rows(t), :], k_ref, p_ref, nkeys)

    def values(t, p_ref, nkeys=l):
        o_ref[rows(t), :] = _attn_values(p_ref, vaug_ref, nkeys, lam, g, lam_init)

    tiles = list(range(nt)) if need_ctx else list(range(CTX_LEN // TOK_TILE, nt))
    if not need_ctx:
        o_ref[0:CTX_LEN, :] = jnp.zeros((CTX_LEN, DA_V_DIM), BF16)
    bufs = (pa_ref, pb_ref)
    keys = lambda t: CTX_LEN if t < CTX_LEN // TOK_TILE else l
    for idx, t in enumerate(tiles):
        probs(t, bufs[idx % 2], keys(t))
        if idx > 0:
            values(tiles[idx - 1], bufs[(idx - 1) % 2], keys(tiles[idx - 1]))
    values(tiles[-1], bufs[(len(tiles) - 1) % 2], keys(tiles[-1]))


FFN_WEIGHT_SHAPES = ((D_MODEL, D_MODEL), (D_MODEL, D_FF), (D_MODEL, D_FF), (D_FF, D_MODEL))


def _attention(q, k, v, lam_vec, subln_g, layer_idx, ffn_weights, need_ctx):
    b, l, _ = q.shape
    lam_init = 0.8 - 0.6 * math.exp(-0.3 * layer_idx)
    seq = pl.BlockSpec((None, l, DA_V_DIM), lambda i, h: (i, 0, h))
    w_in_specs, w_out_specs, w_out_shapes = [], [], []
    for rows, cols in FFN_WEIGHT_SHAPES:
        assert rows % b == 0 and (rows // b) % (2 * SUBLANES) == 0
        w_in_specs.append(pl.BlockSpec((None, rows // b, cols), lambda i, h: (layer_idx, i, 0)))
        w_out_specs.append(pl.BlockSpec((rows // b, cols), lambda i, h: (i, 0)))
        w_out_shapes.append(jax.ShapeDtypeStruct((rows, cols), BF16))
    res = pl.pallas_call(
        functools.partial(_attn_kernel, lam_init=lam_init, need_ctx=need_ctx),
        grid=(b, DA_HEADS),
        in_specs=[seq, seq, seq, _const_spec((4, DA_HEAD_DIM)), _const_spec((1, DA_V_DIM))] + w_in_specs,
        out_specs=[seq] + w_out_specs,
        out_shape=[jax.ShapeDtypeStruct((b, l, DA_WIDTH), BF16)] + w_out_shapes,
        scratch_shapes=[pltpu.VMEM((l, 2 * DA_V_DIM), BF16)] + [pltpu.VMEM((TOK_TILE, 2 * l), BF16)] * 2,
        compiler_params=_params("arbitrary", "arbitrary"),
    )(q, k, v, lam_vec, subln_g, *ffn_weights)
    return res[0], res[1:]


def _out_ffn_kernel(*refs, final):
    rg_ref, ssd_ref, da_ref, mod_ref, g2_ref, wo_ref, wg_ref, wu_ref, wd_ref, gf_ref, o_ref = refs[-11:]
    h_in = _load_stream(refs[:-11])
    g2 = g2_ref[...]
    nb = FFN_BATCH
    mods = [mod_ref[j] for j in range(nb)]
    mixes = []
    for j in range(nb):
        mix = jnp.dot(rg_ref[j], wo_ref[0:RG_WIDTH, :], preferred_element_type=F32)
        mix += jnp.dot(ssd_ref[j], wo_ref[RG_WIDTH:RG_WIDTH + SSD_WIDTH, :], preferred_element_type=F32)
        mix += jnp.dot(da_ref[j], wo_ref[RG_WIDTH + SSD_WIDTH:, :], preferred_element_type=F32)
        mixes.append(mix)
    hs = [h_in[j] + mods[j][2:3] * mixes[j] for j in range(nb)]
    ys = [(_rms(hs[j], g2) * (1.0 + mods[j][4:5]) + mods[j][3:4]).astype(BF16) for j in range(nb)]
    gates, ups = [], []
    for y in ys:
        gates.append(jnp.dot(y, wg_ref[...], preferred_element_type=F32))
        ups.append(jnp.dot(y, wu_ref[...], preferred_element_type=F32))
    acts = [(_silu(gates[j]) * ups[j]).astype(BF16) for j in range(nb)]
    downs = [jnp.dot(a, wd_ref[...], preferred_element_type=F32) for a in acts]
    for j in range(nb):
        h = hs[j] + mods[j][5:6] * downs[j]
        o_ref[j] = _rms(h, gf_ref[...]) if final else h


def _out_ffn(hs, rg, ssd, da, mod, norm2_g, w_out, w_gate, w_up, w_down, final_g, final):
    b, l, _ = rg.shape
    nb = FFN_BATCH
    assert b % nb == 0
    t0 = CTX_LEN // TOK_TILE if final else 0
    nt = l // TOK_TILE - t0
    tok = lambda w: pl.BlockSpec((nb, TOK_TILE, w), lambda t, i: (i, t + t0, 0))
    out_rows = nt * TOK_TILE
    kwargs = {"input_output_aliases": {0: 0}} if (len(hs) == 1 and not final) else {}
    return pl.pallas_call(
        functools.partial(_out_ffn_kernel, final=final),
        grid=(nt, b // nb),
        in_specs=_stream_specs(hs, b, t0, nb) + [
            tok(RG_WIDTH), tok(SSD_WIDTH), tok(DA_WIDTH),
            pl.BlockSpec((nb, 6, D_MODEL), lambda t, i: (jnp.where(t + t0 == 0, b // nb, i), 0, 0)),
            _const_spec((1, D_MODEL)),
            _const_spec((D_MODEL, D_MODEL)),
            _const_spec((D_MODEL, D_FF)),
            _const_spec((D_MODEL, D_FF)),
            _const_spec((D_FF, D_MODEL)),
            _const_spec((1, D_MODEL)),
        ],
        out_specs=pl.BlockSpec((nb, TOK_TILE, D_MODEL), lambda t, i: (i, t, 0)),
        out_shape=jax.ShapeDtypeStruct((b, out_rows, D_MODEL), F32),
        compiler_params=_params("arbitrary", "arbitrary"),
        **kwargs,
    )(*hs, rg, ssd, da, mod, norm2_g, w_out, w_gate, w_up, w_down, final_g)


def _rope_tables(n_lat):
    half = DA_HEAD_DIM // 2
    inv_freq = jnp.power(ROPE_BASE, -jnp.arange(0, half, 2, dtype=F32) / half)
    t = jnp.arange(n_lat, dtype=jnp.int32)
    ang_r = (t // GRID_W).astype(F32)[:, None] * inv_freq
    ang_c = (t % GRID_W).astype(F32)[:, None] * inv_freq
    cos = jnp.concatenate([jnp.cos(ang_r), jnp.cos(ang_r), jnp.cos(ang_c), jnp.cos(ang_c)], axis=1)
    sin = jnp.concatenate([-jnp.sin(ang_r), jnp.sin(ang_r), -jnp.sin(ang_c), jnp.sin(ang_c)], axis=1)
    reps = LANES // DA_HEAD_DIM
    cos = jnp.concatenate([jnp.ones((CTX_LEN, DA_HEAD_DIM), F32), cos], axis=0)
    sin = jnp.concatenate([jnp.zeros((CTX_LEN, DA_HEAD_DIM), F32), sin], axis=0)
    cos = jnp.tile(cos, (1, reps))
    sin = jnp.tile(sin, (1, reps))
    scale = DA_HEAD_DIM ** -0.5 * math.log2(math.e)
    return cos * scale, sin * scale, cos, sin


def _split_w_in(w):
    offs = [0]
    for size in IN_SIZES:
        offs.append(offs[-1] + size)
    w_qkv = w[:, :, offs[5]:offs[8]]
    w_rg = w[:, :, offs[0]:offs[3]]
    pad = jnp.zeros(w.shape[:2] + (LANES - IN_SIZES[4],), w.dtype)
    w_ssd = jnp.concatenate([w[:, :, offs[3]:offs[5]], pad], axis=2)
    return w_qkv.astype(BF16), w_rg.astype(BF16), w_ssd.astype(BF16)


def _block_diag(w):
    eye = jnp.eye(RG_HEADS, dtype=w.dtype)
    return jnp.einsum("hij,hg->higj", w, eye).reshape(RG_WIDTH, RG_WIDTH)


def _lane_pad(v):
    return jnp.concatenate([v, jnp.zeros((LANES - v.shape[0],), v.dtype)])[None, :]


def kernel(x, c, ctx, c_ctx, w_mod, b_mod, norm1_g, w_in, rg_conv_w, rg_conv_b, rg_w_a, rg_b_a, rg_w_x, rg_b_x, rg_lambda, ssd_conv_w, ssd_conv_b, ssd_dt_bias, ssd_a_log, ssd_d, ssd_norm_g, da_lambda, da_subln_g, w_out, norm2_g, w_gate, w_up, w_down, final_norm_g):
    b, n_lat, _ = x.shape
    assert ctx.shape[1] == CTX_LEN and n_lat % TOK_TILE == 0
    n_ctx = max(INPROJ_BATCH, FFN_BATCH)
    rows = -(-(b + n_ctx) // SUBLANES) * SUBLANES
    c_all = jnp.concatenate([c] + [c_ctx[None, :]] * n_ctx
                            + [jnp.zeros((rows - b - n_ctx, D_MODEL), F32)], axis=0)
    mods = _modulation(c_all, w_mod, b_mod)
    tables = _rope_tables(n_lat)
    hs = (ctx, x)
    w_qkv16, w_rg16, w_ssd16 = _split_w_in(w_in)
    out = None
    for l in range(DEPTH):
        final = l == DEPTH - 1
        y, q, k, v = _in_proj(hs, mods[l], norm1_g[l][None, :], w_qkv16, l, tables)
        w_gates = (0.5 * jnp.concatenate(
            [_block_diag(w[l, d]) for d in range(2) for w in (rg_w_a, rg_w_x)], axis=1)).astype(BF16)
        b_gates = 0.5 * jnp.concatenate(
            [bb[l, d] for d in range(2) for bb in (rg_b_a, rg_b_x)])[None, :]
        y_rg, u_ssd = _rglru(y, w_rg16, w_ssd16, l, rg_conv_w[l], rg_conv_b[l][None, :], w_gates, b_gates, rg_lambda[l])
        y_ssd = _ssd(u_ssd, ssd_conv_w[l], ssd_conv_b[l][None, :],
                     _lane_pad(ssd_dt_bias[l].reshape(-1)),
                     _lane_pad(-jnp.exp(ssd_a_log[l].reshape(-1))),
                     jnp.repeat(ssd_d[l], SSD_HEADDIM)[None, :], ssd_norm_g[l][None, :])
        y_da, ffn16 = _attention(q, k, v, da_lambda[l], da_subln_g[l][None, :], l,
                                 (w_out, w_gate, w_up, w_down), not final)
        res = _out_ffn(hs, y_rg, y_ssd, y_da, mods[l], norm2_g[l][None, :], *ffn16,
                       final_norm_g[None, :], final)
        if final:
            out = res
        else:
            hs = (res,)
    return out
```

```python
import functools
import math

import jax
import jax.numpy as jnp
from jax import lax
from jax.experimental import pallas as pl
from jax.experimental.pallas import tpu as pltpu

F32 = jnp.float32
BF16 = jnp.bfloat16

D_MODEL = 1024
DEPTH = 2
CTX_LEN = 256
GRID_W = 64
NORM_EPS = 1e-6

RG_WIDTH = 256
RG_HEADS = 4
RG_BLOCK = 64
RG_C = 8.0

SSD_WIDTH = 256
SSD_HEADDIM = 64
SSD_HEADS = 4
SSD_GROUPS = 2
SSD_STATE = 64
SSD_CHUNK = 128
SSD_XBC = 512

DA_WIDTH = 512
DA_HEADS = 4
DA_HEAD_DIM = 64
DA_V_DIM = 128
ROPE_BASE = 10000.0

D_FF = 2816
IN_SIZES = (256, 256, 256, 512, 8, 512, 512, 512)
D_IN = sum(IN_SIZES)

LANES = 128
SUBLANES = 8
MXU_COLS = 256
TOK_TILE = 256
INPROJ_BATCH = 4
FFN_BATCH = 2
CONV_PAD = SUBLANES
VMEM_LIMIT = 56 * 1024 * 1024

QKV_COLS = 3 * DA_WIDTH
RGP_COLS = 3 * RG_WIDTH
S_XBC = 0
S_DT = 512
S_Z = 640
SSDP_COLS = 896
PROJ_ROWS = 256
PROJ_AHEAD = 2


def _silu(x):
    hx = 0.5 * x
    return hx * jnp.tanh(hx) + hx


def _softplus(x):
    return jnp.maximum(x, 0.0) + jnp.log(1.0 + jnp.exp(-jnp.abs(x)))


def _rms(x, g):
    ms = jnp.mean(x * x, axis=-1, keepdims=True)
    return x * lax.rsqrt(ms + NORM_EPS) * g


def _const_spec(shape, layer=None):
    nd = len(shape)
    if layer is None:
        return pl.BlockSpec(shape, lambda *_: (0,) * nd, pipeline_mode=pl.Buffered(1))
    return pl.BlockSpec((None,) + tuple(shape), lambda *_: (layer,) + (0,) * nd, pipeline_mode=pl.Buffered(1))


def _params(*sem):
    return pltpu.CompilerParams(dimension_semantics=sem, vmem_limit_bytes=VMEM_LIMIT)


def _mod_kernel(c_ref, w_ref, b_ref, o_ref):
    c = c_ref[...]
    a = _silu(c).astype(BF16)
    o_ref[...] = jnp.dot(a, w_ref[...].astype(BF16), preferred_element_type=F32) + b_ref[...]


def _modulation(c_all, w_mod, b_mod):
    r = c_all.shape[0]
    out = pl.pallas_call(
        _mod_kernel,
        grid=(DEPTH, 6),
        in_specs=[
            pl.BlockSpec((r, D_MODEL), lambda l, j: (0, 0)),
            pl.BlockSpec((None, D_MODEL, D_MODEL), lambda l, j: (l, 0, j)),
            pl.BlockSpec((None, 1, D_MODEL), lambda l, j: (l, 0, j)),
        ],
        out_specs=pl.BlockSpec((None, r, D_MODEL), lambda l, j: (l, 0, j)),
        out_shape=jax.ShapeDtypeStruct((DEPTH, r, 6 * D_MODEL), F32),
        compiler_params=_params("arbitrary", "arbitrary"),
    )(c_all, w_mod, b_mod.reshape(DEPTH, 1, 6 * D_MODEL))
    return out.reshape(DEPTH, r, 6, D_MODEL)


def _rope(x, cos, sin, first_half):
    out = []
    for blk in range(x.shape[-1] // LANES):
        xb = x[:, blk * LANES:(blk + 1) * LANES]
        partner = jnp.where(first_half, pltpu.roll(xb, LANES - 16, 1), pltpu.roll(xb, 16, 1))
        out.append(xb * cos + partner * sin)
    return jnp.concatenate(out, axis=1)


def _load_stream(h_refs):
    if len(h_refs) == 1:
        return h_refs[0][...]
    return jnp.where(pl.program_id(0) == 0, h_refs[0][...], h_refs[1][...])


def _stream_specs(hs, b, t0=0, nb=None):
    lead = nb
    if len(hs) == 1:
        return [pl.BlockSpec((lead, TOK_TILE, D_MODEL), lambda t, i: (i, t + t0, 0))]
    assert t0 == 0
    return [pl.BlockSpec((lead, TOK_TILE, D_MODEL), lambda t, i: (jnp.where(t == 0, i, 0), 0, 0)),
            pl.BlockSpec((lead, TOK_TILE, D_MODEL), lambda t, i: (i, jnp.maximum(t - 1, 0), 0))]


def _inproj_kernel(*refs):
    mod_ref, g_ref, w_ref, cq_ref, sq_ref, ck_ref, sk_ref, y_ref, q_ref, k_ref, v_ref = refs[-11:]
    h = _load_stream(refs[:-11])
    g = g_ref[...]
    ys = []
    for j in range(INPROJ_BATCH):
        mod = mod_ref[j]
        ys.append((_rms(h[j], g) * (1.0 + mod[1:2]) + mod[0:1]).astype(BF16))
    us = [jnp.dot(y, w_ref[...], preferred_element_type=F32) for y in ys]
    lane = lax.broadcasted_iota(jnp.int32, (TOK_TILE, LANES), 1)
    first_half = (lane % 32) < 16
    for j, u in enumerate(us):
        y_ref[j] = ys[j]
        q_ref[j] = _rope(u[:, 0:DA_WIDTH], cq_ref[...], sq_ref[...], first_half).astype(BF16)
        k_ref[j] = _rope(u[:, DA_WIDTH:2 * DA_WIDTH], ck_ref[...], sk_ref[...], first_half).astype(BF16)
        v_ref[j] = u[:, 2 * DA_WIDTH:QKV_COLS].astype(BF16)


def _in_proj(hs, mod, norm_g, w_qkv, layer, tables):
    b = hs[0].shape[0]
    l = sum(h.shape[1] for h in hs)
    nt = l // TOK_TILE
    nb = INPROJ_BATCH
    assert b % nb == 0
    tok = lambda w: pl.BlockSpec((nb, TOK_TILE, w), lambda t, i: (i, t, 0))
    tab = pl.BlockSpec((TOK_TILE, LANES), lambda t, i: (t, 0))
    widths = (D_MODEL, DA_WIDTH, DA_WIDTH, DA_WIDTH)
    return pl.pallas_call(
        _inproj_kernel,
        grid=(nt, b // nb),
        in_specs=_stream_specs(hs, b, nb=nb) + [
            pl.BlockSpec((nb, 6, D_MODEL), lambda t, i: (jnp.where(t == 0, b // nb, i), 0, 0)),
            _const_spec((1, D_MODEL)),
            _const_spec((D_MODEL, QKV_COLS), layer),
            tab, tab, tab, tab,
        ],
        out_specs=[tok(w) for w in widths],
        out_shape=[jax.ShapeDtypeStruct((b, l, w), BF16) for w in widths],
        compiler_params=_params("arbitrary", "arbitrary"),
    )(*hs, mod, norm_g, w_qkv, *tables)


def _project_rows(y_ref, w_ref, u_ref, i):
    r0 = i * PROJ_ROWS
    rows = pl.ds(r0 if isinstance(r0, int) else pl.multiple_of(r0, PROJ_ROWS), PROJ_ROWS)
    y = y_ref[rows, :]
    ncols = w_ref.shape[1]
    half = ((ncols // MXU_COLS + 1) // 2) * MXU_COLS
    for cols in (slice(0, half), slice(half, ncols)):
        u_ref[rows, cols] = jnp.dot(y, w_ref[:, cols], preferred_element_type=F32)


def _conv_rows(x_ref, col0, width, r0, rows, w, bias):
    l = x_ref.shape[0]
    assert CTX_LEN % rows == 0
    span = rows + 2 * CONV_PAD
    lo = pl.multiple_of(jnp.maximum(r0 - CONV_PAD, 0), CONV_PAD)
    hi = pl.multiple_of(jnp.minimum(r0 + rows, l - CONV_PAD), CONV_PAD)
    cols = slice(col0, col0 + width)
    seg_start = jnp.logical_or(r0 == 0, r0 == CTX_LEN)
    seg_end = jnp.logical_or(r0 + rows == CTX_LEN, r0 + rows == l)
    xa = jnp.concatenate([jnp.where(seg_start, 0.0, x_ref[pl.ds(lo, CONV_PAD), cols]),
                          x_ref[pl.ds(r0, rows), cols],
                          jnp.where(seg_end, 0.0, x_ref[pl.ds(hi, CONV_PAD), cols])], axis=0)
    acc = None
    for tap in range(4):
        off = tap - 2
        sh = xa if off == 0 else pltpu.roll(xa, (-off) % span, 0)
        term = sh[CONV_PAD:CONV_PAD + rows] * w[tap:tap + 1]
        acc = term if acc is None else acc + term
    return acc + bias


RG_ROWS = 256


def _scan8(a, bx, reverse):
    row = lax.broadcasted_iota(jnp.int32, a.shape, 0)
    for s in (1, 2, 4):
        shift = (SUBLANES - s) if reverse else s
        a_sh = pltpu.roll(a, shift, 0)
        b_sh = pltpu.roll(bx, shift, 0)
        ok = (row < SUBLANES - s) if reverse else (row >= s)
        bx = jnp.where(ok, a * b_sh + bx, bx)
        a = jnp.where(ok, a * a_sh, a)
    return a, bx


def _rglru_kernel(y_ref, wp_ref, ws_ref, cw_ref, cb_ref, wg_ref, bg_ref, lam_ref, o_ref, us_ref,
                  rg_ref, af_ref, bf_ref, ab_ref, bb_ref):
    l = y_ref.shape[0]
    nproj = l // PROJ_ROWS
    assert RG_ROWS == PROJ_ROWS
    for i in range(PROJ_AHEAD):
        _project_rows(y_ref, wp_ref, rg_ref, i)
    cw = cw_ref[...]
    cb = cb_ref[...]
    bg = bg_ref[...]
    coef = (-0.5 * RG_C) * _softplus(-lam_ref[...])

    def coeffs(i, carry):
        r0 = i * RG_ROWS
        xc = _conv_rows(rg_ref, 0, RG_WIDTH, r0, RG_ROWS, cw, cb)
        t = jnp.tanh(jnp.dot(xc.astype(BF16), wg_ref[...], preferred_element_type=F32) + bg)
        hx = 0.5 * xc
        for d, (a_ref, b_ref) in enumerate(((af_ref, bf_ref), (ab_ref, bb_ref))):
            t_a = t[:, (2 * d) * RG_WIDTH:(2 * d + 1) * RG_WIDTH]
            t_x = t[:, (2 * d + 1) * RG_WIDTH:(2 * d + 2) * RG_WIDTH]
            log_a = coef[d:d + 1] * t_a + coef[d:d + 1]
            a = jnp.exp(log_a)
            one_m_a2 = jnp.tanh(log_a) * (-1.0 - a * a)
            a_ref[pl.ds(r0, RG_ROWS), :] = a
            b_ref[pl.ds(r0, RG_ROWS), :] = jnp.sqrt(one_m_a2) * (hx * t_x + hx)
        if i + PROJ_AHEAD < nproj:
            _project_rows(y_ref, wp_ref, rg_ref, i + PROJ_AHEAD)
        return carry

    for i in range(nproj):
        coeffs(i, 0)

    nblk = l // SUBLANES
    nctx = CTX_LEN // SUBLANES

    def scan_block(j, carry):
        hf, hb = carry
        rf = pl.multiple_of(j * SUBLANES, SUBLANES)
        jb = jnp.where(j < nctx, nctx - 1 - j, nblk + nctx - 1 - j)
        rb = pl.multiple_of(jb * SUBLANES, SUBLANES)
        a, bx = _scan8(af_ref[pl.ds(rf, SUBLANES), :], bf_ref[pl.ds(rf, SUBLANES), :], False)
        h = a * hf + bx
        af_ref[pl.ds(rf, SUBLANES), :] = h
        hf = h[SUBLANES - 1:SUBLANES]
        a, bx = _scan8(ab_ref[pl.ds(rb, SUBLANES), :], bb_ref[pl.ds(rb, SUBLANES), :], True)
        h = a * hb + bx
        ab_ref[pl.ds(rb, SUBLANES), :] = h
        hb = h[0:1]
        return hf, hb

    per_step = PROJ_ROWS // SUBLANES

    def scan(i, carry):
        _project_rows(y_ref, ws_ref, us_ref, i)
        for jj in range(per_step):
            carry = scan_block(i * per_step + jj, carry)
        return carry

    zero = jnp.zeros((1, RG_WIDTH), F32)
    lax.fori_loop(0, nproj, scan, (zero, zero))

    def finish(i, carry):
        r0 = pl.multiple_of(i * RG_ROWS, RG_ROWS)
        hsum = af_ref[pl.ds(r0, RG_ROWS), :] + ab_ref[pl.ds(r0, RG_ROWS), :]
        g = rg_ref[pl.ds(r0, RG_ROWS), RG_WIDTH:2 * RG_WIDTH]
        o_ref[pl.ds(r0, RG_ROWS), :] = (hsum * jax.nn.gelu(g, approximate=True)).astype(BF16)
        us_ref[pl.ds(r0, RG_ROWS), S_Z:SSDP_COLS] = rg_ref[pl.ds(r0, RG_ROWS), 2 * RG_WIDTH:RGP_COLS]
        return carry

    lax.fori_loop(0, l // RG_ROWS, finish, 0)


def _rglru(y, w_proj, w_ssd, layer, conv_w, conv_b, w_gates, b_gates, lam):
    b, l, _ = y.shape
    seq = lambda w: pl.BlockSpec((None, l, w), lambda i: (i, 0, 0))
    scratch = [pltpu.VMEM((l, RGP_COLS), F32)] + [pltpu.VMEM((l, RG_WIDTH), F32)] * 4
    return pl.pallas_call(
        _rglru_kernel,
        grid=(b,),
        in_specs=[
            seq(D_MODEL),
            _const_spec((D_MODEL, RGP_COLS), layer),
            _const_spec((D_MODEL, S_Z), layer),
            _const_spec((4, RG_WIDTH), layer),
            _const_spec((1, RG_WIDTH), layer),
            _const_spec((RG_WIDTH, 4 * RG_WIDTH), layer),
            _const_spec((1, 4 * RG_WIDTH), layer),
            _const_spec((2, RG_WIDTH), layer),
        ],
        out_specs=[seq(RG_WIDTH), seq(SSDP_COLS)],
        out_shape=[jax.ShapeDtypeStruct((b, l, RG_WIDTH), BF16), jax.ShapeDtypeStruct((b, l, SSDP_COLS), F32)],
        scratch_shapes=scratch,
        compiler_params=_params("arbitrary"),
    )(y, w_proj, w_ssd, conv_w, conv_b, w_gates, b_gates, lam)


SSD_ROWS = 64
GROUP_W = SSD_WIDTH // SSD_GROUPS
HEADS_PER_GROUP = SSD_HEADS // SSD_GROUPS
STATE_ROWS = SSD_GROUPS * SSD_STATE
NT_DIMS = (((1,), (1,)), ((), ()))
SSD_GROUP_CHUNKS = 9


def _ssd_head_lane(direction, group, j):
    return direction * SSD_HEADS + group * HEADS_PER_GROUP + j


def _ssd_kernel(u_ref, cw_ref, cb_ref, dtb_ref, aneg_ref, dsk_ref, ng_ref, o_ref,
                xbc_ref, dts_ref, a3_ref, y_ref, ecc_ref, ds_ref, sent_ref, etot_ref):
    l = u_ref.shape[0]
    q = SSD_CHUNK
    assert HEADS_PER_GROUP == 2 and GROUP_W == LANES and STATE_ROWS == LANES
    cw = cw_ref[...]
    cb = cb_ref[...]
    dtb = dtb_ref[...]
    dsk = dsk_ref[...]
    aneg = aneg_ref[...] * math.log2(math.e)

    def prep(i, carry):
        r0 = pl.multiple_of(i * SSD_ROWS, SSD_ROWS)
        rows = pl.ds(r0, SSD_ROWS)
        xbc = _silu(_conv_rows(u_ref, S_XBC, SSD_XBC, r0, SSD_ROWS, cw, cb))
        xbc_ref[rows, :] = xbc
        dt = _softplus(u_ref[rows, S_DT:S_Z] + dtb)
        dts_ref[rows, :] = dt
        y_ref[rows, :] = xbc[:, 0:SSD_WIDTH] * dsk
        a = dt * aneg
        hi = a.astype(BF16).astype(F32)
        r1 = a - hi
        mid = r1.astype(BF16).astype(F32)
        lo = r1 - mid
        a3_ref[rows, :] = (hi + pltpu.roll(mid, 2 * SSD_HEADS, 1) + pltpu.roll(lo, 4 * SSD_HEADS, 1)).astype(BF16)
        return carry

    lax.fori_loop(0, l // SSD_ROWS, prep, 0, unroll=12)

    ri = lax.broadcasted_iota(jnp.int32, (q, q), 0)
    ci = lax.broadcasted_iota(jnp.int32, (q, q), 1)
    lower = ri >= ci
    upper = ci >= ri
    tri_both = jnp.concatenate([lower.astype(F32), upper.astype(F32)], axis=0).astype(BF16)
    eye16 = (ri == ci).astype(F32).astype(BF16)
    eye_rows = (lax.broadcasted_iota(jnp.int32, (2 * SUBLANES, LANES), 0)
                == lax.broadcasted_iota(jnp.int32, (2 * SUBLANES, LANES), 1)).astype(F32)
    lane = lax.broadcasted_iota(jnp.int32, (q, LANES), 1)
    left = lane < SSD_HEADDIM
    lane_row = lax.broadcasted_iota(jnp.int32, (1, LANES), 1)
    left_row = lane_row < SSD_HEADDIM
    sub8 = lax.broadcasted_iota(jnp.int32, (SUBLANES, q), 0)
    left_state = lax.broadcasted_iota(jnp.int32, (SSD_STATE, GROUP_W), 1) < SSD_HEADDIM
    nchunk = l // q
    nctx = CTX_LEN // q

    def local(cg, carry):
        chunks = [cg * SSD_GROUP_CHUNKS + k for k in range(SSD_GROUP_CHUNKS)]
        rows = [pl.ds(pl.multiple_of(c * q, q), q) for c in chunks]

        def fold(cs):
            return cs + pltpu.roll(cs, LANES - 2 * SSD_HEADS, 1) + pltpu.roll(cs, LANES - 4 * SSD_HEADS, 1)

        x16, bm_t, grams, cs = [], [], [], []
        for r in rows:
            x16.append(xbc_ref[r, 0:SSD_WIDTH].astype(BF16))
            bm16 = xbc_ref[r, SSD_WIDTH:SSD_WIDTH + LANES].astype(BF16)
            cm = xbc_ref[r, SSD_WIDTH + LANES:SSD_XBC]
            cm_g = [jnp.where((lane < SSD_STATE) == (g == 0), cm, 0.0).astype(BF16) for g in range(SSD_GROUPS)]
            nt = lax.dot_general(jnp.concatenate(cm_g + [eye16], axis=0), bm16, NT_DIMS,
                                 preferred_element_type=F32)
            grams.append([nt[g * q:(g + 1) * q] for g in range(SSD_GROUPS)])
            bm_t.append(nt[SSD_GROUPS * q:])
            cs.append(jnp.dot(tri_both, a3_ref[r, :], preferred_element_type=F32))

        c_col, c_row, dt_row = [], [], []
        for k, r in enumerate(rows):
            cc = jnp.where(lane < SSD_HEADS, fold(cs[k][0:q]), fold(cs[k][q:2 * q]))
            c_col.append(cc)
            both = jnp.where(lane < 2 * SSD_HEADS, cc, pltpu.roll(dts_ref[r, :], 2 * SSD_HEADS, 1))
            rows_t = lax.dot_general(eye_rows, both, NT_DIMS, precision=lax.Precision.HIGHEST,
                                     preferred_element_type=F32)
            c_row.append(rows_t[0:SUBLANES])
            dt_row.append(rows_t[SUBLANES:2 * SUBLANES])

        zblock = jnp.zeros((SSD_STATE, GROUP_W), F32)
        for k, (c, r) in enumerate(zip(chunks, rows)):
            tot_col = jnp.where(sub8 < SSD_HEADS, c_row[k][:, q - 1:q], c_row[k][:, 0:1])
            dtw_row = dt_row[k] * jnp.exp2(tot_col - c_row[k])
            tot_row = jnp.where(lane_row < SSD_HEADS, c_col[k][q - 1:q, :], c_col[k][0:1, :])
            e_tot = jnp.exp2(tot_row)
            ds = [[None] * SSD_GROUPS for _ in range(2)]
            etot = [[None] * SSD_GROUPS for _ in range(2)]
            for g in range(SSD_GROUPS):
                cols = slice(g * GROUP_W, (g + 1) * GROUP_W)
                lhs_m, lhs_b, ecc = [], [], []
                for d in range(2):
                    mask = lower if d == 0 else upper
                    for j in range(HEADS_PER_GROUP):
                        li = _ssd_head_lane(d, g, j)
                        ccb = jnp.broadcast_to(c_col[k][:, li:li + 1], (q, q))
                        decay = jnp.exp2(jnp.where(mask, ccb - c_row[k][li:li + 1, :], -1e30))
                        lhs_m.append((grams[k][g] * decay * dt_row[k][li:li + 1, :]).astype(BF16))
                        lhs_b.append((bm_t[k][g * SSD_STATE:(g + 1) * SSD_STATE, :]
                                      * dtw_row[li:li + 1, :]).astype(BF16))
                        ecc.append(jnp.exp2(ccb))
                prod = jnp.dot(jnp.concatenate(lhs_m + lhs_b, axis=0), x16[k][:, cols],
                               preferred_element_type=F32)
                nm = 2 * HEADS_PER_GROUP * q
                for d in range(2):
                    a, b = (2 * d) * q, (2 * d + 1) * q
                    y_ref[r, cols] += jnp.where(left, prod[a:a + q], prod[b:b + q])
                    ecc_ref[d, r, cols] = jnp.where(left, ecc[2 * d], ecc[2 * d + 1])
                    a, b = nm + (2 * d) * SSD_STATE, nm + (2 * d + 1) * SSD_STATE
                    dsg = jnp.where(left_state, prod[a:a + SSD_STATE], prod[b:b + SSD_STATE])
                    ds[d][g] = jnp.concatenate([dsg, zblock] if g == 0 else [zblock, dsg], axis=1)
                    la = _ssd_head_lane(d, g, 0)
                    lb = _ssd_head_lane(d, g, 1)
                    etot[d][g] = jnp.where(left_row, e_tot[:, la:la + 1], e_tot[:, lb:lb + 1])
            for d in range(2):
                ds_ref[d, pl.ds(pl.multiple_of(c * STATE_ROWS, STATE_ROWS), STATE_ROWS), :] = (
                    jnp.concatenate(ds[d], axis=0))
                etot_ref[d, pl.ds(pl.multiple_of(c * SUBLANES, SUBLANES), SUBLANES), :] = (
                    jnp.broadcast_to(jnp.concatenate(etot[d], axis=1), (SUBLANES, SSD_WIDTH)))
        return carry

    assert nchunk % SSD_GROUP_CHUNKS == 0
    lax.fori_loop(0, nchunk // SSD_GROUP_CHUNKS, local, 0)

    def carry_state(j, state):
        sf, sb = state
        jb = jnp.where(j < nctx, nctx - 1 - j, nchunk + nctx - 1 - j)
        out = []
        for d, (c, s) in enumerate(((j, sf), (jb, sb))):
            srows = pl.ds(pl.multiple_of(c * STATE_ROWS, STATE_ROWS), STATE_ROWS)
            sent_ref[d, srows, :] = s.astype(BF16)
            e = etot_ref[d, pl.ds(pl.multiple_of(c * SUBLANES, SUBLANES), 1), :]
            out.append(e * s + ds_ref[d, srows, :])
        return tuple(out)

    zero = jnp.zeros((STATE_ROWS, SSD_WIDTH), F32)
    lax.fori_loop(0, nchunk, carry_state, (zero, zero))

    ng = ng_ref[...]

    def finish(c, carry):
        r0 = pl.multiple_of(c * q, q)
        rows = pl.ds(r0, q)
        srows = pl.ds(pl.multiple_of(c * STATE_ROWS, STATE_ROWS), STATE_ROWS)
        cm16 = xbc_ref[rows, SSD_WIDTH + LANES:SSD_XBC].astype(BF16)
        y = y_ref[rows, :]
        for d in range(2):
            y = y + ecc_ref[d, rows, :] * jnp.dot(cm16, sent_ref[d, srows, :], preferred_element_type=F32)
        v = y * _silu(u_ref[rows, S_Z:SSDP_COLS])
        parts = []
        for g in range(SSD_GROUPS):
            vg = v[:, g * GROUP_W:(g + 1) * GROUP_W]
            parts.append(vg * lax.rsqrt(jnp.mean(vg * vg, axis=-1, keepdims=True) + NORM_EPS))
        o_ref[rows, :] = (jnp.concatenate(parts, axis=1) * ng).astype(BF16)
        return carry

    lax.fori_loop(0, nchunk, finish, 0, unroll=6)


def _ssd(u, layer, conv_w, conv_b, dt_bias, aneg, dskip, norm_g):
    b, l, _ = u.shape
    nchunk = l // SSD_CHUNK
    seq = lambda w: pl.BlockSpec((None, l, w), lambda i: (i, 0, 0))
    scratch = [
        pltpu.VMEM((l, SSD_XBC), F32),
        pltpu.VMEM((l, LANES), F32),
        pltpu.VMEM((l, LANES), BF16),
        pltpu.VMEM((l, SSD_WIDTH), F32),
        pltpu.VMEM((2, l, SSD_WIDTH), F32),
        pltpu.VMEM((2, nchunk * STATE_ROWS, SSD_WIDTH), F32),
        pltpu.VMEM((2, nchunk * STATE_ROWS, SSD_WIDTH), BF16),
        pltpu.VMEM((2, nchunk * SUBLANES, SSD_WIDTH), F32),
    ]
    return pl.pallas_call(
        _ssd_kernel,
        grid=(b,),
        in_specs=[
            seq(SSDP_COLS),
            _const_spec((4, SSD_XBC), layer),
            _const_spec((1, SSD_XBC), layer),
            _const_spec((1, LANES), layer),
            _const_spec((1, LANES), layer),
            _const_spec((1, SSD_WIDTH), layer),
            _const_spec((1, SSD_WIDTH), layer),
        ],
        out_specs=seq(SSD_WIDTH),
        out_shape=jax.ShapeDtypeStruct((b, l, SSD_WIDTH), BF16),
        scratch_shapes=scratch,
        compiler_params=_params("arbitrary"),
    )(u, conv_w, conv_b, dt_bias, aneg, dskip, norm_g)


def _attn_probs(q, k_ref, p_ref, nkeys):
    l = k_ref.shape[0]
    for c in range(2):
        qc = q[:, c * DA_HEAD_DIM:(c + 1) * DA_HEAD_DIM]
        kc = k_ref[0:nkeys, c * DA_HEAD_DIM:(c + 1) * DA_HEAD_DIM]
        s = lax.dot_general(qc, kc, (((1,), (1,)), ((), ())), preferred_element_type=F32)
        p_ref[:, c * l:c * l + nkeys] = jnp.exp2(s - jnp.max(s, axis=-1, keepdims=True)).astype(BF16)


def _attn_values(p_ref, vaug_ref, nkeys, lam, g, lam_init):
    l = vaug_ref.shape[0]
    outs = []
    for c in range(2):
        ov = jnp.dot(p_ref[:, c * l:c * l + nkeys], vaug_ref[0:nkeys, :], preferred_element_type=F32)
        outs.append(ov[:, 0:DA_V_DIM] / ov[:, DA_V_DIM:DA_V_DIM + 1])
    o = outs[0] - lam * outs[1]
    return (_rms(o, g) * (1.0 - lam_init)).astype(BF16)


def _attn_kernel(q_ref, k_ref, v_ref, lam_ref, g_ref, *refs, lam_init, need_ctx):
    nw = len(FFN_WEIGHT_SHAPES)
    w_refs, o_ref, w16_refs = refs[:nw], refs[nw], refs[nw + 1:2 * nw + 1]
    vaug_ref, pa_ref, pb_ref = refs[2 * nw + 1:]

    @pl.when(pl.program_id(1) == 0)
    def _():
        for w_ref, w16_ref in zip(w_refs, w16_refs):
            w16_ref[...] = w_ref[...].astype(BF16)

    l = q_ref.shape[0]
    nt = l // TOK_TILE
    assert CTX_LEN % TOK_TILE == 0 and nt >= 2
    lv = lam_ref[...]
    lam = (jnp.exp(jnp.sum(lv[0:1] * lv[1:2], axis=-1, keepdims=True))
           - jnp.exp(jnp.sum(lv[2:3] * lv[3:4], axis=-1, keepdims=True)) + lam_init)
    g = g_ref[...]
    vaug_ref[:, 0:DA_V_DIM] = v_ref[...]
    vaug_ref[:, DA_V_DIM:2 * DA_V_DIM] = jnp.ones((l, DA_V_DIM), BF16)

    def rows(t):
        return pl.ds(pl.multiple_of(t * TOK_TILE, TOK_TILE), TOK_TILE)

    def probs(t, p_ref, nkeys=l):
        _attn_probs(q_ref[rows(t), :], k_ref, p_ref, nkeys)

    def values(t, p_ref, nkeys=l):
        o_ref[rows(t), :] = _attn_values(p_ref, vaug_ref, nkeys, lam, g, lam_init)

    tiles = list(range(nt)) if need_ctx else list(range(CTX_LEN // TOK_TILE, nt))
    if not need_ctx:
        o_ref[0:CTX_LEN, :] = jnp.zeros((CTX_LEN, DA_V_DIM), BF16)
    bufs = (pa_ref, pb_ref)
    keys = lambda t: CTX_LEN if t < CTX_LEN // TOK_TILE else l
    for idx, t in enumerate(tiles):
        probs(t, bufs[idx % 2], keys(t))
        if idx > 0:
            values(tiles[idx - 1], bufs[(idx - 1) % 2], keys(tiles[idx - 1]))
    values(tiles[-1], bufs[(len(tiles) - 1) % 2], keys(tiles[-1]))


FFN_WEIGHT_SHAPES = ((D_MODEL, D_MODEL), (D_MODEL, D_FF), (D_MODEL, D_FF), (D_FF, D_MODEL))


def _attention(q, k, v, lam_vec, subln_g, layer_idx, ffn_weights, need_ctx):
    b, l, _ = q.shape
    lam_init = 0.8 - 0.6 * math.exp(-0.3 * layer_idx)
    seq = pl.BlockSpec((None, l, DA_V_DIM), lambda i, h: (i, 0, h))
    w_in_specs, w_out_specs, w_out_shapes = [], [], []
    for rows, cols in FFN_WEIGHT_SHAPES:
        assert rows % b == 0 and (rows // b) % (2 * SUBLANES) == 0
        w_in_specs.append(pl.BlockSpec((None, rows // b, cols), lambda i, h: (layer_idx, i, 0)))
        w_out_specs.append(pl.BlockSpec((rows // b, cols), lambda i, h: (i, 0)))
        w_out_shapes.append(jax.ShapeDtypeStruct((rows, cols), BF16))
    res = pl.pallas_call(
        functools.partial(_attn_kernel, lam_init=lam_init, need_ctx=need_ctx),
        grid=(b, DA_HEADS),
        in_specs=[seq, seq, seq, _const_spec((4, DA_HEAD_DIM)), _const_spec((1, DA_V_DIM))] + w_in_specs,
        out_specs=[seq] + w_out_specs,
        out_shape=[jax.ShapeDtypeStruct((b, l, DA_WIDTH), BF16)] + w_out_shapes,
        scratch_shapes=[pltpu.VMEM((l, 2 * DA_V_DIM), BF16)] + [pltpu.VMEM((TOK_TILE, 2 * l), BF16)] * 2,
        compiler_params=_params("arbitrary", "arbitrary"),
    )(q, k, v, lam_vec, subln_g, *ffn_weights)
    return res[0], res[1:]


def _out_ffn_kernel(*refs, final):
    rg_ref, ssd_ref, da_ref, mod_ref, g2_ref, wo_ref, wg_ref, wu_ref, wd_ref, gf_ref, o_ref = refs[-11:]
    h_in = _load_stream(refs[:-11])
    g2 = g2_ref[...]
    nb = FFN_BATCH
    mods = [mod_ref[j] for j in range(nb)]
    mixes = []
    for j in range(nb):
        mix = jnp.dot(rg_ref[j], wo_ref[0:RG_WIDTH, :], preferred_element_type=F32)
        mix += jnp.dot(ssd_ref[j], wo_ref[RG_WIDTH:RG_WIDTH + SSD_WIDTH, :], preferred_element_type=F32)
        mix += jnp.dot(da_ref[j], wo_ref[RG_WIDTH + SSD_WIDTH:, :], preferred_element_type=F32)
        mixes.append(mix)
    hs = [h_in[j] + mods[j][2:3] * mixes[j] for j in range(nb)]
    ys = [(_rms(hs[j], g2) * (1.0 + mods[j][4:5]) + mods[j][3:4]).astype(BF16) for j in range(nb)]
    gates, ups = [], []
    for y in ys:
        gates.append(jnp.dot(y, wg_ref[...], preferred_element_type=F32))
        ups.append(jnp.dot(y, wu_ref[...], preferred_element_type=F32))
    acts = [(_silu(gates[j]) * ups[j]).astype(BF16) for j in range(nb)]
    downs = [jnp.dot(a, wd_ref[...], preferred_element_type=F32) for a in acts]
    for j in range(nb):
        h = hs[j] + mods[j][5:6] * downs[j]
        o_ref[j] = _rms(h, gf_ref[...]) if final else h


def _out_ffn(hs, rg, ssd, da, mod, norm2_g, w_out, w_gate, w_up, w_down, final_g, final):
    b, l, _ = rg.shape
    nb = FFN_BATCH
    assert b % nb == 0
    t0 = CTX_LEN // TOK_TILE if final else 0
    nt = l // TOK_TILE - t0
    tok = lambda w: pl.BlockSpec((nb, TOK_TILE, w), lambda t, i: (i, t + t0, 0))
    out_rows = nt * TOK_TILE
    kwargs = {"input_output_aliases": {0: 0}} if (len(hs) == 1 and not final) else {}
    return pl.pallas_call(
        functools.partial(_out_ffn_kernel, final=final),
        grid=(nt, b // nb),
        in_specs=_stream_specs(hs, b, t0, nb) + [
            tok(RG_WIDTH), tok(SSD_WIDTH), tok(DA_WIDTH),
            pl.BlockSpec((nb, 6, D_MODEL), lambda t, i: (jnp.where(t + t0 == 0, b // nb, i), 0, 0)),
            _const_spec((1, D_MODEL)),
            _const_spec((D_MODEL, D_MODEL)),
            _const_spec((D_MODEL, D_FF)),
            _const_spec((D_MODEL, D_FF)),
            _const_spec((D_FF, D_MODEL)),
            _const_spec((1, D_MODEL)),
        ],
        out_specs=pl.BlockSpec((nb, TOK_TILE, D_MODEL), lambda t, i: (i, t, 0)),
        out_shape=jax.ShapeDtypeStruct((b, out_rows, D_MODEL), F32),
        compiler_params=_params("arbitrary", "arbitrary"),
        **kwargs,
    )(*hs, rg, ssd, da, mod, norm2_g, w_out, w_gate, w_up, w_down, final_g)


def _rope_tables(n_lat):
    half = DA_HEAD_DIM // 2
    inv_freq = jnp.power(ROPE_BASE, -jnp.arange(0, half, 2, dtype=F32) / half)
    t = jnp.arange(n_lat, dtype=jnp.int32)
    ang_r = (t // GRID_W).astype(F32)[:, None] * inv_freq
    ang_c = (t % GRID_W).astype(F32)[:, None] * inv_freq
    cos = jnp.concatenate([jnp.cos(ang_r), jnp.cos(ang_r), jnp.cos(ang_c), jnp.cos(ang_c)], axis=1)
    sin = jnp.concatenate([-jnp.sin(ang_r), jnp.sin(ang_r), -jnp.sin(ang_c), jnp.sin(ang_c)], axis=1)
    reps = LANES // DA_HEAD_DIM
    cos = jnp.concatenate([jnp.ones((CTX_LEN, DA_HEAD_DIM), F32), cos], axis=0)
    sin = jnp.concatenate([jnp.zeros((CTX_LEN, DA_HEAD_DIM), F32), sin], axis=0)
    cos = jnp.tile(cos, (1, reps))
    sin = jnp.tile(sin, (1, reps))
    scale = DA_HEAD_DIM ** -0.5 * math.log2(math.e)
    return cos * scale, sin * scale, cos, sin


def _split_w_in(w):
    offs = [0]
    for size in IN_SIZES:
        offs.append(offs[-1] + size)
    w_qkv = w[:, :, offs[5]:offs[8]]
    w_rg = w[:, :, offs[0]:offs[3]]
    pad = jnp.zeros(w.shape[:2] + (LANES - IN_SIZES[4],), w.dtype)
    w_ssd = jnp.concatenate([w[:, :, offs[3]:offs[5]], pad], axis=2)
    return w_qkv.astype(BF16), w_rg.astype(BF16), w_ssd.astype(BF16)


def _rg_gate_params(w_a, b_a, w_x, b_x):
    w = jnp.stack([w_a, w_x], axis=2)
    eye = jnp.eye(RG_HEADS, dtype=w.dtype)
    dense = jnp.einsum("ldghij,hk->lhidgkj", w, eye)
    dense = dense.reshape(w.shape[0], RG_WIDTH, 4 * RG_WIDTH)
    bias = jnp.stack([b_a, b_x], axis=2).reshape(w.shape[0], 1, 4 * RG_WIDTH)
    return (0.5 * dense).astype(BF16), 0.5 * bias


def _lane_pad(v):
    return jnp.pad(v, ((0, 0), (0, LANES - v.shape[1])))[:, None, :]


def kernel(x, c, ctx, c_ctx, w_mod, b_mod, norm1_g, w_in, rg_conv_w, rg_conv_b, rg_w_a, rg_b_a, rg_w_x, rg_b_x, rg_lambda, ssd_conv_w, ssd_conv_b, ssd_dt_bias, ssd_a_log, ssd_d, ssd_norm_g, da_lambda, da_subln_g, w_out, norm2_g, w_gate, w_up, w_down, final_norm_g):
    b, n_lat, _ = x.shape
    assert ctx.shape[1] == CTX_LEN and n_lat % TOK_TILE == 0
    n_ctx = max(INPROJ_BATCH, FFN_BATCH)
    rows = -(-(b + n_ctx) // SUBLANES) * SUBLANES
    c_all = jnp.concatenate([c] + [c_ctx[None, :]] * n_ctx
                            + [jnp.zeros((rows - b - n_ctx, D_MODEL), F32)], axis=0)
    mods = _modulation(c_all, w_mod, b_mod)
    tables = _rope_tables(n_lat)
    hs = (ctx, x)
    w_qkv16, w_rg16, w_ssd16 = _split_w_in(w_in)
    w_gates, b_gates = _rg_gate_params(rg_w_a, rg_b_a, rg_w_x, rg_b_x)
    row = lambda p: p.reshape(DEPTH, 1, -1)
    dt_bias = _lane_pad(ssd_dt_bias.reshape(DEPTH, -1))
    aneg = _lane_pad(-jnp.exp(ssd_a_log.reshape(DEPTH, -1)))
    dskip = row(jnp.repeat(ssd_d, SSD_HEADDIM, axis=1))
    out = None
    for l in range(DEPTH):
        final = l == DEPTH - 1
        y, q, k, v = _in_proj(hs, mods[l], norm1_g[l][None, :], w_qkv16, l, tables)
        y_rg, u_ssd = _rglru(y, w_rg16, w_ssd16, l, rg_conv_w, row(rg_conv_b), w_gates, b_gates, rg_lambda)
        y_ssd = _ssd(u_ssd, l, ssd_conv_w, row(ssd_conv_b), dt_bias, aneg, dskip, row(ssd_norm_g))
        y_da, ffn16 = _attention(q, k, v, da_lambda[l], da_subln_g[l][None, :], l,
                                 (w_out, w_gate, w_up, w_down), not final)
        res = _out_ffn(hs, y_rg, y_ssd, y_da, mods[l], norm2_g[l][None, :], *ffn16,
                       final_norm_g[None, :], final)
        if final:
            out = res
        else:
            hs = (res,)
    return out
```
